```python
import math
import jax, jax.numpy as jnp
from jax import lax
import numpy as np

D_MODEL = 1024
BATCH = 8
SEQ = 2048
DEPTH = 2

D_MIX = D_MODEL
D_GROUP = D_MIX // 4
HEAD_DIM = 64
N_HEADS_GROUP = D_GROUP // HEAD_DIM

RWKV_DECAY_LORA = 32
RWKV_AAA_LORA = 32
RWKV_MV_LORA = 32
RWKV_GATE_LORA = 64
RWKV_GN_EPS = HEAD_DIM * 1e-5
RWKV_SIZES = (D_GROUP, D_GROUP, D_GROUP, RWKV_DECAY_LORA, RWKV_AAA_LORA, RWKV_GATE_LORA)
RWKV_COLS = sum(RWKV_SIZES)

DILATED_BRANCHES = ((128, 1), (512, 4), (2048, 16))
ALIBI_SLOPES = tuple(2.0 ** (-8.0 * (h + 1) / N_HEADS_GROUP) for h in range(N_HEADS_GROUP))
ATTN_COLS = 3 * D_GROUP

SSD_STATE = 128
SSD_GROUPS = 2
SSD_CONV = 4
SSD_CHUNK = 128
SSD_XBC = D_GROUP + 2 * SSD_GROUPS * SSD_STATE
SSD_COLS = D_GROUP + SSD_XBC + N_HEADS_GROUP

HGRN_CHUNK = 16
HGRN_COLS = 4 * D_GROUP

IN_COLS = RWKV_COLS + ATTN_COLS + SSD_COLS + HGRN_COLS

D_FF = 4 * D_MODEL
ALPHA = (2.0 * DEPTH) ** 0.25
BETA = (8.0 * DEPTH) ** -0.25
LN_EPS = 1e-5
RMS_EPS = 1e-5

kernel_name = "hymba_rwkv7_dilated_ssd_hgrn2_deepnorm"


def _split(t, sizes):
    out, o = [], 0
    for s in sizes:
        out.append(t[..., o:o + s])
        o += s
    return out


def _layer_norm(x, w, b):
    x32 = x.astype(jnp.float32)
    mu = jnp.mean(x32, -1, keepdims=True)
    var = jnp.mean(jnp.square(x32 - mu), -1, keepdims=True)
    return ((x32 - mu) * lax.rsqrt(var + LN_EPS) * w + b).astype(x.dtype)


def _rms(t):
    return t * lax.rsqrt(jnp.mean(jnp.square(t), -1, keepdims=True) + RMS_EPS)


def _token_shift_lerp(f, mu):
    prev = jnp.pad(f, ((0, 0), (1, 0), (0, 0)))[:, :-1]
    return f + (prev - f) * mu


def rwkv7_time_mix(feat, w0, w2, a0, a2, g2, k_k, k_a, r_k, lnx_w, lnx_b,
                   v_first, v_feat, v0, v2):
    bsz, slen, _ = feat.shape
    H, N = N_HEADS_GROUP, HEAD_DIM
    r, k, v, fw, fa, fg = _split(feat, RWKV_SIZES)
    w_log = -jax.nn.softplus(-(w0 + jnp.tanh(fw) @ w2)) - 0.5
    decay = jnp.exp(-jnp.exp(w_log))
    a = jax.nn.sigmoid(a0 + fa @ a2)
    g = jax.nn.sigmoid(fg) @ g2
    if v_first is None:
        v_first = v
    else:
        v = v + (v_first - v) * jax.nn.sigmoid(v0 + v_feat @ v2)

    def heads(t):
        return t.reshape(bsz, slen, H, N)

    kk = heads(k * k_k)
    kk = kk / jnp.maximum(jnp.sqrt(jnp.sum(jnp.square(kk), -1, keepdims=True)), 1e-12)
    k = k * (1.0 + (a - 1.0) * k_a)

    def step(state, inp):
        r_t, w_t, k_t, v_t, kk_t, a_t = inp
        sa = jnp.einsum('bhvk,bhk->bhv', state, -kk_t)
        state = (state * w_t[:, :, None, :]
                 + sa[..., None] * (kk_t * a_t)[:, :, None, :]
                 + v_t[..., None] * k_t[:, :, None, :])
        return state, jnp.einsum('bhvk,bhk->bhv', state, r_t)

    xs = tuple(jnp.swapaxes(heads(t), 0, 1) for t in (r, decay, k, v, kk, a))
    _, y = lax.scan(step, jnp.zeros((bsz, H, N, N), jnp.float32), xs)
    y = jnp.swapaxes(y, 0, 1)
    mu = jnp.mean(y, -1, keepdims=True)
    var = jnp.mean(jnp.square(y - mu), -1, keepdims=True)
    y = ((y - mu) * lax.rsqrt(var + RWKV_GN_EPS)).reshape(bsz, slen, D_GROUP) * lnx_w + lnx_b
    bonus = jnp.sum(heads(r) * heads(k) * r_k, -1, keepdims=True) * heads(v)
    return (y + bonus.reshape(bsz, slen, D_GROUP)) * g, v_first


def _dilated_branch(q, k, v, window, dilation):
    bsz, slen, H, Dh = q.shape
    L = slen // dilation
    blk = window // dilation
    nb = -(-L // blk)
    Lp = nb * blk

    def to_sub(t):
        t = t.reshape(bsz, L, dilation, H, Dh).transpose(0, 2, 1, 3, 4)
        t = t.reshape(bsz * dilation, L, H, Dh)
        return jnp.pad(t, ((0, 0), (0, Lp - L), (0, 0), (0, 0)))

    qs, ks, vs = to_sub(q), to_sub(k), to_sub(v)
    qb = qs.reshape(-1, nb, blk, H, Dh)

    def band(t):
        tp = jnp.pad(t, ((0, 0), (blk, 0), (0, 0), (0, 0))).reshape(-1, nb + 1, blk, H, Dh)
        return jnp.concatenate([tp[:, :-1], tp[:, 1:]], axis=2)

    kb, vb = band(ks), band(vs)
    i = jnp.arange(blk)[:, None]
    j = jnp.arange(2 * blk)[None, :]
    dist = blk + i - j
    n = jnp.arange(nb)[:, None, None]
    valid = (dist >= 0) & (dist <= blk) & ((n > 0) | (j >= blk))
    slopes = jnp.asarray(ALIBI_SLOPES, jnp.float32)
    bias = -slopes[:, None, None] * (dist * dilation).astype(jnp.float32)
    s = jnp.einsum('znqhd,znkhd->znhqk', qb, kb) * (Dh ** -0.5) + bias[None, None]
    s = jnp.where(valid[None, :, None], s, -jnp.inf)
    m = jnp.max(s, -1, keepdims=True)
    p = jnp.exp(s - m)
    l = jnp.sum(p, -1, keepdims=True)
    o = jnp.einsum('znhqk,znkhd->znqhd', p, vb) / jnp.transpose(l[..., 0], (0, 1, 3, 2))[..., None]
    lse = jnp.transpose((m + jnp.log(l))[..., 0], (0, 1, 3, 2))

    def from_sub(t):
        t = t.reshape(bsz * dilation, Lp, *t.shape[3:])[:, :L]
        t = t.reshape(bsz, dilation, L, *t.shape[2:])
        t = jnp.swapaxes(t, 1, 2)
        return t.reshape(bsz, slen, *t.shape[3:])

    return from_sub(o), from_sub(lse)


def dilated_attention(q, k, v):
    outs, lses = [], []
    for window, dilation in DILATED_BRANCHES:
        o, lse = _dilated_branch(q, k, v, window, dilation)
        outs.append(o)
        lses.append(lse)
    wts = jax.nn.softmax(jnp.stack(lses), axis=0)
    return jnp.sum(jnp.stack(outs) * wts[..., None], axis=0)


def _causal_depthwise_conv(x, w, b):
    y = lax.conv_general_dilated(x, w[:, None, :], window_strides=(1,),
                                 padding=[(w.shape[0] - 1, 0)],
                                 dimension_numbers=('NWC', 'WIO', 'NWC'),
                                 feature_group_count=x.shape[-1])
    return y + b


def _segsum_exp(a):
    cs = jnp.cumsum(a, -1)
    diff = cs[..., :, None] - cs[..., None, :]
    mask = jnp.tril(jnp.ones((a.shape[-1], a.shape[-1]), bool))
    return jnp.where(mask, jnp.exp(jnp.where(mask, diff, 0.0)), 0.0)


def ssd_mixer(feat, conv_w, conv_b, dt_bias, A_log, D, norm_w):
    bsz, slen, _ = feat.shape
    H, P, G, N = N_HEADS_GROUP, HEAD_DIM, SSD_GROUPS, SSD_STATE
    z, xbc, dt = _split(feat, (D_GROUP, SSD_XBC, H))
    xbc = jax.nn.silu(_causal_depthwise_conv(xbc, conv_w.astype(jnp.float32), conv_b))
    xs, Bm, Cm = _split(xbc, (D_GROUP, G * N, G * N))
    dt = jax.nn.softplus(dt + dt_bias)
    A = -jnp.exp(A_log.astype(jnp.float32))
    Lc = SSD_CHUNK
    nc = slen // Lc
    hpg = H // G
    x = xs.reshape(bsz, nc, Lc, H, P)
    Bh = jnp.repeat(Bm.reshape(bsz, nc, Lc, G, N), hpg, axis=3)
    Ch = jnp.repeat(Cm.reshape(bsz, nc, Lc, G, N), hpg, axis=3)
    dtc = dt.reshape(bsz, nc, Lc, H)
    dA = jnp.transpose(dtc * A, (0, 3, 1, 2))
    cs = jnp.cumsum(dA, -1)
    xdt = x * dtc[..., None]
    scores = jnp.einsum('bclhn,bcshn->bhcls', Ch, Bh) * _segsum_exp(dA)
    y_diag = jnp.einsum('bhcls,bcshp->bclhp', scores, xdt)
    decay_states = jnp.exp(cs[..., -1:] - cs)
    chunk_states = jnp.einsum('bclhn,bhcl,bclhp->cbhpn', Bh, decay_states, xdt)
    chunk_decay = jnp.transpose(jnp.exp(cs[..., -1]), (2, 0, 1))

    def step(s, inp):
        st, dec = inp
        return s * dec[..., None, None] + st, s

    _, prev = lax.scan(step, jnp.zeros((bsz, H, P, N), jnp.float32), (chunk_states, chunk_decay))
    y_off = jnp.einsum('bclhn,cbhpn,bhcl->bclhp', Ch, prev, jnp.exp(cs))
    y = (y_diag + y_off).reshape(bsz, slen, H, P) + x.reshape(bsz, slen, H, P) * D[:, None]
    y = y.reshape(bsz, slen, D_GROUP) * jax.nn.silu(z)
    y = _rms(y.reshape(bsz, slen, G, D_GROUP // G)).reshape(bsz, slen, D_GROUP)
    return y * norm_w


def hgrn2_mixer(feat, lower_bound, norm_w):
    bsz, slen, _ = feat.shape
    H, K, V = N_HEADS_GROUP, HEAD_DIM, HEAD_DIM
    q, f, i, g = _split(feat, (D_GROUP,) * 4)
    forget = lower_bound + (1.0 - lower_bound) * jax.nn.sigmoid(f)
    log_f = jnp.log(forget)
    k = 1.0 - forget
    q = jax.nn.silu(q)
    C = HGRN_CHUNK
    nc = slen // C
    q, k, log_f = (t.reshape(bsz, nc, C, H, K) for t in (q, k, log_f))
    v = i.reshape(bsz, nc, C, H, V)
    b = jnp.cumsum(log_f, axis=2)
    diff = b[:, :, :, None] - b[:, :, None, :]
    causal = jnp.tril(jnp.ones((C, C), bool))[None, None, :, :, None, None]
    dec = jnp.exp(jnp.where(causal, diff, -jnp.inf))
    att = jnp.einsum('bnthk,bnshk,bntshk->bnhts', q, k, dec)
    o_intra = jnp.einsum('bnhts,bnshv->bnthv', att, v)
    kdec = k * jnp.exp(b[:, :, -1:] - b)
    U = jnp.einsum('bnshk,bnshv->nbhkv', kdec, v)
    tot = jnp.transpose(jnp.exp(b[:, :, -1]), (1, 0, 2, 3))

    def step(s, inp):
        u, d = inp
        return s * d[..., None] + u, s

    _, prev = lax.scan(step, jnp.zeros((bsz, H, K, V), jnp.float32), (U, tot))
    o_inter = jnp.einsum('bnthk,nbhkv->bnthv', q * jnp.exp(b), prev)
    o = _rms((o_intra + o_inter).reshape(bsz, slen, H, V)).reshape(bsz, slen, D_GROUP)
    return o * norm_w * jax.nn.silu(g)


def setup_inputs(seed: int = 0) -> dict:
    key = jax.random.key(seed)
    ks = iter(jax.random.split(key, 40))
    nrm = lambda shape, scale: jax.random.normal(next(ks), shape, jnp.float32) * scale
    uni = lambda shape, lo, hi: jax.random.uniform(next(ks), shape, jnp.float32, lo, hi)
    L1 = DEPTH - 1
    H = N_HEADS_GROUP
    dt0 = jnp.exp(uni((DEPTH, H), math.log(1e-3), math.log(1e-1)))
    return {
        "x": nrm((BATCH, SEQ, D_MODEL), 1.0),
        "lower_bounds": nrm((DEPTH, D_GROUP), 0.5),
        "w_in": nrm((DEPTH, D_MODEL, IN_COLS), D_MODEL ** -0.5),
        "w_in_vres": nrm((L1, D_MODEL, RWKV_MV_LORA), D_MODEL ** -0.5),
        "mu_shift": uni((DEPTH, RWKV_COLS), 0.0, 1.0),
        "mu_vres": uni((L1, RWKV_MV_LORA), 0.0, 1.0),
        "rwkv_w0": uni((DEPTH, D_GROUP), -6.0, 1.0),
        "rwkv_w2": nrm((DEPTH, RWKV_DECAY_LORA, D_GROUP), RWKV_DECAY_LORA ** -0.5),
        "rwkv_a0": nrm((DEPTH, D_GROUP), 0.1),
        "rwkv_a2": nrm((DEPTH, RWKV_AAA_LORA, D_GROUP), RWKV_AAA_LORA ** -0.5),
        "rwkv_g2": nrm((DEPTH, RWKV_GATE_LORA, D_GROUP), RWKV_GATE_LORA ** -0.5),
        "rwkv_k_k": 0.85 + nrm((DEPTH, D_GROUP), 0.05),
        "rwkv_k_a": 1.0 + nrm((DEPTH, D_GROUP), 0.05),
        "rwkv_r_k": nrm((DEPTH, H, HEAD_DIM), 0.1),
        "rwkv_lnx_w": 1.0 + nrm((DEPTH, D_GROUP), 0.05),
        "rwkv_lnx_b": nrm((DEPTH, D_GROUP), 0.01),
        "rwkv_v0": nrm((L1, D_GROUP), 0.5),
        "rwkv_v2": nrm((L1, RWKV_MV_LORA, D_GROUP), RWKV_MV_LORA ** -0.5),
        "ssd_conv_w": nrm((DEPTH, SSD_CONV, SSD_XBC), SSD_CONV ** -0.5),
        "ssd_conv_b": nrm((DEPTH, SSD_XBC), 0.01),
        "ssd_dt_bias": dt0 + jnp.log(-jnp.expm1(-dt0)),
        "ssd_A_log": jnp.log(uni((DEPTH, H), 1.0, 16.0)),
        "ssd_D": 1.0 + nrm((DEPTH, H), 0.05),
        "ssd_norm_w": 1.0 + nrm((DEPTH, D_GROUP), 0.05),
        "hgrn_norm_w": 1.0 + nrm((DEPTH, D_GROUP), 0.05),
        "w_out": nrm((DEPTH, D_MIX, D_MODEL), BETA * D_MIX ** -0.5),
        "ln1_w": 1.0 + nrm((DEPTH, D_MODEL), 0.05),
        "ln1_b": nrm((DEPTH, D_MODEL), 0.01),
        "w_up": nrm((DEPTH, D_MODEL, D_FF), D_MODEL ** -0.5),
        "w_down": nrm((DEPTH, D_FF, D_MODEL), BETA * D_FF ** -0.5),
        "ln2_w": 1.0 + nrm((DEPTH, D_MODEL), 0.05),
        "ln2_b": nrm((DEPTH, D_MODEL), 0.01),
    }


def reference(x, lower_bounds, w_in, w_in_vres, mu_shift, mu_vres, rwkv_w0, rwkv_w2,
              rwkv_a0, rwkv_a2, rwkv_g2, rwkv_k_k, rwkv_k_a, rwkv_r_k, rwkv_lnx_w,
              rwkv_lnx_b, rwkv_v0, rwkv_v2, ssd_conv_w, ssd_conv_b, ssd_dt_bias, ssd_A_log,
              ssd_D, ssd_norm_w, hgrn_norm_w, w_out, ln1_w, ln1_b, w_up, w_down, ln2_w, ln2_b):
    bsz, slen, _ = x.shape
    lb = jax.nn.softmax(lower_bounds.astype(jnp.float32), axis=0)
    lb = jnp.cumsum(lb, axis=0) - lb[0]
    v_first = None
    for l in range(DEPTH):
        if l == 0:
            proj = x @ w_in[l]
        else:
            proj = x @ jnp.concatenate([w_in[l], w_in_vres[l - 1]], axis=1)
        proj = proj.astype(jnp.float32)
        parts = _split(proj, (RWKV_COLS, ATTN_COLS, SSD_COLS, HGRN_COLS))
        f_rwkv, f_attn, f_ssd, f_hgrn = parts
        f_rwkv = _token_shift_lerp(f_rwkv, mu_shift[l])
        if l == 0:
            y_a, v_first = rwkv7_time_mix(f_rwkv, rwkv_w0[l], rwkv_w2[l], rwkv_a0[l], rwkv_a2[l],
                                          rwkv_g2[l], rwkv_k_k[l], rwkv_k_a[l], rwkv_r_k[l],
                                          rwkv_lnx_w[l], rwkv_lnx_b[l], None, None, None, None)
        else:
            f_vres = _token_shift_lerp(proj[..., IN_COLS:], mu_vres[l - 1])
            y_a, v_first = rwkv7_time_mix(f_rwkv, rwkv_w0[l], rwkv_w2[l], rwkv_a0[l], rwkv_a2[l],
                                          rwkv_g2[l], rwkv_k_k[l], rwkv_k_a[l], rwkv_r_k[l],
                                          rwkv_lnx_w[l], rwkv_lnx_b[l], v_first, f_vres,
                                          rwkv_v0[l - 1], rwkv_v2[l - 1])
        q, k, v = (t.reshape(bsz, slen, N_HEADS_GROUP, HEAD_DIM)
                   for t in _split(f_attn, (D_GROUP,) * 3))
        y_b = dilated_attention(q, k, v).reshape(bsz, slen, D_GROUP)
        y_c = ssd_mixer(f_ssd, ssd_conv_w[l], ssd_conv_b[l], ssd_dt_bias[l], ssd_A_log[l],
                        ssd_D[l], ssd_norm_w[l])
        y_d = hgrn2_mixer(f_hgrn, lb[l], hgrn_norm_w[l])
        mix = jnp.concatenate([y_a, y_b, y_c, y_d], axis=-1).astype(x.dtype) @ w_out[l]
        x = _layer_norm(ALPHA * x + mix, ln1_w[l], ln1_b[l])
        h = jnp.square(jax.nn.relu(x @ w_up[l]))
        x = _layer_norm(ALPHA * x + h @ w_down[l], ln2_w[l], ln2_b[l])
    return x
```

```python
import functools
import math

import jax
import jax.numpy as jnp
from jax import lax
from jax.experimental import pallas as pl
from jax.experimental.pallas import tpu as pltpu

F32 = jnp.float32
BF16 = jnp.bfloat16

D_MODEL = 1024
DEPTH = 2
D_GROUP = 256
HEAD_DIM = 64
N_HEADS = 4
D_FF = 4 * D_MODEL
ALPHA = (2.0 * DEPTH) ** 0.25
LN_EPS = 1e-5
RMS_EPS = 1e-5
RWKV_GN_EPS = HEAD_DIM * 1e-5
RWKV_COLS = 896
ATTN_COLS = 768
SSD_STATE = 128
SSD_XBC = 768
SSD_COLS = 1028
HGRN_COLS = 1024
DILATIONS = (1, 4, 16)
ATTN_BLK = 128
ALIBI_SLOPES = tuple(2.0 ** (-8.0 * (h + 1) / N_HEADS) for h in range(N_HEADS))

RW_W, AT_W, SD_W, HG_W = 1024, 768, 1152, 1024
PROJ_W = RW_W + AT_W + SD_W + HG_W

RW_T = 64
SSD_T = 128
HG_C = 16
HG_T = 128
PROJ_TM = 512
POST_TM = 512
FF_CHUNK = 512

VMEM_LIMIT = 56 * 1024 * 1024


def _dot(a, b, dims):
    return lax.dot_general(a, b, (dims, ((), ())), preferred_element_type=F32)


_NN = ((1,), (0,))
_NT = ((1,), (1,))
_TN = ((0,), (0,))


def _mm(a, b, dims=_NN):
    return _dot(a.astype(BF16), b.astype(BF16), dims)


def _split3(x):
    hi = x.astype(BF16)
    r1 = x - hi.astype(F32)
    mid = r1.astype(BF16)
    lo = (r1 - mid.astype(F32)).astype(BF16)
    return hi, mid, lo


def _mmx_l(m01, x, dims=_NN):
    hi, mid, lo = _split3(x)
    return _dot(m01, hi, dims) + _dot(m01, mid, dims) + _dot(m01, lo, dims)


def _mmx_r(x, m01, dims=_NN):
    hi, mid, lo = _split3(x)
    return _dot(hi, m01, dims) + _dot(mid, m01, dims) + _dot(lo, m01, dims)


def _mm3(a, b, dims=_NN):
    ah = a.astype(BF16)
    al = (a - ah.astype(F32)).astype(BF16)
    bh = b.astype(BF16)
    bl = (b - bh.astype(F32)).astype(BF16)
    return _dot(ah, bh, dims) + _dot(ah, bl, dims) + _dot(al, bh, dims)


def _iota(shape, dim):
    return lax.broadcasted_iota(jnp.int32, shape, dim)


def _head_masks(width=D_GROUP):
    lane = _iota((1, width), 1) // HEAD_DIM
    return [(lane == h).astype(F32) for h in range(N_HEADS)]


def _same_head(n=D_GROUP):
    return (_iota((n, n), 0) // HEAD_DIM) == (_iota((n, n), 1) // HEAD_DIM)


def _sigmoid(x):
    return 1.0 / (1.0 + jnp.exp(-x))


def _silu(x):
    return x * _sigmoid(x)


def _softplus(x):
    return jnp.maximum(x, 0.0) + jnp.log(1.0 + jnp.exp(-jnp.abs(x)))


def _layer_norm(x, w, b):
    mu = jnp.mean(x, axis=-1, keepdims=True)
    d = x - mu
    var = jnp.mean(d * d, axis=-1, keepdims=True)
    return d * lax.rsqrt(var + LN_EPS) * w + b


def _shift_rows(x, prev_row):
    row = _iota((x.shape[0], 1), 0)
    return jnp.where(row == 0, prev_row, pltpu.roll(x, 1, axis=0))


def _proj_kernel(x_ref, w_ref, rw_ref, at_ref, sd_ref, hg_ref):
    xb = x_ref[...].astype(BF16)
    off = 0
    for ref, width in ((rw_ref, RW_W), (at_ref, AT_W), (sd_ref, SD_W), (hg_ref, HG_W)):
        for c0 in range(0, width, 256):
            cw = min(256, width - c0)
            ref[:, c0:c0 + cw] = jnp.dot(xb, w_ref[:, off + c0:off + c0 + cw],
                                         preferred_element_type=F32)
        off += width


def _project(x2d, w_cat):
    m = x2d.shape[0]
    outs = [jax.ShapeDtypeStruct((m, w), F32) for w in (RW_W, AT_W, SD_W, HG_W)]
    return pl.pallas_call(
        _proj_kernel,
        grid=(m // PROJ_TM,),
        in_specs=[pl.BlockSpec((PROJ_TM, D_MODEL), lambda i: (i, 0)),
                  pl.BlockSpec((D_MODEL, PROJ_W), lambda i: (0, 0), pipeline_mode=pl.Buffered(1))],
        out_specs=[pl.BlockSpec((PROJ_TM, w), lambda i: (i, 0)) for w in (RW_W, AT_W, SD_W, HG_W)],
        out_shape=outs,
        compiler_params=pltpu.CompilerParams(dimension_semantics=("parallel",),
                                             vmem_limit_bytes=VMEM_LIMIT),
    )(x2d, w_cat)


def _stack_heads(x, masks):
    return jnp.concatenate([x * m for m in masks], axis=0)


def _unstack_heads(xs, t):
    out = xs[0:t]
    for h in range(1, N_HEADS):
        out = out + xs[h * t:(h + 1) * t]
    return out


def _rwkv_kernel(has_vres, *refs):
    if has_vres:
        (f_ref, fprev_ref, vf_ref, mu_ref, vec_ref, w2_ref, a2_ref, g2_ref, v2_ref,
         y_ref, s_ref) = refs
    else:
        (f_ref, fprev_ref, mu_ref, vec_ref, w2_ref, a2_ref, g2_ref,
         y_ref, vout_ref, s_ref) = refs
    c = pl.program_id(1)
    T = RW_T
    G = D_GROUP

    @pl.when(c == 0)
    def _():
        s_ref[...] = jnp.zeros_like(s_ref)

    f = f_ref[...]
    prev_row = jnp.where(c == 0, 0.0, fprev_ref[7:8, :])
    x = f + (_shift_rows(f, prev_row) - f) * mu_ref[...]
    r = x[:, 0:G]
    k = x[:, G:2 * G]
    v = x[:, 2 * G:3 * G]
    seg = x[:, 3 * G:3 * G + 128]
    w0, a0, k_k, k_a = (vec_ref[i:i + 1, :] for i in range(4))
    lnx_w, lnx_b, r_k = (vec_ref[i:i + 1, :] for i in range(4, 7))

    p = w0 + _mm3(jnp.tanh(seg), w2_ref[...])
    lw = -math.exp(-0.5) * _sigmoid(p)
    a = _sigmoid(a0 + _mm3(seg, a2_ref[...]))
    g = _mm3(_sigmoid(seg), g2_ref[...])
    if has_vres:
        seg2 = x[:, 3 * G + 128:3 * G + 256]
        v0 = vec_ref[7:8, :]
        v = v + (vf_ref[...] - v) * _sigmoid(v0 + _mm3(seg2, v2_ref[...]))
    else:
        vout_ref[...] = v

    masks = _head_masks()
    same = _same_head()
    bo = same.astype(BF16)
    kk = k * k_k
    kk = kk / jnp.maximum(jnp.sqrt(_mmx_r(kk * kk, bo)), 1e-12)
    k2 = k * (1.0 + (a - 1.0) * k_a)
    kka = kk * a

    ltri = (_iota((T, T), 1) <= _iota((T, T), 0)).astype(BF16)
    cs = _mmx_l(ltri, lw)
    c_last = cs[T - 1:T, :]
    e_neg = jnp.exp(-cs)
    e_dec = jnp.exp(c_last - cs)
    at = -kk * jnp.exp(cs - lw)
    bh = kka * e_neg
    kh = k2 * e_neg
    rt = r * jnp.exp(cs)
    bdec = kka * e_dec
    kdec = k2 * e_dec

    at_s = _stack_heads(at, masks)
    lhs = jnp.concatenate([at_s, _stack_heads(rt, masks)], axis=0)
    rhs = jnp.concatenate([_stack_heads(bh, masks), _stack_heads(kh, masks)], axis=0)
    aa = _mm(lhs, rhs, _NT)
    rr = _iota((G, G), 0) % T
    cc = _iota((G, G), 1) % T
    strict = cc < rr
    incl = cc <= rr
    a_ab = jnp.where(strict, aa[0:G, 0:G], 0.0)
    a_ak = jnp.where(strict, aa[0:G, G:2 * G], 0.0)
    m_rb = jnp.where(incl, aa[G:2 * G, 0:G], 0.0)
    m_rk = jnp.where(incl, aa[G:2 * G, G:2 * G], 0.0)

    eye = (_iota((G, G), 0) == _iota((G, G), 1)).astype(F32)
    tinv = eye + a_ab
    pw = a_ab
    for _ in range(5):
        pw = _mm(pw, pw)
        tinv = tinv + _mm(tinv, pw)

    v_s = _stack_heads(v, masks)
    pq = _mm(tinv, jnp.concatenate([at_s, _mm(a_ak, v_s)], axis=1))
    p_s = pq[:, 0:G]
    q_s = pq[:, G:2 * G]
    r_eff = rt + _unstack_heads(_mm(m_rb, p_s), T)
    o_par = _unstack_heads(_mm(m_rb, q_s) + _mm(m_rk, v_s), T)
    p_u = _unstack_heads(p_s, T)
    q_u = _unstack_heads(q_s, T)

    s = s_ref[...]
    o = _mm(r_eff, s, _NT) + o_par
    gb = jnp.where(same, _mm(bdec, p_u, _TN), 0.0)
    ut = jnp.where(same, _mm(q_u, bdec, _TN) + _mm(v, kdec, _TN), 0.0)
    s_ref[...] = s * jnp.exp(c_last) + _mm(s, gb, _NT) + ut

    mean = _mmx_r(o, bo) * (1.0 / HEAD_DIM)
    d = o - mean
    var = _mmx_r(d * d, bo) * (1.0 / HEAD_DIM)
    yn = d * lax.rsqrt(var + RWKV_GN_EPS) * lnx_w + lnx_b
    bonus = _mmx_r(r * k2 * r_k, bo) * v
    y_ref[...] = (yn + bonus) * g


def _rwkv(rw, v_first, mu, vec, w2p, a2p, g2p, v2p, batch, seq):
    has_vres = v_first is not None
    nc = seq // RW_T
    m = batch * seq
    row_spec = lambda w: pl.BlockSpec((RW_T, w), lambda b, c: (b * nc + c, 0))
    const = lambda shape: pl.BlockSpec(shape, lambda b, c: (0, 0))
    prev_spec = pl.BlockSpec(
        (8, RW_W), lambda b, c: (jnp.maximum(b * (seq // 8) + c * (RW_T // 8) - 1, 0), 0))
    in_specs = [row_spec(RW_W), prev_spec]
    args = [rw, rw]
    if has_vres:
        in_specs.append(row_spec(D_GROUP))
        args.append(v_first)
    in_specs += [const((1, RW_W)), const((8, D_GROUP)), const((128, D_GROUP)),
                 const((128, D_GROUP)), const((128, D_GROUP))]
    args += [mu, vec, w2p, a2p, g2p]
    if has_vres:
        in_specs.append(const((128, D_GROUP)))
        args.append(v2p)
    out_shape = [jax.ShapeDtypeStruct((m, D_GROUP), F32)]
    out_specs = [row_spec(D_GROUP)]
    if not has_vres:
        out_shape.append(jax.ShapeDtypeStruct((m, D_GROUP), F32))
        out_specs.append(row_spec(D_GROUP))
    outs = pl.pallas_call(
        functools.partial(_rwkv_kernel, has_vres),
        grid=(batch, nc),
        in_specs=in_specs,
        out_specs=out_specs,
        out_shape=out_shape,
        scratch_shapes=[pltpu.VMEM((D_GROUP, D_GROUP), F32)],
        compiler_params=pltpu.CompilerParams(dimension_semantics=("parallel", "arbitrary"),
                                             vmem_limit_bytes=VMEM_LIMIT),
    )(*args)
    if has_vres:
        return outs[0], v_first
    return outs[0], outs[1]


def _attn_kernel(qkv_ref, o_ref, perm_ref, acc_ref, l_ref, m_ref):
    G = D_GROUP
    blk = ATTN_BLK
    seq = qkv_ref.shape[0]
    nres = DILATIONS[-1]
    sub = seq // nres
    masks = _head_masks()
    scale = HEAD_DIM ** -0.5

    def jmap(a, d):
        run = blk * d // nres
        return (nres // d) * (a % run) + a // run

    to_perm = (_iota((blk, blk), 1) == jmap(_iota((blk, blk), 0), 1)).astype(BF16)
    from_perm = (_iota((blk, blk), 0) == jmap(_iota((blk, blk), 1), 1)).astype(BF16)
    run1 = blk // nres

    def permute(n, carry):
        r0 = pl.multiple_of(n * blk, blk)
        pb = jnp.dot(to_perm, qkv_ref[pl.ds(r0, blk), :].astype(BF16), preferred_element_type=F32)
        for e in range(nres):
            dst = pl.multiple_of(e * sub + n * run1, run1)
            perm_ref[pl.ds(dst, run1), :] = pb[e * run1:(e + 1) * run1, :]
        return carry

    lax.fori_loop(0, seq // blk, permute, 0)

    for bi, d in enumerate(reversed(DILATIONS)):
        runs = nres // d
        run = blk // runs
        nb = seq // (d * blk)
        has_prev = nb > 1
        nk = 2 * blk if has_prev else blk
        jq = jmap(_iota((blk, nk), 0), d)
        kb = _iota((blk, nk), 1)
        jk = jmap(kb % blk, d) + (kb // blk) * blk
        dist = (blk if has_prev else 0) + jq - jk
        in_band = (dist >= 0) & (dist <= blk)
        cur_only = kb >= blk
        distf = dist.astype(F32)
        first = bi == 0
        last = bi == len(DILATIONS) - 1

        def body(it, carry, d=d, runs=runs, run=run, nb=nb, has_prev=has_prev, in_band=in_band,
                 cur_only=cur_only, distf=distf, first=first, last=last):
            rho = it // nb
            n = it % nb

            def starts(nn):
                return [pl.multiple_of((rho + d * e) * sub + nn * run, run) for e in range(runs)]

            def gather(st, c0):
                return jnp.concatenate([perm_ref[pl.ds(s0, run), c0:c0 + G] for s0 in st], axis=0)

            cur = starts(n)
            q = gather(cur, 0)
            if has_prev:
                prv = starts(jnp.maximum(n - 1, 0))
                kcat = jnp.concatenate([gather(prv, G), gather(cur, G)], axis=0)
                vcat = jnp.concatenate([gather(prv, 2 * G), gather(cur, 2 * G)], axis=0)
                valid = in_band & ((n > 0) | cur_only)
            else:
                kcat = gather(cur, G)
                vcat = gather(cur, 2 * G)
                valid = in_band
            kcat = kcat.astype(BF16)
            acc_b = jnp.zeros((blk, G), F32)
            l_b = jnp.zeros((blk, G), F32)
            m_b = jnp.zeros((blk, G), F32)
            for h in range(N_HEADS):
                s = _dot((q * masks[h]).astype(BF16), kcat, _NT) * scale
                s = s - distf * (ALIBI_SLOPES[h] * d)
                s = jnp.where(valid, s, -jnp.inf)
                mh = jnp.max(s, axis=-1, keepdims=True)
                ph = jnp.exp(s - mh)
                lh = jnp.sum(ph, axis=-1, keepdims=True)
                acc_b = acc_b + _mm(ph, vcat * masks[h])
                l_b = l_b + lh * masks[h]
                m_b = m_b + mh * masks[h]
            outs = []
            for e, s0 in enumerate(cur):
                idx = pl.ds(s0, run)
                sl = slice(e * run, (e + 1) * run)
                if first:
                    acc_ref[idx, :] = acc_b[sl]
                    l_ref[idx, :] = l_b[sl]
                    m_ref[idx, :] = m_b[sl]
                else:
                    m_old = m_ref[idx, :]
                    m_new = jnp.maximum(m_old, m_b[sl])
                    w_old = jnp.exp(m_old - m_new)
                    w_new = jnp.exp(m_b[sl] - m_new)
                    acc_n = acc_ref[idx, :] * w_old + acc_b[sl] * w_new
                    l_n = l_ref[idx, :] * w_old + l_b[sl] * w_new
                    if last:
                        outs.append(acc_n / l_n)
                    else:
                        acc_ref[idx, :] = acc_n
                        l_ref[idx, :] = l_n
                        m_ref[idx, :] = m_new
            if last:
                r0 = pl.multiple_of(n * blk, blk)
                o_ref[pl.ds(r0, blk), :] = _mmx_l(from_perm, jnp.concatenate(outs, axis=0))
            return carry

        lax.fori_loop(0, d * nb, body, 0)


def _attention(at, batch, seq):
    m = batch * seq
    return pl.pallas_call(
        _attn_kernel,
        grid=(batch,),
        in_specs=[pl.BlockSpec((seq, AT_W), lambda b: (b, 0))],
        out_specs=pl.BlockSpec((seq, D_GROUP), lambda b: (b, 0)),
        out_shape=jax.ShapeDtypeStruct((m, D_GROUP), F32),
        scratch_shapes=[pltpu.VMEM((seq, AT_W), F32)] + [pltpu.VMEM((seq, D_GROUP), F32)] * 3,
        compiler_params=pltpu.CompilerParams(dimension_semantics=("parallel",),
                                             vmem_limit_bytes=VMEM_LIMIT),
    )(at)


def _ssd_kernel(f_ref, fprev_ref, cw_ref, cb_ref, dtb_ref, vec_ref, ex_ref, y_ref, st_ref, pad_ref):
    c = pl.program_id(1)
    T = SSD_T
    G = D_GROUP
    N = SSD_STATE

    @pl.when(c == 0)
    def _():
        st_ref[...] = jnp.zeros_like(st_ref)

    xbc_raw = f_ref[:, G:G + SSD_XBC]
    pad_ref[0:8, :] = jnp.where(c == 0, 0.0, fprev_ref[:, G:G + SSD_XBC])
    pad_ref[8:8 + T, :] = xbc_raw
    conv = xbc_raw * cw_ref[3:4, :] + cb_ref[...]
    for j in range(1, 4):
        conv = conv + pad_ref[8 - j:8 - j + T, :] * cw_ref[3 - j:4 - j, :]
    xbc = _silu(conv)
    xs = xbc[:, 0:G]
    bm = xbc[:, G:G + 2 * N]
    cm = xbc[:, G + 2 * N:G + 4 * N]
    z = f_ref[:, 0:G]

    a_dense, d_dense, norm_w = (vec_ref[i:i + 1, :] for i in range(3))
    dt = _softplus(f_ref[:, G + SSD_XBC:G + SSD_XBC + 128] + dtb_ref[...])
    dt_dense = _mmx_r(dt, ex_ref[...])
    da = dt_dense * a_dense
    ltri = (_iota((T, T), 1) <= _iota((T, T), 0))
    ltri_b = ltri.astype(BF16)
    cs = _mmx_l(ltri_b, da)
    cs_last = cs[T - 1:T, :]
    xdt = xs * dt_dense
    umat = (_iota((T, T), 0) > _iota((T, T), 1)).astype(F32)

    masks = _head_masks()
    y_diag = jnp.zeros((T, G), F32)
    scores = [_mm(cm[:, g * N:(g + 1) * N], bm[:, g * N:(g + 1) * N], _NT) for g in range(2)]
    for h in range(N_HEADS):
        da_col = jnp.broadcast_to(da[:, h * HEAD_DIM:h * HEAD_DIM + 1], (T, T))
        seg = _mmx_l(ltri_b, da_col * umat)
        dec = jnp.where(ltri, jnp.exp(jnp.where(ltri, seg, 0.0)), 0.0)
        y_diag = y_diag + _mm(scores[h // 2] * dec, xdt * masks[h])

    st = st_ref[...]
    y_off = jnp.concatenate([_mm(cm[:, 0:N], st[:, 0:N]), _mm(cm[:, N:2 * N], st[:, N:2 * N])],
                            axis=1) * jnp.exp(cs)
    xd = xdt * jnp.exp(cs_last - cs)
    st_ref[...] = st * jnp.exp(cs_last) + jnp.concatenate(
        [_mm(bm[:, 0:N], xd[:, 0:N], _TN), _mm(bm[:, N:2 * N], xd[:, N:2 * N], _TN)], axis=1)

    y = (y_diag + y_off + xs * d_dense) * _silu(z)
    halves = []
    for g in range(2):
        yg = y[:, g * N:(g + 1) * N]
        halves.append(yg * lax.rsqrt(jnp.mean(yg * yg, axis=-1, keepdims=True) + RMS_EPS))
    y_ref[...] = jnp.concatenate(halves, axis=1) * norm_w


def _ssd(sd, cw, cb, dtb, vec, ex, batch, seq):
    nc = seq // SSD_T
    m = batch * seq
    const = lambda shape: pl.BlockSpec(shape, lambda b, c: (0, 0))
    return pl.pallas_call(
        _ssd_kernel,
        grid=(batch, nc),
        in_specs=[pl.BlockSpec((SSD_T, SD_W), lambda b, c: (b * nc + c, 0)),
                  pl.BlockSpec((8, SD_W),
                               lambda b, c: (jnp.maximum(b * (seq // 8) + c * (SSD_T // 8) - 1, 0), 0)),
                  const((4, SSD_XBC)), const((1, SSD_XBC)), const((1, 128)), const((8, D_GROUP)),
                  const((128, D_GROUP))],
        out_specs=pl.BlockSpec((SSD_T, D_GROUP), lambda b, c: (b * nc + c, 0)),
        out_shape=jax.ShapeDtypeStruct((m, D_GROUP), F32),
        scratch_shapes=[pltpu.VMEM((SSD_STATE, D_GROUP), F32),
                        pltpu.VMEM((8 + SSD_T, SSD_XBC), F32)],
        compiler_params=pltpu.CompilerParams(dimension_semantics=("parallel", "arbitrary"),
                                             vmem_limit_bytes=VMEM_LIMIT),
    )(sd, sd, cw, cb, dtb, vec, ex)


def _hgrn_kernel(f_ref, vec_ref, y_ref, st_ref, q_s, k_s, b_s, o_s):
    c = pl.program_id(1)
    T = HG_T
    C = HG_C
    G = D_GROUP

    @pl.when(c == 0)
    def _():
        st_ref[...] = jnp.zeros_like(st_ref)

    lb = vec_ref[0:1, :]
    norm_w = vec_ref[1:2, :]
    forget = lb + (1.0 - lb) * _sigmoid(f_ref[:, G:2 * G])
    q_s[...] = _silu(f_ref[:, 0:G])
    k_s[...] = 1.0 - forget
    blk_tri = ((_iota((T, T), 0) // C == _iota((T, T), 1) // C)
               & (_iota((T, T), 1) <= _iota((T, T), 0))).astype(BF16)
    b_s[...] = _mmx_l(blk_tri, jnp.log(forget))

    same = _same_head()
    bo = same.astype(BF16)
    t_idx = _iota((C, 1), 0)

    def body(i, carry):
        r0 = pl.multiple_of(i * C, C)
        bq = b_s[pl.ds(r0, C), :]
        qq = q_s[pl.ds(r0, C), :]
        kq = k_s[pl.ds(r0, C), :]
        vv = f_ref[pl.ds(r0, C), 2 * G:3 * G]
        parts = []
        for s in range(C):
            e = jnp.exp(jnp.minimum(bq - bq[s:s + 1, :], 0.0))
            parts.append(jnp.where(t_idx >= s, qq * e * kq[s:s + 1, :], 0.0))
        z = _mm(jnp.concatenate(parts, axis=0), bo)
        o = _mm(qq * jnp.exp(bq), st_ref[...], _NT)
        for s in range(C):
            o = o + z[s * C:(s + 1) * C, :] * vv[s:s + 1, :]
        o_s[pl.ds(r0, C), :] = o
        b_last = bq[C - 1:C, :]
        upd = _mm(vv, kq * jnp.exp(b_last - bq), _TN)
        st_ref[...] = st_ref[...] * jnp.exp(b_last) + jnp.where(same, upd, 0.0)
        return carry

    lax.fori_loop(0, T // C, body, 0)
    o = o_s[...]
    ms = _mmx_r(o * o, bo) * (1.0 / HEAD_DIM)
    y_ref[...] = o * lax.rsqrt(ms + RMS_EPS) * norm_w * _silu(f_ref[:, 3 * G:4 * G])


def _hgrn(hg, vec, batch, seq):
    nc = seq // HG_T
    m = batch * seq
    return pl.pallas_call(
        _hgrn_kernel,
        grid=(batch, nc),
        in_specs=[pl.BlockSpec((HG_T, HG_W), lambda b, c: (b * nc + c, 0)),
                  pl.BlockSpec((8, D_GROUP), lambda b, c: (0, 0))],
        out_specs=pl.BlockSpec((HG_T, D_GROUP), lambda b, c: (b * nc + c, 0)),
        out_shape=jax.ShapeDtypeStruct((m, D_GROUP), F32),
        scratch_shapes=[pltpu.VMEM((D_GROUP, D_GROUP), F32)] + [pltpu.VMEM((HG_T, D_GROUP), F32)] * 4,
        compiler_params=pltpu.CompilerParams(dimension_semantics=("parallel", "arbitrary"),
                                             vmem_limit_bytes=VMEM_LIMIT),
    )(hg, vec)


def _post_kernel(x_ref, ya_ref, yb_ref, yc_ref, yd_ref, wo_ref, wu_ref, wd_ref, ln_ref, o_ref):
    x = x_ref[...]
    mix = jnp.zeros(x.shape, F32)
    for i, ref in enumerate((ya_ref, yb_ref, yc_ref, yd_ref)):
        mix = mix + jnp.dot(ref[...].astype(BF16), wo_ref[i * D_GROUP:(i + 1) * D_GROUP, :],
                            preferred_element_type=F32)
    x1 = _layer_norm(ALPHA * x + mix, ln_ref[0:1, :], ln_ref[1:2, :])
    xb = x1.astype(BF16)
    acc = ALPHA * x1
    for c0 in range(0, D_FF, FF_CHUNK):
        h = jnp.maximum(jnp.dot(xb, wu_ref[:, c0:c0 + FF_CHUNK], preferred_element_type=F32), 0.0)
        acc = acc + jnp.dot((h * h).astype(BF16), wd_ref[c0:c0 + FF_CHUNK, :],
                            preferred_element_type=F32)
    o_ref[...] = _layer_norm(acc, ln_ref[2:3, :], ln_ref[3:4, :])


def _post(x2d, ys, wo, wu, wd, ln):
    m = x2d.shape[0]
    row = lambda w: pl.BlockSpec((POST_TM, w), lambda i: (i, 0))
    res = lambda shape: pl.BlockSpec(shape, lambda i: (0, 0), pipeline_mode=pl.Buffered(1))
    return pl.pallas_call(
        _post_kernel,
        grid=(m // POST_TM,),
        in_specs=[row(D_MODEL)] + [row(D_GROUP)] * 4
        + [res((D_MODEL, D_MODEL)), res((D_MODEL, D_FF)), res((D_FF, D_MODEL)), res((8, D_MODEL))],
        out_specs=row(D_MODEL),
        out_shape=jax.ShapeDtypeStruct((m, D_MODEL), F32),
        compiler_params=pltpu.CompilerParams(dimension_semantics=("parallel",),
                                             vmem_limit_bytes=VMEM_LIMIT),
    )(x2d, *ys, wo, wu, wd, ln)


def _pad_cols(a, width):
    return jnp.pad(a, ((0, 0), (0, width - a.shape[1])))


def _pad_rows(a, top, total):
    return jnp.pad(a, ((top, total - top - a.shape[0]), (0, 0)))


def _rep_heads(v):
    return jnp.repeat(v.astype(F32), HEAD_DIM)[None, :]


def _rows8(rows, width):
    out = jnp.concatenate([r.reshape(1, width).astype(F32) for r in rows], axis=0)
    return jnp.pad(out, ((0, 8 - out.shape[0]), (0, 0)))


def kernel(x, lower_bounds, w_in, w_in_vres, mu_shift, mu_vres, rwkv_w0, rwkv_w2, rwkv_a0, rwkv_a2, rwkv_g2, rwkv_k_k, rwkv_k_a, rwkv_r_k, rwkv_lnx_w, rwkv_lnx_b, rwkv_v0, rwkv_v2, ssd_conv_w, ssd_conv_b, ssd_dt_bias, ssd_A_log, ssd_D, ssd_norm_w, hgrn_norm_w, w_out, ln1_w, ln1_b, w_up, w_down, ln2_w, ln2_b):
    batch, seq, _ = x.shape
    lb = jax.nn.softmax(lower_bounds.astype(F32), axis=0)
    lb = jnp.cumsum(lb, axis=0) - lb[0]
    expand = (jnp.arange(128)[:, None] == (jnp.arange(D_GROUP)[None, :] // HEAD_DIM)).astype(BF16)

    h = x.reshape(batch * seq, D_MODEL)
    v_first = None
    for l in range(DEPTH):
        wl = w_in[l]
        o1 = RWKV_COLS
        o2 = o1 + ATTN_COLS
        o3 = o2 + SSD_COLS
        rw_cols = wl[:, :o1]
        mu = mu_shift[l][None, :]
        if l > 0:
            rw_cols = jnp.concatenate([rw_cols, w_in_vres[l - 1]], axis=1)
            mu = jnp.concatenate([mu, mu_vres[l - 1][None, :]], axis=1)
        w_cat = jnp.concatenate([_pad_cols(rw_cols, RW_W), wl[:, o1:o2], _pad_cols(wl[:, o2:o3], SD_W),
                                 wl[:, o3:]], axis=1).astype(BF16)
        rw, at, sd, hg = _project(h, w_cat)

        vec_rows = [rwkv_w0[l], rwkv_a0[l], rwkv_k_k[l], rwkv_k_a[l], rwkv_lnx_w[l], rwkv_lnx_b[l],
                    rwkv_r_k[l].reshape(-1)]
        if l > 0:
            vec_rows.append(rwkv_v0[l - 1])
        y_a, v_first = _rwkv(
            rw, v_first, _pad_cols(mu, RW_W), _rows8(vec_rows, D_GROUP),
            _pad_rows(rwkv_w2[l], 0, 128), _pad_rows(rwkv_a2[l], 32, 128), _pad_rows(rwkv_g2[l], 64, 128),
            _pad_rows(rwkv_v2[l - 1], 0, 128) if l > 0 else None, batch, seq)
        y_b = _attention(at, batch, seq)
        ssd_vec = _rows8([_rep_heads(-jnp.exp(ssd_A_log[l].astype(F32))), _rep_heads(ssd_D[l]),
                          ssd_norm_w[l]], D_GROUP)
        y_c = _ssd(sd, ssd_conv_w[l].astype(F32), ssd_conv_b[l][None, :],
                   _pad_cols(ssd_dt_bias[l][None, :], 128), ssd_vec, expand, batch, seq)
        y_d = _hgrn(hg, _rows8([lb[l], hgrn_norm_w[l]], D_GROUP), batch, seq)
        ln = _rows8([ln1_w[l], ln1_b[l], ln2_w[l], ln2_b[l]], D_MODEL)
        h = _post(h, (y_a, y_b, y_c, y_d), w_out[l].astype(BF16), w_up[l].astype(BF16),
                  w_down[l].astype(BF16), ln)
    return h.reshape(batch, seq, D_MODEL)
```

```python
import functools
import math

import jax
import jax.numpy as jnp
from jax import lax
from jax.experimental import pallas as pl
from jax.experimental.pallas import tpu as pltpu

F32 = jnp.float32
BF16 = jnp.bfloat16

D_MODEL = 1024
DEPTH = 2
D_GROUP = 256
HEAD_DIM = 64
N_HEADS = 4
D_FF = 4 * D_MODEL
ALPHA = (2.0 * DEPTH) ** 0.25
LN_EPS = 1e-5
RMS_EPS = 1e-5
RWKV_GN_EPS = HEAD_DIM * 1e-5
RWKV_COLS = 896
ATTN_COLS = 768
SSD_STATE = 128
SSD_XBC = 768
SSD_COLS = 1028
HGRN_COLS = 1024
DILATIONS = (1, 4, 16)
ATTN_BLK = 128
ALIBI_SLOPES = tuple(2.0 ** (-8.0 * (h + 1) / N_HEADS) for h in range(N_HEADS))

RW_W, AT_W, SD_W, HG_W = 1024, 768, 1152, 1024
PROJ_W = RW_W + AT_W + SD_W + HG_W

RW_T = 64
RW_NCH = 4
RW_ROWS = RW_T * RW_NCH
SSD_T = 128
HG_C = 16
HG_T = 128
PROJ_TM = 512
POST_TM = 512
FF_CHUNK = 512

VMEM_LIMIT = 56 * 1024 * 1024


def _dot(a, b, dims):
    return lax.dot_general(a, b, (dims, ((), ())), preferred_element_type=F32)


_NN = ((1,), (0,))
_NT = ((1,), (1,))
_TN = ((0,), (0,))


def _mm(a, b, dims=_NN):
    return _dot(a.astype(BF16), b.astype(BF16), dims)


def _split(x, pieces):
    out = []
    for i in range(pieces):
        hi = x.astype(BF16)
        out.append(hi)
        if i + 1 < pieces:
            x = x - hi.astype(F32)
    return out


def _mmx_l(m01, x, dims=_NN, pieces=3):
    return sum(_dot(m01, xp, dims) for xp in _split(x, pieces))


def _mmx_r(x, m01, dims=_NN, pieces=3):
    return sum(_dot(xp, m01, dims) for xp in _split(x, pieces))


def _mm3(a, b, dims=_NN):
    ah = a.astype(BF16)
    al = (a - ah.astype(F32)).astype(BF16)
    bh = b.astype(BF16)
    bl = (b - bh.astype(F32)).astype(BF16)
    return _dot(ah, bh, dims) + _dot(ah, bl, dims) + _dot(al, bh, dims)


def _iota(shape, dim):
    return lax.broadcasted_iota(jnp.int32, shape, dim)


def _head_masks(width=D_GROUP):
    lane = _iota((1, width), 1) // HEAD_DIM
    return [(lane == h).astype(F32) for h in range(N_HEADS)]


def _same_head(n=D_GROUP):
    return (_iota((n, n), 0) // HEAD_DIM) == (_iota((n, n), 1) // HEAD_DIM)


def _sigmoid(x):
    return 1.0 / (1.0 + jnp.exp(-x))


def _silu(x):
    return x * _sigmoid(x)


def _softplus(x):
    return jnp.maximum(x, 0.0) + jnp.log(1.0 + jnp.exp(-jnp.abs(x)))


def _layer_norm(x, w, b):
    mu = jnp.mean(x, axis=-1, keepdims=True)
    d = x - mu
    var = jnp.mean(d * d, axis=-1, keepdims=True)
    return d * lax.rsqrt(var + LN_EPS) * w + b


def _shift_rows(x, prev_row):
    row = _iota((x.shape[0], 1), 0)
    return jnp.where(row == 0, prev_row, pltpu.roll(x, 1, axis=0))


def _proj_kernel(x_ref, w_ref, rw_ref, at_ref, sd_ref, hg_ref):
    xb = x_ref[...].astype(BF16)
    off = 0
    for ref, width in ((rw_ref, RW_W), (at_ref, AT_W), (sd_ref, SD_W), (hg_ref, HG_W)):
        for c0 in range(0, width, 256):
            cw = min(256, width - c0)
            ref[:, c0:c0 + cw] = jnp.dot(xb, w_ref[:, off + c0:off + c0 + cw],
                                         preferred_element_type=F32)
        off += width


def _project(x2d, w_cat):
    m = x2d.shape[0]
    outs = [jax.ShapeDtypeStruct((m, w), F32) for w in (RW_W, AT_W, SD_W, HG_W)]
    return pl.pallas_call(
        _proj_kernel,
        grid=(m // PROJ_TM,),
        in_specs=[pl.BlockSpec((PROJ_TM, D_MODEL), lambda i: (i, 0)),
                  pl.BlockSpec((D_MODEL, PROJ_W), lambda i: (0, 0), pipeline_mode=pl.Buffered(1))],
        out_specs=[pl.BlockSpec((PROJ_TM, w), lambda i: (i, 0)) for w in (RW_W, AT_W, SD_W, HG_W)],
        out_shape=outs,
        compiler_params=pltpu.CompilerParams(dimension_semantics=("parallel",),
                                             vmem_limit_bytes=VMEM_LIMIT),
    )(x2d, w_cat)


def _stack_heads(x, head_sel):
    xb = x.astype(BF16)
    zero = jnp.zeros_like(xb)
    return jnp.concatenate([jnp.where(m, xb, zero) for m in head_sel], axis=0)


def _rwkv_chunks(at, bh, kh, rt, bdec, kdec, v, head_sel, same):
    n = len(at)
    rng = range(n)
    T = at[0].shape[0]
    G = D_GROUP
    row = _iota((T, G), 0)
    lane_t = _iota((T, G), 1) % T
    strict = lane_t < row
    incl = lane_t <= row
    st = lambda x: _stack_heads(x, head_sel)
    lhs = [jnp.concatenate([at[j], rt[j]], axis=0).astype(BF16) for j in rng]
    ab = [_dot(lhs[j], st(bh[j]), _NT) for j in rng]
    ak = [_dot(lhs[j], st(kh[j]), _NT) for j in rng]
    n_u = [jnp.where(strict, ab[j][0:T], 0.0) for j in rng]
    a_ak = [jnp.where(strict, ak[j][0:T], 0.0) for j in rng]
    m_rb = [jnp.where(incl, ab[j][T:2 * T], 0.0).astype(BF16) for j in rng]
    m_rk = [jnp.where(incl, ak[j][T:2 * T], 0.0).astype(BF16) for j in rng]

    eye = jnp.where(lane_t == row, 1.0, 0.0)
    x_u = [eye + n_u[j] for j in rng]
    pw = [_dot(n_u[j].astype(BF16), st(n_u[j]), _NN) for j in rng]
    steps = T.bit_length() - 2
    for it in range(steps):
        w = [st(pw[j]) for j in rng]
        if it + 1 < steps:
            res = [_dot(jnp.concatenate([pw[j], x_u[j]], axis=0).astype(BF16), w[j], _NN) for j in rng]
            pw = [res[j][0:T] for j in rng]
            x_u = [x_u[j] + res[j][T:2 * T] for j in rng]
        else:
            x_u = [x_u[j] + _dot(x_u[j].astype(BF16), w[j], _NN) for j in rng]

    v_s = [st(v[j]) for j in rng]
    akv = [_dot(a_ak[j].astype(BF16), v_s[j], _NN) for j in rng]
    xb = [x_u[j].astype(BF16) for j in rng]
    p_u = [_dot(xb[j], st(at[j]), _NN) for j in rng]
    q_u = [_dot(xb[j], st(akv[j]), _NN) for j in rng]
    r_eff = [rt[j] + _dot(m_rb[j], st(p_u[j]), _NN) for j in rng]
    o_par = [_dot(m_rb[j], st(q_u[j]), _NN) + _dot(m_rk[j], v_s[j], _NN) for j in rng]
    gbt = [jnp.where(same, _mm(p_u[j], bdec[j], _TN), 0.0) for j in rng]
    ut = [jnp.where(same, _mm(q_u[j], bdec[j], _TN) + _mm(v[j], kdec[j], _TN), 0.0) for j in rng]
    return r_eff, o_par, gbt, ut


def _rwkv_kernel(has_vres, *refs):
    if has_vres:
        (f_ref, fprev_ref, vf_ref, mu_ref, vec_ref, w2_ref, a2_ref, g2_ref, v2_ref,
         y_ref, s_ref) = refs
    else:
        (f_ref, fprev_ref, mu_ref, vec_ref, w2_ref, a2_ref, g2_ref,
         y_ref, vout_ref, s_ref) = refs
    c = pl.program_id(1)
    T = RW_T
    R = RW_ROWS
    G = D_GROUP

    @pl.when(c == 0)
    def _():
        s_ref[...] = jnp.zeros_like(s_ref)

    f = f_ref[...]
    prev_row = jnp.where(c == 0, 0.0, fprev_ref[7:8, :])
    x = f + (_shift_rows(f, prev_row) - f) * mu_ref[...]
    r = x[:, 0:G]
    k = x[:, G:2 * G]
    v = x[:, 2 * G:3 * G]
    seg = x[:, 3 * G:3 * G + 128]
    w0, a0, k_k, k_a = (vec_ref[i:i + 1, :] for i in range(4))
    lnx_w, lnx_b, r_k = (vec_ref[i:i + 1, :] for i in range(4, 7))

    p = w0 + _mm3(jnp.tanh(seg), w2_ref[...])
    lw = -math.exp(-0.5) * _sigmoid(p)
    a = _sigmoid(a0 + _mm(seg, a2_ref[...]))
    g = _mm(_sigmoid(seg), g2_ref[...])
    if has_vres:
        seg2 = x[:, 3 * G + 128:3 * G + 256]
        v0 = vec_ref[7:8, :]
        v = v + (vf_ref[...] - v) * _sigmoid(v0 + _mm(seg2, v2_ref[...]))
    else:
        vout_ref[...] = v

    head_sel = [(_iota((1, G), 1) // HEAD_DIM) == h for h in range(N_HEADS)]
    same = _same_head()
    bo = same.astype(BF16)
    kk = k * k_k
    kk = kk / jnp.maximum(jnp.sqrt(_mmx_r(kk * kk, bo, pieces=2)), 1e-12)
    k2 = k * (1.0 + (a - 1.0) * k_a)
    kka = kk * a

    ri = _iota((R, R), 0)
    ci = _iota((R, R), 1)
    in_chunk = (ri // T) == (ci // T)
    cs = _mmx_l((in_chunk & (ci <= ri)).astype(BF16), lw, pieces=2)
    c_last = jnp.concatenate(
        [jnp.broadcast_to(cs[(j + 1) * T - 1:(j + 1) * T, :], (T, G)) for j in range(RW_NCH)], axis=0)
    e_neg = jnp.exp(-cs)
    e_dec = jnp.exp(c_last - cs)
    at = -kk * jnp.exp(cs - lw)
    bh = kka * e_neg
    kh = k2 * e_neg
    rt = r * jnp.exp(cs)
    bdec = kka * e_dec
    kdec = k2 * e_dec

    chunks = lambda z: [z[j * T:(j + 1) * T] for j in range(RW_NCH)]
    r_eff, o_par, gbt, ut = _rwkv_chunks(chunks(at), chunks(bh), chunks(kh), chunks(rt), chunks(bdec),
                                         chunks(kdec), chunks(v), head_sel, same)
    s = s_ref[...]
    outs = []
    for j in range(RW_NCH):
        outs.append(_mm(r_eff[j], s, _NT) + o_par[j])
        w_chunk = jnp.exp(c_last[j * T:j * T + 1, :])
        s = s * w_chunk + _mm(s, gbt[j]) + ut[j]
    s_ref[...] = s
    o = jnp.concatenate(outs, axis=0)

    mean = _mmx_r(o, bo, pieces=2) * (1.0 / HEAD_DIM)
    d = o - mean
    var = _mmx_r(d * d, bo, pieces=2) * (1.0 / HEAD_DIM)
    yn = d * lax.rsqrt(var + RWKV_GN_EPS) * lnx_w + lnx_b
    bonus = _mmx_r(r * k2 * r_k, bo, pieces=2) * v
    y_ref[...] = (yn + bonus) * g


def _rwkv(rw, v_first, mu, vec, w2p, a2p, g2p, v2p, batch, seq):
    has_vres = v_first is not None
    nc = seq // RW_ROWS
    m = batch * seq
    row_spec = lambda w: pl.BlockSpec((RW_ROWS, w), lambda b, c: (b * nc + c, 0))
    const = lambda shape: pl.BlockSpec(shape, lambda b, c: (0, 0))
    prev_spec = pl.BlockSpec(
        (8, RW_W), lambda b, c: (jnp.maximum(b * (seq // 8) + c * (RW_ROWS // 8) - 1, 0), 0))
    in_specs = [row_spec(RW_W), prev_spec]
    args = [rw, rw]
    if has_vres:
        in_specs.append(row_spec(D_GROUP))
        args.append(v_first)
    in_specs += [const((1, RW_W)), const((8, D_GROUP)), const((128, D_GROUP)),
                 const((128, D_GROUP)), const((128, D_GROUP))]
    args += [mu, vec, w2p, a2p, g2p]
    if has_vres:
        in_specs.append(const((128, D_GROUP)))
        args.append(v2p)
    out_shape = [jax.ShapeDtypeStruct((m, D_GROUP), F32)]
    out_specs = [row_spec(D_GROUP)]
    if not has_vres:
        out_shape.append(jax.ShapeDtypeStruct((m, D_GROUP), F32))
        out_specs.append(row_spec(D_GROUP))
    outs = pl.pallas_call(
        functools.partial(_rwkv_kernel, has_vres),
        grid=(batch, nc),
        in_specs=in_specs,
        out_specs=out_specs,
        out_shape=out_shape,
        scratch_shapes=[pltpu.VMEM((D_GROUP, D_GROUP), F32)],
        compiler_params=pltpu.CompilerParams(dimension_semantics=("parallel", "arbitrary"),
                                             vmem_limit_bytes=VMEM_LIMIT),
    )(*args)
    if has_vres:
        return outs[0], v_first
    return outs[0], outs[1]


def _attn_kernel(qkv_ref, o_ref, perm_ref, acc_ref, l_ref, m_ref):
    G = D_GROUP
    blk = ATTN_BLK
    seq = qkv_ref.shape[0]
    nres = DILATIONS[-1]
    sub = seq // nres
    masks = _head_masks()
    scale = HEAD_DIM ** -0.5

    def jmap(a, d):
        run = blk * d // nres
        return (nres // d) * (a % run) + a // run

    to_perm = (_iota((blk, blk), 1) == jmap(_iota((blk, blk), 0), 1)).astype(BF16)
    from_perm = (_iota((blk, blk), 0) == jmap(_iota((blk, blk), 1), 1)).astype(BF16)
    run1 = blk // nres

    def permute(n, carry):
        r0 = pl.multiple_of(n * blk, blk)
        pb = jnp.dot(to_perm, qkv_ref[pl.ds(r0, blk), :].astype(BF16), preferred_element_type=F32)
        for e in range(nres):
            dst = pl.multiple_of(e * sub + n * run1, run1)
            perm_ref[pl.ds(dst, run1), :] = pb[e * run1:(e + 1) * run1, :]
        return carry

    lax.fori_loop(0, seq // blk, permute, 0)

    for bi, d in enumerate(reversed(DILATIONS)):
        runs = nres // d
        run = blk // runs
        nb = seq // (d * blk)
        has_prev = nb > 1
        nk = 2 * blk if has_prev else blk
        jq = jmap(_iota((blk, nk), 0), d)
        kb = _iota((blk, nk), 1)
        jk = jmap(kb % blk, d) + (kb // blk) * blk
        dist = (blk if has_prev else 0) + jq - jk
        in_band = (dist >= 0) & (dist <= blk)
        cur_only = kb >= blk
        distf = dist.astype(F32)
        first = bi == 0
        last = bi == len(DILATIONS) - 1

        def body(it, carry, d=d, runs=runs, run=run, nb=nb, has_prev=has_prev, in_band=in_band,
                 cur_only=cur_only, distf=distf, first=first, last=last):
            rho = it // nb
            n = it % nb

            def starts(nn):
                return [pl.multiple_of((rho + d * e) * sub + nn * run, run) for e in range(runs)]

            def gather(st, c0):
                return jnp.concatenate([perm_ref[pl.ds(s0, run), c0:c0 + G] for s0 in st], axis=0)

            cur = starts(n)
            q = gather(cur, 0)
            if has_prev:
                prv = starts(jnp.maximum(n - 1, 0))
                kcat = jnp.concatenate([gather(prv, G), gather(cur, G)], axis=0)
                vcat = jnp.concatenate([gather(prv, 2 * G), gather(cur, 2 * G)], axis=0)
                valid = in_band & ((n > 0) | cur_only)
            else:
                kcat = gather(cur, G)
                vcat = gather(cur, 2 * G)
                valid = in_band
            kcat = kcat.astype(BF16)
            acc_b = jnp.zeros((blk, G), F32)
            l_b = jnp.zeros((blk, G), F32)
            m_b = jnp.zeros((blk, G), F32)
            for h in range(N_HEADS):
                s = _dot((q * masks[h]).astype(BF16), kcat, _NT) * scale
                s = s - distf * (ALIBI_SLOPES[h] * d)
                s = jnp.where(valid, s, -jnp.inf)
                mh = jnp.max(s, axis=-1, keepdims=True)
                ph = jnp.exp(s - mh)
                lh = jnp.sum(ph, axis=-1, keepdims=True)
                acc_b = acc_b + _mm(ph, vcat * masks[h])
                l_b = l_b + lh * masks[h]
                m_b = m_b + mh * masks[h]
            outs = []
            for e, s0 in enumerate(cur):
                idx = pl.ds(s0, run)
                sl = slice(e * run, (e + 1) * run)
                if first:
                    acc_ref[idx, :] = acc_b[sl]
                    l_ref[idx, :] = l_b[sl]
                    m_ref[idx, :] = m_b[sl]
                else:
                    m_old = m_ref[idx, :]
                    m_new = jnp.maximum(m_old, m_b[sl])
                    w_old = jnp.exp(m_old - m_new)
                    w_new = jnp.exp(m_b[sl] - m_new)
                    acc_n = acc_ref[idx, :] * w_old + acc_b[sl] * w_new
                    l_n = l_ref[idx, :] * w_old + l_b[sl] * w_new
                    if last:
                        outs.append(acc_n / l_n)
                    else:
                        acc_ref[idx, :] = acc_n
                        l_ref[idx, :] = l_n
                        m_ref[idx, :] = m_new
            if last:
                r0 = pl.multiple_of(n * blk, blk)
                o_ref[pl.ds(r0, blk), :] = _mmx_l(from_perm, jnp.concatenate(outs, axis=0))
            return carry

        lax.fori_loop(0, d * nb, body, 0)


def _attention(at, batch, seq):
    m = batch * seq
    return pl.pallas_call(
        _attn_kernel,
        grid=(batch,),
        in_specs=[pl.BlockSpec((seq, AT_W), lambda b: (b, 0))],
        out_specs=pl.BlockSpec((seq, D_GROUP), lambda b: (b, 0)),
        out_shape=jax.ShapeDtypeStruct((m, D_GROUP), F32),
        scratch_shapes=[pltpu.VMEM((seq, AT_W), F32)] + [pltpu.VMEM((seq, D_GROUP), F32)] * 3,
        compiler_params=pltpu.CompilerParams(dimension_semantics=("parallel",),
                                             vmem_limit_bytes=VMEM_LIMIT),
    )(at)


def _ssd_kernel(f_ref, fprev_ref, cw_ref, cb_ref, dtb_ref, vec_ref, ex_ref, y_ref, st_ref, pad_ref):
    c = pl.program_id(1)
    T = SSD_T
    G = D_GROUP
    N = SSD_STATE

    @pl.when(c == 0)
    def _():
        st_ref[...] = jnp.zeros_like(st_ref)

    xbc_raw = f_ref[:, G:G + SSD_XBC]
    pad_ref[0:8, :] = jnp.where(c == 0, 0.0, fprev_ref[:, G:G + SSD_XBC])
    pad_ref[8:8 + T, :] = xbc_raw
    conv = xbc_raw * cw_ref[3:4, :] + cb_ref[...]
    for j in range(1, 4):
        conv = conv + pad_ref[8 - j:8 - j + T, :] * cw_ref[3 - j:4 - j, :]
    xbc = _silu(conv)
    xs = xbc[:, 0:G]
    bm = xbc[:, G:G + 2 * N]
    cm = xbc[:, G + 2 * N:G + 4 * N]
    z = f_ref[:, 0:G]

    a_dense, d_dense, norm_w = (vec_ref[i:i + 1, :] for i in range(3))
    dt = _softplus(f_ref[:, G + SSD_XBC:G + SSD_XBC + 128] + dtb_ref[...])
    dt_dense = _mmx_r(dt, ex_ref[...])
    da = dt_dense * a_dense
    ltri = (_iota((T, T), 1) <= _iota((T, T), 0))
    ltri_b = ltri.astype(BF16)
    cs = _mmx_l(ltri_b, da)
    cs_last = cs[T - 1:T, :]
    xdt = xs * dt_dense
    umat = (_iota((T, T), 0) > _iota((T, T), 1)).astype(F32)

    masks = _head_masks()
    y_diag = jnp.zeros((T, G), F32)
    scores = [_mm(cm[:, g * N:(g + 1) * N], bm[:, g * N:(g + 1) * N], _NT) for g in range(2)]
    for h in range(N_HEADS):
        da_col = jnp.broadcast_to(da[:, h * HEAD_DIM:h * HEAD_DIM + 1], (T, T))
        seg = _mmx_l(ltri_b, da_col * umat)
        dec = jnp.where(ltri, jnp.exp(jnp.where(ltri, seg, 0.0)), 0.0)
        y_diag = y_diag + _mm(scores[h // 2] * dec, xdt * masks[h])

    st = st_ref[...]
    y_off = jnp.concatenate([_mm(cm[:, 0:N], st[:, 0:N]), _mm(cm[:, N:2 * N], st[:, N:2 * N])],
                            axis=1) * jnp.exp(cs)
    xd = xdt * jnp.exp(cs_last - cs)
    st_ref[...] = st * jnp.exp(cs_last) + jnp.concatenate(
        [_mm(bm[:, 0:N], xd[:, 0:N], _TN), _mm(bm[:, N:2 * N], xd[:, N:2 * N], _TN)], axis=1)

    y = (y_diag + y_off + xs * d_dense) * _silu(z)
    halves = []
    for g in range(2):
        yg = y[:, g * N:(g + 1) * N]
        halves.append(yg * lax.rsqrt(jnp.mean(yg * yg, axis=-1, keepdims=True) + RMS_EPS))
    y_ref[...] = jnp.concatenate(halves, axis=1) * norm_w


def _ssd(sd, cw, cb, dtb, vec, ex, batch, seq):
    nc = seq // SSD_T
    m = batch * seq
    const = lambda shape: pl.BlockSpec(shape, lambda b, c: (0, 0))
    return pl.pallas_call(
        _ssd_kernel,
        grid=(batch, nc),
        in_specs=[pl.BlockSpec((SSD_T, SD_W), lambda b, c: (b * nc + c, 0)),
                  pl.BlockSpec((8, SD_W),
                               lambda b, c: (jnp.maximum(b * (seq // 8) + c * (SSD_T // 8) - 1, 0), 0)),
                  const((4, SSD_XBC)), const((1, SSD_XBC)), const((1, 128)), const((8, D_GROUP)),
                  const((128, D_GROUP))],
        out_specs=pl.BlockSpec((SSD_T, D_GROUP), lambda b, c: (b * nc + c, 0)),
        out_shape=jax.ShapeDtypeStruct((m, D_GROUP), F32),
        scratch_shapes=[pltpu.VMEM((SSD_STATE, D_GROUP), F32),
                        pltpu.VMEM((8 + SSD_T, SSD_XBC), F32)],
        compiler_params=pltpu.CompilerParams(dimension_semantics=("parallel", "arbitrary"),
                                             vmem_limit_bytes=VMEM_LIMIT),
    )(sd, sd, cw, cb, dtb, vec, ex)


def _hgrn_kernel(f_ref, vec_ref, y_ref, st_ref, q_s, k_s, b_s, o_s):
    c = pl.program_id(1)
    T = HG_T
    C = HG_C
    G = D_GROUP

    @pl.when(c == 0)
    def _():
        st_ref[...] = jnp.zeros_like(st_ref)

    lb = vec_ref[0:1, :]
    norm_w = vec_ref[1:2, :]
    forget = lb + (1.0 - lb) * _sigmoid(f_ref[:, G:2 * G])
    q_s[...] = _silu(f_ref[:, 0:G])
    k_s[...] = 1.0 - forget
    blk_tri = ((_iota((T, T), 0) // C == _iota((T, T), 1) // C)
               & (_iota((T, T), 1) <= _iota((T, T), 0))).astype(BF16)
    b_s[...] = _mmx_l(blk_tri, jnp.log(forget))

    same = _same_head()
    bo = same.astype(BF16)
    t_idx = _iota((C, 1), 0)

    def body(i, carry):
        r0 = pl.multiple_of(i * C, C)
        bq = b_s[pl.ds(r0, C), :]
        qq = q_s[pl.ds(r0, C), :]
        kq = k_s[pl.ds(r0, C), :]
        vv = f_ref[pl.ds(r0, C), 2 * G:3 * G]
        parts = []
        for s in range(C):
            e = jnp.exp(jnp.minimum(bq - bq[s:s + 1, :], 0.0))
            parts.append(jnp.where(t_idx >= s, qq * e * kq[s:s + 1, :], 0.0))
        z = _mm(jnp.concatenate(parts, axis=0), bo)
        o = _mm(qq * jnp.exp(bq), st_ref[...], _NT)
        for s in range(C):
            o = o + z[s * C:(s + 1) * C, :] * vv[s:s + 1, :]
        o_s[pl.ds(r0, C), :] = o
        b_last = bq[C - 1:C, :]
        upd = _mm(vv, kq * jnp.exp(b_last - bq), _TN)
        st_ref[...] = st_ref[...] * jnp.exp(b_last) + jnp.where(same, upd, 0.0)
        return carry

    lax.fori_loop(0, T // C, body, 0)
    o = o_s[...]
    ms = _mmx_r(o * o, bo) * (1.0 / HEAD_DIM)
    y_ref[...] = o * lax.rsqrt(ms + RMS_EPS) * norm_w * _silu(f_ref[:, 3 * G:4 * G])


def _hgrn(hg, vec, batch, seq):
    nc = seq // HG_T
    m = batch * seq
    return pl.pallas_call(
        _hgrn_kernel,
        grid=(batch, nc),
        in_specs=[pl.BlockSpec((HG_T, HG_W), lambda b, c: (b * nc + c, 0)),
                  pl.BlockSpec((8, D_GROUP), lambda b, c: (0, 0))],
        out_specs=pl.BlockSpec((HG_T, D_GROUP), lambda b, c: (b * nc + c, 0)),
        out_shape=jax.ShapeDtypeStruct((m, D_GROUP), F32),
        scratch_shapes=[pltpu.VMEM((D_GROUP, D_GROUP), F32)] + [pltpu.VMEM((HG_T, D_GROUP), F32)] * 4,
        compiler_params=pltpu.CompilerParams(dimension_semantics=("parallel", "arbitrary"),
                                             vmem_limit_bytes=VMEM_LIMIT),
    )(hg, vec)


def _post_kernel(x_ref, ya_ref, yb_ref, yc_ref, yd_ref, wo_ref, wu_ref, wd_ref, ln_ref, o_ref):
    x = x_ref[...]
    mix = jnp.zeros(x.shape, F32)
    for i, ref in enumerate((ya_ref, yb_ref, yc_ref, yd_ref)):
        mix = mix + jnp.dot(ref[...].astype(BF16), wo_ref[i * D_GROUP:(i + 1) * D_GROUP, :],
                            preferred_element_type=F32)
    x1 = _layer_norm(ALPHA * x + mix, ln_ref[0:1, :], ln_ref[1:2, :])
    xb = x1.astype(BF16)
    acc = ALPHA * x1
    for c0 in range(0, D_FF, FF_CHUNK):
        h = jnp.maximum(jnp.dot(xb, wu_ref[:, c0:c0 + FF_CHUNK], preferred_element_type=F32), 0.0)
        acc = acc + jnp.dot((h * h).astype(BF16), wd_ref[c0:c0 + FF_CHUNK, :],
                            preferred_element_type=F32)
    o_ref[...] = _layer_norm(acc, ln_ref[2:3, :], ln_ref[3:4, :])


def _post(x2d, ys, wo, wu, wd, ln):
    m = x2d.shape[0]
    row = lambda w: pl.BlockSpec((POST_TM, w), lambda i: (i, 0))
    res = lambda shape: pl.BlockSpec(shape, lambda i: (0, 0), pipeline_mode=pl.Buffered(1))
    return pl.pallas_call(
        _post_kernel,
        grid=(m // POST_TM,),
        in_specs=[row(D_MODEL)] + [row(D_GROUP)] * 4
        + [res((D_MODEL, D_MODEL)), res((D_MODEL, D_FF)), res((D_FF, D_MODEL)), res((8, D_MODEL))],
        out_specs=row(D_MODEL),
        out_shape=jax.ShapeDtypeStruct((m, D_MODEL), F32),
        compiler_params=pltpu.CompilerParams(dimension_semantics=("parallel",),
                                             vmem_limit_bytes=VMEM_LIMIT),
    )(x2d, *ys, wo, wu, wd, ln)


def _pad_cols(a, width):
    return jnp.pad(a, ((0, 0), (0, width - a.shape[1])))


def _pad_rows(a, top, total):
    return jnp.pad(a, ((top, total - top - a.shape[0]), (0, 0)))


def _rep_heads(v):
    return jnp.repeat(v.astype(F32), HEAD_DIM)[None, :]


def _rows8(rows, width):
    out = jnp.concatenate([r.reshape(1, width).astype(F32) for r in rows], axis=0)
    return jnp.pad(out, ((0, 8 - out.shape[0]), (0, 0)))


def kernel(x, lower_bounds, w_in, w_in_vres, mu_shift, mu_vres, rwkv_w0, rwkv_w2, rwkv_a0, rwkv_a2, rwkv_g2, rwkv_k_k, rwkv_k_a, rwkv_r_k, rwkv_lnx_w, rwkv_lnx_b, rwkv_v0, rwkv_v2, ssd_conv_w, ssd_conv_b, ssd_dt_bias, ssd_A_log, ssd_D, ssd_norm_w, hgrn_norm_w, w_out, ln1_w, ln1_b, w_up, w_down, ln2_w, ln2_b):
    batch, seq, _ = x.shape
    lb = jax.nn.softmax(lower_bounds.astype(F32), axis=0)
    lb = jnp.cumsum(lb, axis=0) - lb[0]
    expand = (jnp.arange(128)[:, None] == (jnp.arange(D_GROUP)[None, :] // HEAD_DIM)).astype(BF16)

    h = x.reshape(batch * seq, D_MODEL)
    v_first = None
    for l in range(DEPTH):
        wl = lax.optimization_barrier(w_in[l].astype(BF16))
        o1 = RWKV_COLS
        o2 = o1 + ATTN_COLS
        o3 = o2 + SSD_COLS
        rw_cols = wl[:, :o1]
        mu = mu_shift[l][None, :]
        if l > 0:
            rw_cols = jnp.concatenate([rw_cols, w_in_vres[l - 1].astype(BF16)], axis=1)
            mu = jnp.concatenate([mu, mu_vres[l - 1][None, :]], axis=1)
        w_cat = jnp.concatenate([_pad_cols(rw_cols, RW_W), wl[:, o1:o2], _pad_cols(wl[:, o2:o3], SD_W),
                                 wl[:, o3:]], axis=1).astype(BF16)
        rw, at, sd, hg = _project(h, w_cat)

        vec_rows = [rwkv_w0[l], rwkv_a0[l], rwkv_k_k[l], rwkv_k_a[l], rwkv_lnx_w[l], rwkv_lnx_b[l],
                    rwkv_r_k[l].reshape(-1)]
        if l > 0:
            vec_rows.append(rwkv_v0[l - 1])
        y_a, v_first = _rwkv(
            rw, v_first, _pad_cols(mu, RW_W), _rows8(vec_rows, D_GROUP),
            _pad_rows(rwkv_w2[l], 0, 128), _pad_rows(rwkv_a2[l], 32, 128), _pad_rows(rwkv_g2[l], 64, 128),
            _pad_rows(rwkv_v2[l - 1], 0, 128) if l > 0 else None, batch, seq)
        y_b = _attention(at, batch, seq)
        ssd_vec = _rows8([_rep_heads(-jnp.exp(ssd_A_log[l].astype(F32))), _rep_heads(ssd_D[l]),
                          ssd_norm_w[l]], D_GROUP)
        y_c = _ssd(sd, ssd_conv_w[l].astype(F32), ssd_conv_b[l][None, :],
                   _pad_cols(ssd_dt_bias[l][None, :], 128), ssd_vec, expand, batch, seq)
        y_d = _hgrn(hg, _rows8([lb[l], hgrn_norm_w[l]], D_GROUP), batch, seq)
        ln = _rows8([ln1_w[l], ln1_b[l], ln2_w[l], ln2_b[l]], D_MODEL)
        h = _post(h, (y_a, y_b, y_c, y_d), w_out[l].astype(BF16), w_up[l].astype(BF16),
                  w_down[l].astype(BF16), ln)
    return h.reshape(batch, seq, D_MODEL)
```

```python
import functools
import math

import jax
import jax.numpy as jnp
from jax import lax
from jax.experimental import pallas as pl
from jax.experimental.pallas import tpu as pltpu

F32 = jnp.float32
BF16 = jnp.bfloat16

D_MODEL = 1024
DEPTH = 2
D_GROUP = 256
HEAD_DIM = 64
N_HEADS = 4
D_FF = 4 * D_MODEL
ALPHA = (2.0 * DEPTH) ** 0.25
LN_EPS = 1e-5
RMS_EPS = 1e-5
RWKV_GN_EPS = HEAD_DIM * 1e-5
RWKV_COLS = 896
ATTN_COLS = 768
SSD_STATE = 128
SSD_XBC = 768
SSD_COLS = 1028
HGRN_COLS = 1024
DILATIONS = (1, 4, 16)
ATTN_BLK = 128
ALIBI_SLOPES = tuple(2.0 ** (-8.0 * (h + 1) / N_HEADS) for h in range(N_HEADS))

RW_W, AT_W, SD_W, HG_W = 1024, 768, 1152, 1024
PROJ_W = RW_W + AT_W + SD_W + HG_W

RW_T = 64
RW_NCH = 4
RW_ROWS = RW_T * RW_NCH
SSD_T = 128
HG_C = 16
HG_T = 256
PROJ_TM = 512
POST_TM = 512
FF_CHUNK = 512

VMEM_LIMIT = 56 * 1024 * 1024


def _dot(a, b, dims):
    return lax.dot_general(a, b, (dims, ((), ())), preferred_element_type=F32)


_NN = ((1,), (0,))
_NT = ((1,), (1,))
_TN = ((0,), (0,))


def _mm(a, b, dims=_NN):
    return _dot(a.astype(BF16), b.astype(BF16), dims)


def _split(x, pieces):
    out = []
    for i in range(pieces):
        hi = x.astype(BF16)
        out.append(hi)
        if i + 1 < pieces:
            x = x - hi.astype(F32)
    return out


def _mmx_l(m01, x, dims=_NN, pieces=3):
    return sum(_dot(m01, xp, dims) for xp in _split(x, pieces))


def _mmx_r(x, m01, dims=_NN, pieces=3):
    return sum(_dot(xp, m01, dims) for xp in _split(x, pieces))


def _mm3(a, b, dims=_NN):
    ah = a.astype(BF16)
    al = (a - ah.astype(F32)).astype(BF16)
    bh = b.astype(BF16)
    bl = (b - bh.astype(F32)).astype(BF16)
    return _dot(ah, bh, dims) + _dot(ah, bl, dims) + _dot(al, bh, dims)


def _iota(shape, dim):
    return lax.broadcasted_iota(jnp.int32, shape, dim)


def _head_masks(width=D_GROUP):
    lane = _iota((1, width), 1) // HEAD_DIM
    return [(lane == h).astype(F32) for h in range(N_HEADS)]


def _same_head(n=D_GROUP):
    return (_iota((n, n), 0) // HEAD_DIM) == (_iota((n, n), 1) // HEAD_DIM)


def _sigmoid(x):
    return 1.0 / (1.0 + jnp.exp(-x))


def _silu(x):
    return x * _sigmoid(x)


def _softplus(x):
    return jnp.maximum(x, 0.0) + jnp.log(1.0 + jnp.exp(-jnp.abs(x)))


def _layer_norm(x, w, b):
    mu = jnp.mean(x, axis=-1, keepdims=True)
    d = x - mu
    var = jnp.mean(d * d, axis=-1, keepdims=True)
    return d * lax.rsqrt(var + LN_EPS) * w + b


def _shift_rows(x, prev_row):
    row = _iota((x.shape[0], 1), 0)
    return jnp.where(row == 0, prev_row, pltpu.roll(x, 1, axis=0))


def _proj_kernel(x_ref, w_ref, rw_ref, at_ref, sd_ref, hg_ref):
    xb = x_ref[...].astype(BF16)
    off = 0
    for ref, width in ((rw_ref, RW_W), (at_ref, AT_W), (sd_ref, SD_W), (hg_ref, HG_W)):
        for c0 in range(0, width, 256):
            cw = min(256, width - c0)
            ref[:, c0:c0 + cw] = jnp.dot(xb, w_ref[:, off + c0:off + c0 + cw],
                                         preferred_element_type=F32)
        off += width


def _project(x2d, w_cat):
    m = x2d.shape[0]
    outs = [jax.ShapeDtypeStruct((m, w), F32) for w in (RW_W, AT_W, SD_W, HG_W)]
    return pl.pallas_call(
        _proj_kernel,
        grid=(m // PROJ_TM,),
        in_specs=[pl.BlockSpec((PROJ_TM, D_MODEL), lambda i: (i, 0)),
                  pl.BlockSpec((D_MODEL, PROJ_W), lambda i: (0, 0), pipeline_mode=pl.Buffered(1))],
        out_specs=[pl.BlockSpec((PROJ_TM, w), lambda i: (i, 0)) for w in (RW_W, AT_W, SD_W, HG_W)],
        out_shape=outs,
        compiler_params=pltpu.CompilerParams(dimension_semantics=("parallel",),
                                             vmem_limit_bytes=VMEM_LIMIT),
    )(x2d, w_cat)


def _stack_heads(x, head_sel):
    xb = x.astype(BF16)
    zero = jnp.zeros_like(xb)
    return jnp.concatenate([jnp.where(m, xb, zero) for m in head_sel], axis=0)


def _rwkv_chunks(at, bh, kh, rt, bdec, kdec, v, head_sel, same):
    n = len(at)
    rng = range(n)
    T = at[0].shape[0]
    G = D_GROUP
    row = _iota((T, G), 0)
    lane_t = _iota((T, G), 1) % T
    strict = lane_t < row
    incl = lane_t <= row
    st = lambda x: _stack_heads(x, head_sel)
    lhs = [jnp.concatenate([at[j], rt[j]], axis=0).astype(BF16) for j in rng]
    ab = [_dot(lhs[j], st(bh[j]), _NT) for j in rng]
    ak = [_dot(lhs[j], st(kh[j]), _NT) for j in rng]
    n_u = [jnp.where(strict, ab[j][0:T], 0.0) for j in rng]
    a_ak = [jnp.where(strict, ak[j][0:T], 0.0) for j in rng]
    m_rb = [jnp.where(incl, ab[j][T:2 * T], 0.0).astype(BF16) for j in rng]
    m_rk = [jnp.where(incl, ak[j][T:2 * T], 0.0).astype(BF16) for j in rng]

    eye = jnp.where(lane_t == row, 1.0, 0.0)
    x_u = [eye + n_u[j] for j in rng]
    pw = [_dot(n_u[j].astype(BF16), st(n_u[j]), _NN) for j in rng]
    steps = T.bit_length() - 2
    for it in range(steps):
        w = [st(pw[j]) for j in rng]
        if it + 1 < steps:
            res = [_dot(jnp.concatenate([pw[j], x_u[j]], axis=0).astype(BF16), w[j], _NN) for j in rng]
            pw = [res[j][0:T] for j in rng]
            x_u = [x_u[j] + res[j][T:2 * T] for j in rng]
        else:
            x_u = [x_u[j] + _dot(x_u[j].astype(BF16), w[j], _NN) for j in rng]

    v_s = [st(v[j]) for j in rng]
    akv = [_dot(a_ak[j].astype(BF16), v_s[j], _NN) for j in rng]
    xb = [x_u[j].astype(BF16) for j in rng]
    p_u = [_dot(xb[j], st(at[j]), _NN) for j in rng]
    q_u = [_dot(xb[j], st(akv[j]), _NN) for j in rng]
    r_eff = [rt[j] + _dot(m_rb[j], st(p_u[j]), _NN) for j in rng]
    o_par = [_dot(m_rb[j], st(q_u[j]), _NN) + _dot(m_rk[j], v_s[j], _NN) for j in rng]
    gbt = [jnp.where(same, _mm(p_u[j], bdec[j], _TN), 0.0) for j in rng]
    ut = [jnp.where(same, _mm(q_u[j], bdec[j], _TN) + _mm(v[j], kdec[j], _TN), 0.0) for j in rng]
    return r_eff, o_par, gbt, ut


def _rwkv_kernel(has_vres, *refs):
    if has_vres:
        (f_ref, fprev_ref, vf_ref, mu_ref, vec_ref, w2_ref, a2_ref, g2_ref, v2_ref,
         y_ref, s_ref) = refs
    else:
        (f_ref, fprev_ref, mu_ref, vec_ref, w2_ref, a2_ref, g2_ref,
         y_ref, vout_ref, s_ref) = refs
    c = pl.program_id(1)
    T = RW_T
    R = RW_ROWS
    G = D_GROUP

    @pl.when(c == 0)
    def _():
        s_ref[...] = jnp.zeros_like(s_ref)

    f = f_ref[...]
    prev_row = jnp.where(c == 0, 0.0, fprev_ref[7:8, :])
    x = f + (_shift_rows(f, prev_row) - f) * mu_ref[...]
    r = x[:, 0:G]
    k = x[:, G:2 * G]
    v = x[:, 2 * G:3 * G]
    seg = x[:, 3 * G:3 * G + 128]
    w0, a0, k_k, k_a = (vec_ref[i:i + 1, :] for i in range(4))
    lnx_w, lnx_b, r_k = (vec_ref[i:i + 1, :] for i in range(4, 7))

    p = w0 + _mm3(jnp.tanh(seg), w2_ref[...])
    lw = -math.exp(-0.5) * _sigmoid(p)
    a = _sigmoid(a0 + _mm(seg, a2_ref[...]))
    g = _mm(_sigmoid(seg), g2_ref[...])
    if has_vres:
        seg2 = x[:, 3 * G + 128:3 * G + 256]
        v0 = vec_ref[7:8, :]
        v = v + (vf_ref[...] - v) * _sigmoid(v0 + _mm(seg2, v2_ref[...]))
    else:
        vout_ref[...] = v

    head_sel = [(_iota((1, G), 1) // HEAD_DIM) == h for h in range(N_HEADS)]
    same = _same_head()
    bo = same.astype(BF16)
    kk = k * k_k
    kk = kk / jnp.maximum(jnp.sqrt(_mmx_r(kk * kk, bo, pieces=2)), 1e-12)
    k2 = k * (1.0 + (a - 1.0) * k_a)
    kka = kk * a

    ri = _iota((R, R), 0)
    ci = _iota((R, R), 1)
    in_chunk = (ri // T) == (ci // T)
    cs = _mmx_l((in_chunk & (ci <= ri)).astype(BF16), lw, pieces=2)
    c_last = jnp.concatenate(
        [jnp.broadcast_to(cs[(j + 1) * T - 1:(j + 1) * T, :], (T, G)) for j in range(RW_NCH)], axis=0)
    e_neg = jnp.exp(-cs)
    e_dec = jnp.exp(c_last - cs)
    at = -kk * jnp.exp(cs - lw)
    bh = kka * e_neg
    kh = k2 * e_neg
    rt = r * jnp.exp(cs)
    bdec = kka * e_dec
    kdec = k2 * e_dec

    chunks = lambda z: [z[j * T:(j + 1) * T] for j in range(RW_NCH)]
    r_eff, o_par, gbt, ut = _rwkv_chunks(chunks(at), chunks(bh), chunks(kh), chunks(rt), chunks(bdec),
                                         chunks(kdec), chunks(v), head_sel, same)
    s = s_ref[...]
    outs = []
    for j in range(RW_NCH):
        outs.append(_mm(r_eff[j], s, _NT) + o_par[j])
        w_chunk = jnp.exp(c_last[j * T:j * T + 1, :])
        s = s * w_chunk + _mm(s, gbt[j]) + ut[j]
    s_ref[...] = s
    o = jnp.concatenate(outs, axis=0)

    mean = _mmx_r(o, bo, pieces=2) * (1.0 / HEAD_DIM)
    d = o - mean
    var = _mmx_r(d * d, bo, pieces=2) * (1.0 / HEAD_DIM)
    yn = d * lax.rsqrt(var + RWKV_GN_EPS) * lnx_w + lnx_b
    bonus = _mmx_r(r * k2 * r_k, bo, pieces=2) * v
    y_ref[...] = (yn + bonus) * g


def _rwkv(rw, v_first, mu, vec, w2p, a2p, g2p, v2p, batch, seq):
    has_vres = v_first is not None
    nc = seq // RW_ROWS
    m = batch * seq
    row_spec = lambda w: pl.BlockSpec((RW_ROWS, w), lambda b, c: (b * nc + c, 0))
    const = lambda shape: pl.BlockSpec(shape, lambda b, c: (0, 0))
    prev_spec = pl.BlockSpec(
        (8, RW_W), lambda b, c: (jnp.maximum(b * (seq // 8) + c * (RW_ROWS // 8) - 1, 0), 0))
    in_specs = [row_spec(RW_W), prev_spec]
    args = [rw, rw]
    if has_vres:
        in_specs.append(row_spec(D_GROUP))
        args.append(v_first)
    in_specs += [const((1, RW_W)), const((8, D_GROUP)), const((128, D_GROUP)),
                 const((128, D_GROUP)), const((128, D_GROUP))]
    args += [mu, vec, w2p, a2p, g2p]
    if has_vres:
        in_specs.append(const((128, D_GROUP)))
        args.append(v2p)
    out_shape = [jax.ShapeDtypeStruct((m, D_GROUP), F32)]
    out_specs = [row_spec(D_GROUP)]
    if not has_vres:
        out_shape.append(jax.ShapeDtypeStruct((m, D_GROUP), F32))
        out_specs.append(row_spec(D_GROUP))
    outs = pl.pallas_call(
        functools.partial(_rwkv_kernel, has_vres),
        grid=(batch, nc),
        in_specs=in_specs,
        out_specs=out_specs,
        out_shape=out_shape,
        scratch_shapes=[pltpu.VMEM((D_GROUP, D_GROUP), F32)],
        compiler_params=pltpu.CompilerParams(dimension_semantics=("parallel", "arbitrary"),
                                             vmem_limit_bytes=VMEM_LIMIT),
    )(*args)
    if has_vres:
        return outs[0], v_first
    return outs[0], outs[1]


def _attn_kernel(qkv_ref, o_ref, perm_ref, acc_ref, l_ref, m_ref, bias_ref):
    G = D_GROUP
    blk = ATTN_BLK
    seq = qkv_ref.shape[0]
    nres = DILATIONS[-1]
    sub = seq // nres
    head_sel = [(_iota((1, G), 1) // HEAD_DIM) == h for h in range(N_HEADS)]
    masks = _head_masks()
    scale = HEAD_DIM ** -0.5

    def jmap(a, d):
        run = blk * d // nres
        return (nres // d) * (a % run) + a // run

    to_perm = (_iota((blk, blk), 1) == jmap(_iota((blk, blk), 0), 1)).astype(BF16)
    from_perm = (_iota((blk, blk), 0) == jmap(_iota((blk, blk), 1), 1)).astype(BF16)
    run1 = blk // nres
    col_scale = jnp.where(_iota((1, AT_W), 1) < G, scale, 1.0)

    def permute(n, carry):
        r0 = pl.multiple_of(n * blk, blk)
        pb = jnp.dot(to_perm, (qkv_ref[pl.ds(r0, blk), :] * col_scale).astype(BF16),
                     preferred_element_type=F32)
        for e in range(nres):
            dst = pl.multiple_of(e * sub + n * run1, run1)
            perm_ref[pl.ds(dst, run1), :] = pb[e * run1:(e + 1) * run1, :]
        return carry

    lax.fori_loop(0, seq // blk, permute, 0)

    branches = tuple(reversed(DILATIONS))
    for bi, d in enumerate(branches):
        jq = jmap(_iota((blk, 2 * blk), 0), d)
        kb = _iota((blk, 2 * blk), 1)
        dist = blk + jq - (jmap(kb % blk, d) + (kb // blk) * blk)
        in_band = (dist >= 0) & (dist <= blk)
        for h in range(N_HEADS):
            bias_ref[bi * N_HEADS + h] = jnp.where(
                in_band, dist.astype(F32) * (-ALIBI_SLOPES[h] * d), -jnp.inf)

    for bi, d in enumerate(branches):
        runs = nres // d
        run = blk // runs
        nb = seq // (d * blk)
        first = bi == 0
        last = bi == len(branches) - 1

        def blocks(items, with_prev, bi=bi, d=d, runs=runs, run=run, first=first, last=last):
            its = range(len(items))
            heads = range(N_HEADS)

            def starts(rho, nn):
                return [pl.multiple_of((rho + d * e) * sub + nn * run, run) for e in range(runs)]

            def gather(st, c0):
                return jnp.concatenate([perm_ref[pl.ds(s0, run), c0:c0 + G] for s0 in st],
                                       axis=0).astype(BF16)

            cur = [starts(rho, n) for rho, n in items]
            qb = [gather(cur[i], 0) for i in its]
            if with_prev:
                prv = [starts(rho, n - 1) for rho, n in items]
                kcat = [jnp.concatenate([gather(prv[i], G), gather(cur[i], G)], axis=0) for i in its]
                vcat = [jnp.concatenate([gather(prv[i], 2 * G), gather(cur[i], 2 * G)], axis=0)
                        for i in its]
            else:
                kcat = [gather(cur[i], G) for i in its]
                vcat = [gather(cur[i], 2 * G) for i in its]
            zq = jnp.zeros_like(qb[0])
            kc0 = 0 if with_prev else blk
            pairs = [(0, 1), (2, 3)]
            s_p = [[_dot(jnp.concatenate([jnp.where(head_sel[h], qb[i], zq) for h in pr], axis=0),
                         kcat[i], _NT) for pr in pairs] for i in its]
            s = [[s_p[i][h // 2][(h % 2) * blk:(h % 2 + 1) * blk]
                  + bias_ref[bi * N_HEADS + h, :, kc0:2 * blk] for h in heads] for i in its]
            mh = [[jnp.max(s[i][h], axis=-1, keepdims=True) for h in heads] for i in its]
            ph = [[jnp.exp(s[i][h] - mh[i][h]) for h in heads] for i in its]
            lh = [[jnp.sum(ph[i][h], axis=-1, keepdims=True) for h in heads] for i in its]
            pv_p = [[_dot(jnp.concatenate([ph[i][h].astype(BF16) for h in pr], axis=0), vcat[i], _NN)
                     for pr in pairs] for i in its]
            acc_b = [sum(pv_p[i][h // 2][(h % 2) * blk:(h % 2 + 1) * blk] * masks[h] for h in heads)
                     for i in its]
            l_b = [sum(lh[i][h] * masks[h] for h in heads) for i in its]
            m_b = [sum(mh[i][h] * masks[h] for h in heads) for i in its]
            for i in its:
                outs = []
                for e, s0 in enumerate(cur[i]):
                    idx = pl.ds(s0, run)
                    sl = slice(e * run, (e + 1) * run)
                    if first:
                        acc_ref[idx, :] = acc_b[i][sl]
                        l_ref[idx, :] = l_b[i][sl]
                        m_ref[idx, :] = m_b[i][sl]
                    else:
                        m_old = m_ref[idx, :]
                        m_new = jnp.maximum(m_old, m_b[i][sl])
                        w_old = jnp.exp(m_old - m_new)
                        w_new = jnp.exp(m_b[i][sl] - m_new)
                        acc_n = acc_ref[idx, :] * w_old + acc_b[i][sl] * w_new
                        l_n = l_ref[idx, :] * w_old + l_b[i][sl] * w_new
                        if last:
                            outs.append(acc_n / l_n)
                        else:
                            acc_ref[idx, :] = acc_n
                            l_ref[idx, :] = l_n
                            m_ref[idx, :] = m_new
                if last:
                    r0 = pl.multiple_of(items[i][1] * blk, blk)
                    o_ref[pl.ds(r0, blk), :] = _mmx_l(from_perm, jnp.concatenate(outs, axis=0))

        def run_blocks(count, index, with_prev, blocks=blocks):
            def two(it, carry):
                blocks([index(2 * it), index(2 * it + 1)], with_prev)
                return carry

            lax.fori_loop(0, count // 2, two, 0)
            if count % 2:
                blocks([index(count - 1)], with_prev)

        run_blocks(d, lambda i: (i, 0), False)
        if nb > 1:
            run_blocks(d * (nb - 1), lambda i, nb=nb: (i // (nb - 1), i % (nb - 1) + 1), True)


def _attention(at, batch, seq):
    m = batch * seq
    return pl.pallas_call(
        _attn_kernel,
        grid=(batch,),
        in_specs=[pl.BlockSpec((seq, AT_W), lambda b: (b, 0))],
        out_specs=pl.BlockSpec((seq, D_GROUP), lambda b: (b, 0)),
        out_shape=jax.ShapeDtypeStruct((m, D_GROUP), F32),
        scratch_shapes=[pltpu.VMEM((seq, AT_W), F32)] + [pltpu.VMEM((seq, D_GROUP), F32)] * 3
        + [pltpu.VMEM((len(DILATIONS) * N_HEADS, ATTN_BLK, 2 * ATTN_BLK), F32)],
        compiler_params=pltpu.CompilerParams(dimension_semantics=("parallel",),
                                             vmem_limit_bytes=VMEM_LIMIT),
    )(at)


def _ssd_kernel(f_ref, fprev_ref, cw_ref, cb_ref, dtb_ref, vec_ref, ex_ref, y_ref, st_ref, pad_ref):
    c = pl.program_id(1)
    T = SSD_T
    G = D_GROUP
    N = SSD_STATE

    @pl.when(c == 0)
    def _():
        st_ref[...] = jnp.zeros_like(st_ref)

    xbc_raw = f_ref[:, G:G + SSD_XBC]
    pad_ref[0:8, :] = jnp.where(c == 0, 0.0, fprev_ref[:, G:G + SSD_XBC])
    pad_ref[8:8 + T, :] = xbc_raw
    conv = xbc_raw * cw_ref[3:4, :] + cb_ref[...]
    for j in range(1, 4):
        conv = conv + pad_ref[8 - j:8 - j + T, :] * cw_ref[3 - j:4 - j, :]
    xbc = _silu(conv)
    xs = xbc[:, 0:G]
    bm = xbc[:, G:G + 2 * N]
    cm = xbc[:, G + 2 * N:G + 4 * N]
    z = f_ref[:, 0:G]

    a_dense, d_dense, norm_w = (vec_ref[i:i + 1, :] for i in range(3))
    dt = _softplus(f_ref[:, G + SSD_XBC:G + SSD_XBC + 128] + dtb_ref[...])
    dt_dense = _mmx_r(dt, ex_ref[...])
    da = dt_dense * a_dense
    ltri = (_iota((T, T), 1) <= _iota((T, T), 0))
    ltri_b = ltri.astype(BF16)
    cs = _mmx_l(ltri_b, da)
    cs_last = cs[T - 1:T, :]
    xdt = xs * dt_dense
    umat = (_iota((T, T), 0) > _iota((T, T), 1)).astype(F32)

    masks = _head_masks()
    y_diag = jnp.zeros((T, G), F32)
    scores = [_mm(cm[:, g * N:(g + 1) * N], bm[:, g * N:(g + 1) * N], _NT) for g in range(2)]
    for h in range(N_HEADS):
        da_col = jnp.broadcast_to(da[:, h * HEAD_DIM:h * HEAD_DIM + 1], (T, T))
        seg = _mmx_l(ltri_b, da_col * umat)
        dec = jnp.where(ltri, jnp.exp(jnp.where(ltri, seg, 0.0)), 0.0)
        y_diag = y_diag + _mm(scores[h // 2] * dec, xdt * masks[h])

    st = st_ref[...]
    y_off = jnp.concatenate([_mm(cm[:, 0:N], st[:, 0:N]), _mm(cm[:, N:2 * N], st[:, N:2 * N])],
                            axis=1) * jnp.exp(cs)
    xd = xdt * jnp.exp(cs_last - cs)
    st_ref[...] = st * jnp.exp(cs_last) + jnp.concatenate(
        [_mm(bm[:, 0:N], xd[:, 0:N], _TN), _mm(bm[:, N:2 * N], xd[:, N:2 * N], _TN)], axis=1)

    y = (y_diag + y_off + xs * d_dense) * _silu(z)
    halves = []
    for g in range(2):
        yg = y[:, g * N:(g + 1) * N]
        halves.append(yg * lax.rsqrt(jnp.mean(yg * yg, axis=-1, keepdims=True) + RMS_EPS))
    y_ref[...] = jnp.concatenate(halves, axis=1) * norm_w


def _ssd(sd, cw, cb, dtb, vec, ex, batch, seq):
    nc = seq // SSD_T
    m = batch * seq
    const = lambda shape: pl.BlockSpec(shape, lambda b, c: (0, 0))
    return pl.pallas_call(
        _ssd_kernel,
        grid=(batch, nc),
        in_specs=[pl.BlockSpec((SSD_T, SD_W), lambda b, c: (b * nc + c, 0)),
                  pl.BlockSpec((8, SD_W),
                               lambda b, c: (jnp.maximum(b * (seq // 8) + c * (SSD_T // 8) - 1, 0), 0)),
                  const((4, SSD_XBC)), const((1, SSD_XBC)), const((1, 128)), const((8, D_GROUP)),
                  const((128, D_GROUP))],
        out_specs=pl.BlockSpec((SSD_T, D_GROUP), lambda b, c: (b * nc + c, 0)),
        out_shape=jax.ShapeDtypeStruct((m, D_GROUP), F32),
        scratch_shapes=[pltpu.VMEM((SSD_STATE, D_GROUP), F32),
                        pltpu.VMEM((8 + SSD_T, SSD_XBC), F32)],
        compiler_params=pltpu.CompilerParams(dimension_semantics=("parallel", "arbitrary"),
                                             vmem_limit_bytes=VMEM_LIMIT),
    )(sd, sd, cw, cb, dtb, vec, ex)


def _hgrn_kernel(f_ref, vec_ref, y_ref, st_ref):
    c = pl.program_id(1)
    T = HG_T
    C = HG_C
    G = D_GROUP
    nblk = T // C

    @pl.when(c == 0)
    def _():
        st_ref[...] = jnp.zeros_like(st_ref)

    lb = vec_ref[0:1, :]
    norm_w = vec_ref[1:2, :]
    forget = lb + (1.0 - lb) * _sigmoid(f_ref[:, G:2 * G])
    q = _silu(f_ref[:, 0:G])
    k = 1.0 - forget
    v = f_ref[:, 2 * G:3 * G]
    blk_tri = ((_iota((T, T), 0) // C == _iota((T, T), 1) // C)
               & (_iota((T, T), 1) <= _iota((T, T), 0))).astype(BF16)
    b = _mmx_l(blk_tri, jnp.log(forget), pieces=2)

    head_sel = [(_iota((1, G), 1) // HEAD_DIM) == h for h in range(N_HEADS)]
    bo = _same_head().astype(BF16)
    t_idx = _iota((C, 1), 0)
    blocks = lambda z: [z[i * C:(i + 1) * C] for i in range(nblk)]
    bq, qq, kq, vv = blocks(b), blocks(q), blocks(k), blocks(v)
    b_last = [bq[i][C - 1:C, :] for i in range(nblk)]

    upd = [_dot(_stack_heads(vv[i], head_sel), _stack_heads(kq[i] * jnp.exp(b_last[i] - bq[i]), head_sel),
                _TN) for i in range(nblk)]
    q_in = [qq[i] * jnp.exp(bq[i]) for i in range(nblk)]
    parts = []
    for i in range(nblk):
        for s in range(C):
            e = jnp.exp(bq[i] - bq[i][s:s + 1, :])
            parts.append(jnp.where(t_idx >= s, qq[i] * e * kq[i][s:s + 1, :], 0.0).astype(BF16))
    z = _dot(jnp.concatenate(parts, axis=0), bo, _NN)

    st = st_ref[...]
    outs = []
    for i in range(nblk):
        o = _mm(q_in[i], st, _NT)
        for s in range(C):
            r0 = (i * C + s) * C
            o = o + z[r0:r0 + C, :] * vv[i][s:s + 1, :]
        outs.append(o)
        st = st * jnp.exp(b_last[i]) + upd[i]
    st_ref[...] = st
    o = jnp.concatenate(outs, axis=0)
    ms = _mmx_r(o * o, bo, pieces=2) * (1.0 / HEAD_DIM)
    y_ref[...] = o * lax.rsqrt(ms + RMS_EPS) * norm_w * _silu(f_ref[:, 3 * G:4 * G])


def _hgrn(hg, vec, batch, seq):
    nc = seq // HG_T
    m = batch * seq
    return pl.pallas_call(
        _hgrn_kernel,
        grid=(batch, nc),
        in_specs=[pl.BlockSpec((HG_T, HG_W), lambda b, c: (b * nc + c, 0)),
                  pl.BlockSpec((8, D_GROUP), lambda b, c: (0, 0))],
        out_specs=pl.BlockSpec((HG_T, D_GROUP), lambda b, c: (b * nc + c, 0)),
        out_shape=jax.ShapeDtypeStruct((m, D_GROUP), F32),
        scratch_shapes=[pltpu.VMEM((D_GROUP, D_GROUP), F32)],
        compiler_params=pltpu.CompilerParams(dimension_semantics=("parallel", "arbitrary"),
                                             vmem_limit_bytes=VMEM_LIMIT),
    )(hg, vec)


def _post_kernel(x_ref, ya_ref, yb_ref, yc_ref, yd_ref, wo_ref, wu_ref, wd_ref, ln_ref, o_ref):
    x = x_ref[...]
    mix = jnp.zeros(x.shape, F32)
    for i, ref in enumerate((ya_ref, yb_ref, yc_ref, yd_ref)):
        mix = mix + jnp.dot(ref[...].astype(BF16), wo_ref[i * D_GROUP:(i + 1) * D_GROUP, :],
                            preferred_element_type=F32)
    x1 = _layer_norm(ALPHA * x + mix, ln_ref[0:1, :], ln_ref[1:2, :])
    xb = x1.astype(BF16)
    acc = ALPHA * x1
    for c0 in range(0, D_FF, FF_CHUNK):
        h = jnp.maximum(jnp.dot(xb, wu_ref[:, c0:c0 + FF_CHUNK], preferred_element_type=F32), 0.0)
        acc = acc + jnp.dot((h * h).astype(BF16), wd_ref[c0:c0 + FF_CHUNK, :],
                            preferred_element_type=F32)
    o_ref[...] = _layer_norm(acc, ln_ref[2:3, :], ln_ref[3:4, :])


def _post(x2d, ys, wo, wu, wd, ln):
    m = x2d.shape[0]
    row = lambda w: pl.BlockSpec((POST_TM, w), lambda i: (i, 0))
    res = lambda shape: pl.BlockSpec(shape, lambda i: (0, 0), pipeline_mode=pl.Buffered(1))
    return pl.pallas_call(
        _post_kernel,
        grid=(m // POST_TM,),
        in_specs=[row(D_MODEL)] + [row(D_GROUP)] * 4
        + [res((D_MODEL, D_MODEL)), res((D_MODEL, D_FF)), res((D_FF, D_MODEL)), res((8, D_MODEL))],
        out_specs=row(D_MODEL),
        out_shape=jax.ShapeDtypeStruct((m, D_MODEL), F32),
        compiler_params=pltpu.CompilerParams(dimension_semantics=("parallel",),
                                             vmem_limit_bytes=VMEM_LIMIT),
    )(x2d, *ys, wo, wu, wd, ln)


def _pad_cols(a, width):
    return jnp.pad(a, ((0, 0), (0, width - a.shape[1])))


def _pad_rows(a, top, total):
    return jnp.pad(a, ((top, total - top - a.shape[0]), (0, 0)))


def _rep_heads(v):
    return jnp.repeat(v.astype(F32), HEAD_DIM)[None, :]


def _rows8(rows, width):
    out = jnp.concatenate([r.reshape(1, width).astype(F32) for r in rows], axis=0)
    return jnp.pad(out, ((0, 8 - out.shape[0]), (0, 0)))


def kernel(x, lower_bounds, w_in, w_in_vres, mu_shift, mu_vres, rwkv_w0, rwkv_w2, rwkv_a0, rwkv_a2, rwkv_g2, rwkv_k_k, rwkv_k_a, rwkv_r_k, rwkv_lnx_w, rwkv_lnx_b, rwkv_v0, rwkv_v2, ssd_conv_w, ssd_conv_b, ssd_dt_bias, ssd_A_log, ssd_D, ssd_norm_w, hgrn_norm_w, w_out, ln1_w, ln1_b, w_up, w_down, ln2_w, ln2_b):
    batch, seq, _ = x.shape
    lb = jax.nn.softmax(lower_bounds.astype(F32), axis=0)
    lb = jnp.cumsum(lb, axis=0) - lb[0]
    expand = (jnp.arange(128)[:, None] == (jnp.arange(D_GROUP)[None, :] // HEAD_DIM)).astype(BF16)

    h = x.reshape(batch * seq, D_MODEL)
    v_first = None
    for l in range(DEPTH):
        wl = lax.optimization_barrier(w_in[l].astype(BF16))
        o1 = RWKV_COLS
        o2 = o1 + ATTN_COLS
        o3 = o2 + SSD_COLS
        rw_cols = wl[:, :o1]
        mu = mu_shift[l][None, :]
        if l > 0:
            rw_cols = jnp.concatenate([rw_cols, w_in_vres[l - 1].astype(BF16)], axis=1)
            mu = jnp.concatenate([mu, mu_vres[l - 1][None, :]], axis=1)
        w_cat = jnp.concatenate([_pad_cols(rw_cols, RW_W), wl[:, o1:o2], _pad_cols(wl[:, o2:o3], SD_W),
                                 wl[:, o3:]], axis=1).astype(BF16)
        rw, at, sd, hg = _project(h, w_cat)

        vec_rows = [rwkv_w0[l], rwkv_a0[l], rwkv_k_k[l], rwkv_k_a[l], rwkv_lnx_w[l], rwkv_lnx_b[l],
                    rwkv_r_k[l].reshape(-1)]
        if l > 0:
            vec_rows.append(rwkv_v0[l - 1])
        y_a, v_first = _rwkv(
            rw, v_first, _pad_cols(mu, RW_W), _rows8(vec_rows, D_GROUP),
            _pad_rows(rwkv_w2[l], 0, 128), _pad_rows(rwkv_a2[l], 32, 128), _pad_rows(rwkv_g2[l], 64, 128),
            _pad_rows(rwkv_v2[l - 1], 0, 128) if l > 0 else None, batch, seq)
        y_b = _attention(at, batch, seq)
        ssd_vec = _rows8([_rep_heads(-jnp.exp(ssd_A_log[l].astype(F32))), _rep_heads(ssd_D[l]),
                          ssd_norm_w[l]], D_GROUP)
        y_c = _ssd(sd, ssd_conv_w[l].astype(F32), ssd_conv_b[l][None, :],
                   _pad_cols(ssd_dt_bias[l][None, :], 128), ssd_vec, expand, batch, seq)
        y_d = _hgrn(hg, _rows8([lb[l], hgrn_norm_w[l]], D_GROUP), batch, seq)
        ln = _rows8([ln1_w[l], ln1_b[l], ln2_w[l], ln2_b[l]], D_MODEL)
        h = _post(h, (y_a, y_b, y_c, y_d), w_out[l].astype(BF16), w_up[l].astype(BF16),
                  w_down[l].astype(BF16), ln)
    return h.reshape(batch, seq, D_MODEL)
```

```python
import functools
import math

import jax
import jax.numpy as jnp
from jax import lax
from jax.experimental import pallas as pl
from jax.experimental.pallas import tpu as pltpu

F32 = jnp.float32
BF16 = jnp.bfloat16

D_MODEL = 1024
DEPTH = 2
D_GROUP = 256
HEAD_DIM = 64
N_HEADS = 4
D_FF = 4 * D_MODEL
ALPHA = (2.0 * DEPTH) ** 0.25
LN_EPS = 1e-5
RMS_EPS = 1e-5
RWKV_GN_EPS = HEAD_DIM * 1e-5
RWKV_COLS = 896
ATTN_COLS = 768
SSD_STATE = 128
SSD_XBC = 768
SSD_COLS = 1028
HGRN_COLS = 1024
DILATIONS = (1, 4, 16)
ATTN_BLK = 128
ALIBI_SLOPES = tuple(2.0 ** (-8.0 * (h + 1) / N_HEADS) for h in range(N_HEADS))

RW_W, AT_W, SD_W, HG_W = 1024, 768, 1152, 1024
PROJ_W = RW_W + AT_W + SD_W + HG_W

RW_T = 64
RW_NCH = 4
RW_NB = 2
RW_ROWS = RW_T * RW_NCH
SSD_T = 128
HG_C = 16
HG_T = 256
PROJ_TM = 512
POST_TM = 512
FF_CHUNK = 512

VMEM_LIMIT = 56 * 1024 * 1024


def _dot(a, b, dims):
    return lax.dot_general(a, b, (dims, ((), ())), preferred_element_type=F32)


_NN = ((1,), (0,))
_NT = ((1,), (1,))
_TN = ((0,), (0,))


def _mm(a, b, dims=_NN):
    return _dot(a.astype(BF16), b.astype(BF16), dims)


def _split(x, pieces):
    out = []
    for i in range(pieces):
        hi = x.astype(BF16)
        out.append(hi)
        if i + 1 < pieces:
            x = x - hi.astype(F32)
    return out


def _mmx_l(m01, x, dims=_NN, pieces=3):
    return sum(_dot(m01, xp, dims) for xp in _split(x, pieces))


def _mmx_r(x, m01, dims=_NN, pieces=3):
    return sum(_dot(xp, m01, dims) for xp in _split(x, pieces))


def _mm3(a, b, dims=_NN):
    ah = a.astype(BF16)
    al = (a - ah.astype(F32)).astype(BF16)
    bh = b.astype(BF16)
    bl = (b - bh.astype(F32)).astype(BF16)
    return _dot(ah, bh, dims) + _dot(ah, bl, dims) + _dot(al, bh, dims)


def _iota(shape, dim):
    return lax.broadcasted_iota(jnp.int32, shape, dim)


def _head_masks(width=D_GROUP):
    lane = _iota((1, width), 1) // HEAD_DIM
    return [(lane == h).astype(F32) for h in range(N_HEADS)]


def _same_head(n=D_GROUP):
    return (_iota((n, n), 0) // HEAD_DIM) == (_iota((n, n), 1) // HEAD_DIM)


def _sigmoid(x):
    return 0.5 * jnp.tanh(0.5 * x) + 0.5


def _sigmoid_tail(x):
    return 1.0 / (1.0 + jnp.exp(-x))


def _silu(x):
    return x * _sigmoid(x)


def _softplus(x):
    return jnp.maximum(x, 0.0) + jnp.log(1.0 + jnp.exp(-jnp.abs(x)))


def _layer_norm(x, w, b):
    mu = jnp.mean(x, axis=-1, keepdims=True)
    d = x - mu
    var = jnp.mean(d * d, axis=-1, keepdims=True)
    return d * lax.rsqrt(var + LN_EPS) * w + b


def _shift_rows(x, prev_row):
    row = _iota((x.shape[0], 1), 0)
    return jnp.where(row == 0, prev_row, pltpu.roll(x, 1, axis=0))


def _proj_kernel(x_ref, w_ref, rw_ref, at_ref, sd_ref, hg_ref):
    xb = x_ref[...].astype(BF16)
    off = 0
    for ref, width in ((rw_ref, RW_W), (at_ref, AT_W), (sd_ref, SD_W), (hg_ref, HG_W)):
        for c0 in range(0, width, 256):
            cw = min(256, width - c0)
            ref[:, c0:c0 + cw] = jnp.dot(xb, w_ref[:, off + c0:off + c0 + cw],
                                         preferred_element_type=F32)
        off += width


def _project(x2d, w_cat):
    m = x2d.shape[0]
    outs = [jax.ShapeDtypeStruct((m, w), F32) for w in (RW_W, AT_W, SD_W, HG_W)]
    return pl.pallas_call(
        _proj_kernel,
        grid=(m // PROJ_TM,),
        in_specs=[pl.BlockSpec((PROJ_TM, D_MODEL), lambda i: (i, 0)),
                  pl.BlockSpec((D_MODEL, PROJ_W), lambda i: (0, 0), pipeline_mode=pl.Buffered(1))],
        out_specs=[pl.BlockSpec((PROJ_TM, w), lambda i: (i, 0)) for w in (RW_W, AT_W, SD_W, HG_W)],
        out_shape=outs,
        compiler_params=pltpu.CompilerParams(dimension_semantics=("parallel",),
                                             vmem_limit_bytes=VMEM_LIMIT),
    )(x2d, w_cat)


def _stack_heads(x, head_sel):
    xb = x.astype(BF16)
    zero = jnp.zeros_like(xb)
    return jnp.concatenate([jnp.where(m, xb, zero) for m in head_sel], axis=0)


def _rwkv_chunks(at, bh, kh, rt, bdec, kdec, v, head_sel, same):
    n = len(at)
    rng = range(n)
    T = at[0].shape[0]
    G = D_GROUP
    row = _iota((T, G), 0)
    lane_t = _iota((T, G), 1) % T
    strict = lane_t < row
    incl = lane_t <= row
    st = lambda x: _stack_heads(x, head_sel)
    lhs = [jnp.concatenate([at[j], rt[j]], axis=0).astype(BF16) for j in rng]
    ab = [_dot(lhs[j], st(bh[j]), _NT) for j in rng]
    ak = [_dot(lhs[j], st(kh[j]), _NT) for j in rng]
    n_u = [jnp.where(strict, ab[j][0:T], 0.0) for j in rng]
    a_ak = [jnp.where(strict, ak[j][0:T], 0.0) for j in rng]
    m_rb = [jnp.where(incl, ab[j][T:2 * T], 0.0).astype(BF16) for j in rng]
    m_rk = [jnp.where(incl, ak[j][T:2 * T], 0.0).astype(BF16) for j in rng]

    eye = jnp.where(lane_t == row, 1.0, 0.0)
    x_u = [eye + n_u[j] for j in rng]
    pw = [_dot(n_u[j].astype(BF16), st(n_u[j]), _NN) for j in rng]
    steps = T.bit_length() - 2
    for it in range(steps):
        w = [st(pw[j]) for j in rng]
        if it + 1 < steps:
            res = [_dot(jnp.concatenate([pw[j], x_u[j]], axis=0).astype(BF16), w[j], _NN) for j in rng]
            pw = [res[j][0:T] for j in rng]
            x_u = [x_u[j] + res[j][T:2 * T] for j in rng]
        else:
            x_u = [x_u[j] + _dot(x_u[j].astype(BF16), w[j], _NN) for j in rng]

    v_s = [st(v[j]) for j in rng]
    akv = [_dot(a_ak[j].astype(BF16), v_s[j], _NN) for j in rng]
    xb = [x_u[j].astype(BF16) for j in rng]
    p_u = [_dot(xb[j], st(at[j]), _NN) for j in rng]
    q_u = [_dot(xb[j], st(akv[j]), _NN) for j in rng]
    r_eff = [rt[j] + _dot(m_rb[j], st(p_u[j]), _NN) for j in rng]
    o_par = [_dot(m_rb[j], st(q_u[j]), _NN) + _dot(m_rk[j], v_s[j], _NN) for j in rng]
    gbt = [jnp.where(same, _mm(p_u[j], bdec[j], _TN), 0.0) for j in rng]
    ut = [jnp.where(same, _mm(jnp.concatenate([q_u[j], v[j]], axis=0),
                               jnp.concatenate([bdec[j], kdec[j]], axis=0), _TN), 0.0) for j in rng]
    return r_eff, o_par, gbt, ut


def _rwkv_kernel(has_vres, *refs):
    if has_vres:
        (f_ref, fprev_ref, vf_ref, mu_ref, vec_ref, w2_ref, a2_ref, g2_ref, v2_ref,
         y_ref, s_ref) = refs
    else:
        (f_ref, fprev_ref, mu_ref, vec_ref, w2_ref, a2_ref, g2_ref,
         y_ref, vout_ref, s_ref) = refs
    c = pl.program_id(1)
    T = RW_T
    R = RW_ROWS * RW_NB
    G = D_GROUP
    nchunks = RW_NCH * RW_NB

    @pl.when(c == 0)
    def _():
        s_ref[...] = jnp.zeros_like(s_ref)

    f = f_ref[...].reshape(R, RW_W)
    row = _iota((R, 1), 0)
    shifted = pltpu.roll(f, 1, axis=0)
    for b in range(RW_NB):
        prev_row = jnp.where(c == 0, 0.0, fprev_ref[b, 7:8, :])
        shifted = jnp.where(row == b * RW_ROWS, prev_row, shifted)
    x = f + (shifted - f) * mu_ref[...]
    r = x[:, 0:G]
    k = x[:, G:2 * G]
    v = x[:, 2 * G:3 * G]
    seg = x[:, 3 * G:3 * G + 128]
    w0, a0, k_k, k_a = (vec_ref[i:i + 1, :] for i in range(4))
    lnx_w, lnx_b, r_k = (vec_ref[i:i + 1, :] for i in range(4, 7))

    p = w0 + _mm3(jnp.tanh(seg), w2_ref[...])
    lw = -math.exp(-0.5) * _sigmoid(p)
    a = _sigmoid(a0 + _mm(seg, a2_ref[...]))
    g = _mm(_sigmoid(seg), g2_ref[...])
    if has_vres:
        seg2 = x[:, 3 * G + 128:3 * G + 256]
        v0 = vec_ref[7:8, :]
        v = v + (vf_ref[...].reshape(R, G) - v) * _sigmoid(v0 + _mm(seg2, v2_ref[...]))
    else:
        vout_ref[...] = v.reshape(RW_NB, RW_ROWS, G)

    head_sel = [(_iota((1, G), 1) // HEAD_DIM) == h for h in range(N_HEADS)]
    same = _same_head()
    bo = same.astype(BF16)
    kk = k * k_k
    kk = kk * lax.rsqrt(jnp.maximum(_mm(kk * kk, bo), 1e-24))
    k2 = k * (1.0 + (a - 1.0) * k_a)
    kka = kk * a

    ri = _iota((R, R), 0)
    ci = _iota((R, R), 1)
    in_chunk = (ri // T) == (ci // T)
    cs = _mmx_l((in_chunk & (ci <= ri)).astype(BF16), lw, pieces=2)
    c_last = jnp.concatenate(
        [jnp.broadcast_to(cs[(j + 1) * T - 1:(j + 1) * T, :], (T, G)) for j in range(nchunks)], axis=0)
    e_neg = jnp.exp(-cs)
    e_dec = jnp.exp(c_last - cs)
    at = -kk * jnp.exp(cs - lw)
    bh = kka * e_neg
    kh = k2 * e_neg
    rt = r * jnp.exp(cs)
    bdec = kka * e_dec
    kdec = k2 * e_dec

    chunks = lambda z: [z[j * T:(j + 1) * T] for j in range(nchunks)]
    r_eff, o_par, gbt, ut = _rwkv_chunks(chunks(at), chunks(bh), chunks(kh), chunks(rt), chunks(bdec),
                                         chunks(kdec), chunks(v), head_sel, same)
    s = [s_ref[b] for b in range(RW_NB)]
    outs = [None] * nchunks
    for j in range(RW_NCH):
        for b in range(RW_NB):
            i = b * RW_NCH + j
            outs[i] = _mm(r_eff[i], s[b], _NT) + o_par[i]
            w_chunk = jnp.exp(c_last[i * T:i * T + 1, :])
            s[b] = s[b] * w_chunk + _mm(s[b], gbt[i]) + ut[i]
    for b in range(RW_NB):
        s_ref[b] = s[b]
    o = jnp.concatenate(outs, axis=0)

    mean = _mmx_r(o, bo, pieces=2) * (1.0 / HEAD_DIM)
    d = o - mean
    var = _mm(d * d, bo) * (1.0 / HEAD_DIM)
    yn = d * lax.rsqrt(var + RWKV_GN_EPS) * lnx_w + lnx_b
    bonus = _mm(r * k2 * r_k, bo) * v
    y_ref[...] = ((yn + bonus) * g).reshape(RW_NB, RW_ROWS, G)


def _rwkv(rw, v_first, mu, vec, w2p, a2p, g2p, v2p, batch, seq):
    has_vres = v_first is not None
    nc = seq // RW_ROWS
    m = batch * seq
    row_spec = lambda w: pl.BlockSpec((RW_NB, RW_ROWS, w), lambda b, c: (b, c, 0))
    const = lambda shape: pl.BlockSpec(shape, lambda b, c: (0, 0))
    prev_spec = pl.BlockSpec(
        (RW_NB, 8, RW_W), lambda b, c: (b, jnp.maximum(c * (RW_ROWS // 8) - 1, 0), 0))
    rw3 = rw.reshape(batch, seq, RW_W)
    in_specs = [row_spec(RW_W), prev_spec]
    args = [rw3, rw3]
    if has_vres:
        in_specs.append(row_spec(D_GROUP))
        args.append(v_first.reshape(batch, seq, D_GROUP))
    in_specs += [const((1, RW_W)), const((8, D_GROUP)), const((128, D_GROUP)),
                 const((128, D_GROUP)), const((128, D_GROUP))]
    args += [mu, vec, w2p, a2p, g2p]
    if has_vres:
        in_specs.append(const((128, D_GROUP)))
        args.append(v2p)
    out_shape = [jax.ShapeDtypeStruct((batch, seq, D_GROUP), F32)]
    out_specs = [row_spec(D_GROUP)]
    if not has_vres:
        out_shape.append(jax.ShapeDtypeStruct((batch, seq, D_GROUP), F32))
        out_specs.append(row_spec(D_GROUP))
    outs = pl.pallas_call(
        functools.partial(_rwkv_kernel, has_vres),
        grid=(batch // RW_NB, nc),
        in_specs=in_specs,
        out_specs=out_specs,
        out_shape=out_shape,
        scratch_shapes=[pltpu.VMEM((RW_NB, D_GROUP, D_GROUP), F32)],
        compiler_params=pltpu.CompilerParams(dimension_semantics=("parallel", "arbitrary"),
                                             vmem_limit_bytes=VMEM_LIMIT),
    )(*args)
    y = outs[0].reshape(m, D_GROUP)
    if has_vres:
        return y, v_first
    return y, outs[1].reshape(m, D_GROUP)


def _attn_kernel(qkv_ref, o_ref, perm_ref, acc_ref, l_ref, m_ref, bias_ref):
    G = D_GROUP
    blk = ATTN_BLK
    seq = qkv_ref.shape[0]
    nres = DILATIONS[-1]
    sub = seq // nres
    head_sel = [(_iota((1, G), 1) // HEAD_DIM) == h for h in range(N_HEADS)]
    masks = _head_masks()
    scale = HEAD_DIM ** -0.5

    def jmap(a, d):
        run = blk * d // nres
        return (nres // d) * (a % run) + a // run

    to_perm = (_iota((blk, blk), 1) == jmap(_iota((blk, blk), 0), 1)).astype(BF16)
    from_perm = (_iota((blk, blk), 0) == jmap(_iota((blk, blk), 1), 1)).astype(BF16)
    run1 = blk // nres
    col_scale = jnp.where(_iota((1, AT_W), 1) < G, scale, 1.0)

    def permute(n, carry):
        r0 = pl.multiple_of(n * blk, blk)
        pb = jnp.dot(to_perm, (qkv_ref[pl.ds(r0, blk), :] * col_scale).astype(BF16),
                     preferred_element_type=F32)
        for e in range(nres):
            dst = pl.multiple_of(e * sub + n * run1, run1)
            perm_ref[pl.ds(dst, run1), :] = pb[e * run1:(e + 1) * run1, :]
        return carry

    lax.fori_loop(0, seq // blk, permute, 0)

    branches = tuple(reversed(DILATIONS))
    for bi, d in enumerate(branches):
        jq = jmap(_iota((blk, 2 * blk), 0), d)
        kb = _iota((blk, 2 * blk), 1)
        dist = blk + jq - (jmap(kb % blk, d) + (kb // blk) * blk)
        in_band = (dist >= 0) & (dist <= blk)
        for h in range(N_HEADS):
            bias_ref[bi * N_HEADS + h] = jnp.where(
                in_band, dist.astype(F32) * (-ALIBI_SLOPES[h] * d), -jnp.inf)

    for bi, d in enumerate(branches):
        runs = nres // d
        run = blk // runs
        nb = seq // (d * blk)
        first = bi == 0
        last = bi == len(branches) - 1

        def blocks(items, with_prev, bi=bi, d=d, runs=runs, run=run, first=first, last=last):
            its = range(len(items))
            heads = range(N_HEADS)

            def starts(rho, nn):
                return [pl.multiple_of((rho + d * e) * sub + nn * run, run) for e in range(runs)]

            def gather(st, c0):
                return jnp.concatenate([perm_ref[pl.ds(s0, run), c0:c0 + G] for s0 in st],
                                       axis=0).astype(BF16)

            cur = [starts(rho, n) for rho, n in items]
            qb = [gather(cur[i], 0) for i in its]
            if with_prev:
                prv = [starts(rho, n - 1) for rho, n in items]
                kcat = [jnp.concatenate([gather(prv[i], G), gather(cur[i], G)], axis=0) for i in its]
                vcat = [jnp.concatenate([gather(prv[i], 2 * G), gather(cur[i], 2 * G)], axis=0)
                        for i in its]
            else:
                kcat = [gather(cur[i], G) for i in its]
                vcat = [gather(cur[i], 2 * G) for i in its]
            zq = jnp.zeros_like(qb[0])
            kc0 = 0 if with_prev else blk
            pairs = [(0, 1), (2, 3)]
            s_p = [[_dot(jnp.concatenate([jnp.where(head_sel[h], qb[i], zq) for h in pr], axis=0),
                         kcat[i], _NT) for pr in pairs] for i in its]
            s = [[s_p[i][h // 2][(h % 2) * blk:(h % 2 + 1) * blk]
                  + bias_ref[bi * N_HEADS + h, :, kc0:2 * blk] for h in heads] for i in its]
            mh = [[jnp.max(s[i][h], axis=-1, keepdims=True) for h in heads] for i in its]
            ph = [[jnp.exp(s[i][h] - mh[i][h]) for h in heads] for i in its]
            lh = [[jnp.sum(ph[i][h], axis=-1, keepdims=True) for h in heads] for i in its]
            pv_p = [[_dot(jnp.concatenate([ph[i][h].astype(BF16) for h in pr], axis=0), vcat[i], _NN)
                     for pr in pairs] for i in its]
            acc_b = [sum(pv_p[i][h // 2][(h % 2) * blk:(h % 2 + 1) * blk] * masks[h] for h in heads)
                     for i in its]
            l_b = [sum(lh[i][h] * masks[h] for h in heads) for i in its]
            m_b = [sum(mh[i][h] * masks[h] for h in heads) for i in its]
            for i in its:
                outs = []
                for e, s0 in enumerate(cur[i]):
                    idx = pl.ds(s0, run)
                    sl = slice(e * run, (e + 1) * run)
                    if first:
                        acc_ref[idx, :] = acc_b[i][sl]
                        l_ref[idx, :] = l_b[i][sl]
                        m_ref[idx, :] = m_b[i][sl]
                    else:
                        m_old = m_ref[idx, :]
                        m_new = jnp.maximum(m_old, m_b[i][sl])
                        w_old = jnp.exp(m_old - m_new)
                        w_new = jnp.exp(m_b[i][sl] - m_new)
                        acc_n = acc_ref[idx, :] * w_old + acc_b[i][sl] * w_new
                        l_n = l_ref[idx, :] * w_old + l_b[i][sl] * w_new
                        if last:
                            outs.append(acc_n / l_n)
                        else:
                            acc_ref[idx, :] = acc_n
                            l_ref[idx, :] = l_n
                            m_ref[idx, :] = m_new
                if last:
                    r0 = pl.multiple_of(items[i][1] * blk, blk)
                    o_ref[pl.ds(r0, blk), :] = _mmx_l(from_perm, jnp.concatenate(outs, axis=0))

        def run_blocks(count, index, with_prev, blocks=blocks):
            def two(it, carry):
                blocks([index(2 * it), index(2 * it + 1)], with_prev)
                return carry

            lax.fori_loop(0, count // 2, two, 0)
            if count % 2:
                blocks([index(count - 1)], with_prev)

        run_blocks(d, lambda i: (i, 0), False)
        if nb > 1:
            run_blocks(d * (nb - 1), lambda i, nb=nb: (i // (nb - 1), i % (nb - 1) + 1), True)


def _attention(at, batch, seq):
    m = batch * seq
    return pl.pallas_call(
        _attn_kernel,
        grid=(batch,),
        in_specs=[pl.BlockSpec((seq, AT_W), lambda b: (b, 0))],
        out_specs=pl.BlockSpec((seq, D_GROUP), lambda b: (b, 0)),
        out_shape=jax.ShapeDtypeStruct((m, D_GROUP), F32),
        scratch_shapes=[pltpu.VMEM((seq, AT_W), F32)] + [pltpu.VMEM((seq, D_GROUP), F32)] * 3
        + [pltpu.VMEM((len(DILATIONS) * N_HEADS, ATTN_BLK, 2 * ATTN_BLK), F32)],
        compiler_params=pltpu.CompilerParams(dimension_semantics=("parallel",),
                                             vmem_limit_bytes=VMEM_LIMIT),
    )(at)


def _ssd_kernel(f_ref, fprev_ref, cw_ref, cb_ref, dtb_ref, vec_ref, ex_ref, y_ref, st_ref):
    c = pl.program_id(1)
    T = SSD_T
    G = D_GROUP
    N = SSD_STATE

    @pl.when(c == 0)
    def _():
        st_ref[...] = jnp.zeros_like(st_ref)

    xbc_raw = f_ref[:, G:G + SSD_XBC]
    prev8 = jnp.where(c == 0, 0.0, fprev_ref[:, G:G + SSD_XBC])
    row8 = _iota((8, 1), 0)
    conv = xbc_raw * cw_ref[3:4, :] + cb_ref[...]
    for j in range(1, 4):
        rolled = pltpu.roll(xbc_raw, j, axis=0)
        head = jnp.where(row8 < j, pltpu.roll(prev8, j, axis=0), rolled[0:8])
        conv = conv + jnp.concatenate([head, rolled[8:]], axis=0) * cw_ref[3 - j:4 - j, :]
    xbc = _silu(conv)
    xs = xbc[:, 0:G]
    bm = xbc[:, G:G + 2 * N]
    cm = xbc[:, G + 2 * N:G + 4 * N]
    z = f_ref[:, 0:G]

    a_dense, d_dense, norm_w = (vec_ref[i:i + 1, :] for i in range(3))
    dt = _softplus(f_ref[:, G + SSD_XBC:G + SSD_XBC + 128] + dtb_ref[...])
    dt_dense = _mmx_r(dt, ex_ref[...], pieces=2)
    da = dt_dense * a_dense
    ltri = (_iota((T, T), 1) <= _iota((T, T), 0))
    ltri_b = ltri.astype(BF16)
    cs = _mmx_l(ltri_b, da, pieces=2)
    cs_last = cs[T - 1:T, :]
    xdt = xs * dt_dense
    umat = (_iota((T, T), 0) > _iota((T, T), 1)).astype(F32)

    masks = _head_masks()
    y_diag = jnp.zeros((T, G), F32)
    scores = [_mm(cm[:, g * N:(g + 1) * N], bm[:, g * N:(g + 1) * N], _NT) for g in range(2)]
    for h in range(N_HEADS):
        da_col = jnp.broadcast_to(da[:, h * HEAD_DIM:h * HEAD_DIM + 1], (T, T))
        seg = _mmx_l(ltri_b, da_col * umat, pieces=2)
        dec = jnp.where(ltri, jnp.exp(jnp.where(ltri, seg, 0.0)), 0.0)
        y_diag = y_diag + _mm(scores[h // 2] * dec, xdt * masks[h])

    st = st_ref[...]
    y_off = jnp.concatenate([_mm(cm[:, 0:N], st[:, 0:N]), _mm(cm[:, N:2 * N], st[:, N:2 * N])],
                            axis=1) * jnp.exp(cs)
    xd = xdt * jnp.exp(cs_last - cs)
    st_ref[...] = st * jnp.exp(cs_last) + jnp.concatenate(
        [_mm(bm[:, 0:N], xd[:, 0:N], _TN), _mm(bm[:, N:2 * N], xd[:, N:2 * N], _TN)], axis=1)

    y = (y_diag + y_off + xs * d_dense) * _silu(z)
    halves = []
    for g in range(2):
        yg = y[:, g * N:(g + 1) * N]
        halves.append(yg * lax.rsqrt(jnp.mean(yg * yg, axis=-1, keepdims=True) + RMS_EPS))
    y_ref[...] = jnp.concatenate(halves, axis=1) * norm_w


def _ssd(sd, cw, cb, dtb, vec, ex, batch, seq):
    nc = seq // SSD_T
    m = batch * seq
    const = lambda shape: pl.BlockSpec(shape, lambda b, c: (0, 0))
    return pl.pallas_call(
        _ssd_kernel,
        grid=(batch, nc),
        in_specs=[pl.BlockSpec((SSD_T, SD_W), lambda b, c: (b * nc + c, 0)),
                  pl.BlockSpec((8, SD_W),
                               lambda b, c: (jnp.maximum(b * (seq // 8) + c * (SSD_T // 8) - 1, 0), 0)),
                  const((4, SSD_XBC)), const((1, SSD_XBC)), const((1, 128)), const((8, D_GROUP)),
                  const((128, D_GROUP))],
        out_specs=pl.BlockSpec((SSD_T, D_GROUP), lambda b, c: (b * nc + c, 0)),
        out_shape=jax.ShapeDtypeStruct((m, D_GROUP), F32),
        scratch_shapes=[pltpu.VMEM((SSD_STATE, D_GROUP), F32)],
        compiler_params=pltpu.CompilerParams(dimension_semantics=("parallel", "arbitrary"),
                                             vmem_limit_bytes=VMEM_LIMIT),
    )(sd, sd, cw, cb, dtb, vec, ex)


def _hgrn_kernel(f_ref, vec_ref, y_ref, st_ref):
    c = pl.program_id(1)
    T = HG_T
    C = HG_C
    G = D_GROUP
    nblk = T // C

    @pl.when(c == 0)
    def _():
        st_ref[...] = jnp.zeros_like(st_ref)

    lb = vec_ref[0:1, :]
    norm_w = vec_ref[1:2, :]
    forget = lb + (1.0 - lb) * _sigmoid_tail(f_ref[:, G:2 * G])
    q = _silu(f_ref[:, 0:G])
    k = 1.0 - forget
    v = f_ref[:, 2 * G:3 * G]
    blk_tri = ((_iota((T, T), 0) // C == _iota((T, T), 1) // C)
               & (_iota((T, T), 1) <= _iota((T, T), 0))).astype(BF16)
    b = _mmx_l(blk_tri, jnp.log(forget), pieces=2)

    head_sel = [(_iota((1, G), 1) // HEAD_DIM) == h for h in range(N_HEADS)]
    bo = _same_head().astype(BF16)
    t_idx = _iota((C, 1), 0)
    blocks = lambda z: [z[i * C:(i + 1) * C] for i in range(nblk)]
    bq, qq, kq, vv = blocks(b), blocks(q), blocks(k), blocks(v)
    b_last = [bq[i][C - 1:C, :] for i in range(nblk)]

    upd = [_dot(_stack_heads(vv[i], head_sel), _stack_heads(kq[i] * jnp.exp(b_last[i] - bq[i]), head_sel),
                _TN) for i in range(nblk)]
    q_in = [qq[i] * jnp.exp(bq[i]) for i in range(nblk)]
    parts = []
    for i in range(nblk):
        for s in range(C):
            e = jnp.exp(bq[i] - bq[i][s:s + 1, :])
            parts.append(jnp.where(t_idx >= s, qq[i] * e * kq[i][s:s + 1, :], 0.0).astype(BF16))
    z = _dot(jnp.concatenate(parts, axis=0), bo, _NN)

    st = st_ref[...]
    outs = []
    for i in range(nblk):
        o = _mm(q_in[i], st, _NT)
        for s in range(C):
            r0 = (i * C + s) * C
            o = o + z[r0:r0 + C, :] * vv[i][s:s + 1, :]
        outs.append(o)
        st = st * jnp.exp(b_last[i]) + upd[i]
    st_ref[...] = st
    o = jnp.concatenate(outs, axis=0)
    ms = _mmx_r(o * o, bo, pieces=2) * (1.0 / HEAD_DIM)
    y_ref[...] = o * lax.rsqrt(ms + RMS_EPS) * norm_w * _silu(f_ref[:, 3 * G:4 * G])


def _hgrn(hg, vec, batch, seq):
    nc = seq // HG_T
    m = batch * seq
    return pl.pallas_call(
        _hgrn_kernel,
        grid=(batch, nc),
        in_specs=[pl.BlockSpec((HG_T, HG_W), lambda b, c: (b * nc + c, 0)),
                  pl.BlockSpec((8, D_GROUP), lambda b, c: (0, 0))],
        out_specs=pl.BlockSpec((HG_T, D_GROUP), lambda b, c: (b * nc + c, 0)),
        out_shape=jax.ShapeDtypeStruct((m, D_GROUP), F32),
        scratch_shapes=[pltpu.VMEM((D_GROUP, D_GROUP), F32)],
        compiler_params=pltpu.CompilerParams(dimension_semantics=("parallel", "arbitrary"),
                                             vmem_limit_bytes=VMEM_LIMIT),
    )(hg, vec)


def _post_kernel(x_ref, ya_ref, yb_ref, yc_ref, yd_ref, wo_ref, wu_ref, wd_ref, ln_ref, o_ref):
    x = x_ref[...]
    mix = jnp.zeros(x.shape, F32)
    for i, ref in enumerate((ya_ref, yb_ref, yc_ref, yd_ref)):
        mix = mix + jnp.dot(ref[...].astype(BF16), wo_ref[i * D_GROUP:(i + 1) * D_GROUP, :],
                            preferred_element_type=F32)
    x1 = _layer_norm(ALPHA * x + mix, ln_ref[0:1, :], ln_ref[1:2, :])
    xb = x1.astype(BF16)
    acc = ALPHA * x1
    for c0 in range(0, D_FF, FF_CHUNK):
        h = jnp.maximum(jnp.dot(xb, wu_ref[:, c0:c0 + FF_CHUNK], preferred_element_type=F32), 0.0)
        acc = acc + jnp.dot((h * h).astype(BF16), wd_ref[c0:c0 + FF_CHUNK, :],
                            preferred_element_type=F32)
    o_ref[...] = _layer_norm(acc, ln_ref[2:3, :], ln_ref[3:4, :])


def _post(x2d, ys, wo, wu, wd, ln):
    m = x2d.shape[0]
    row = lambda w: pl.BlockSpec((POST_TM, w), lambda i: (i, 0))
    res = lambda shape: pl.BlockSpec(shape, lambda i: (0, 0), pipeline_mode=pl.Buffered(1))
    return pl.pallas_call(
        _post_kernel,
        grid=(m // POST_TM,),
        in_specs=[row(D_MODEL)] + [row(D_GROUP)] * 4
        + [res((D_MODEL, D_MODEL)), res((D_MODEL, D_FF)), res((D_FF, D_MODEL)), res((8, D_MODEL))],
        out_specs=row(D_MODEL),
        out_shape=jax.ShapeDtypeStruct((m, D_MODEL), F32),
        compiler_params=pltpu.CompilerParams(dimension_semantics=("parallel",),
                                             vmem_limit_bytes=VMEM_LIMIT),
    )(x2d, *ys, wo, wu, wd, ln)


def _pad_cols(a, width):
    return jnp.pad(a, ((0, 0), (0, width - a.shape[1])))


def _pad_rows(a, top, total):
    return jnp.pad(a, ((top, total - top - a.shape[0]), (0, 0)))


def _rep_heads(v):
    return jnp.repeat(v.astype(F32), HEAD_DIM)[None, :]


def _rows8(rows, width):
    out = jnp.concatenate([r.reshape(1, width).astype(F32) for r in rows], axis=0)
    return jnp.pad(out, ((0, 8 - out.shape[0]), (0, 0)))


def kernel(x, lower_bounds, w_in, w_in_vres, mu_shift, mu_vres, rwkv_w0, rwkv_w2, rwkv_a0, rwkv_a2, rwkv_g2, rwkv_k_k, rwkv_k_a, rwkv_r_k, rwkv_lnx_w, rwkv_lnx_b, rwkv_v0, rwkv_v2, ssd_conv_w, ssd_conv_b, ssd_dt_bias, ssd_A_log, ssd_D, ssd_norm_w, hgrn_norm_w, w_out, ln1_w, ln1_b, w_up, w_down, ln2_w, ln2_b):
    batch, seq, _ = x.shape
    lb = jax.nn.softmax(lower_bounds.astype(F32), axis=0)
    lb = jnp.cumsum(lb, axis=0) - lb[0]
    expand = (jnp.arange(128)[:, None] == (jnp.arange(D_GROUP)[None, :] // HEAD_DIM)).astype(BF16)

    h = x.reshape(batch * seq, D_MODEL)
    v_first = None
    for l in range(DEPTH):
        wl = lax.optimization_barrier(w_in[l].astype(BF16))
        o1 = RWKV_COLS
        o2 = o1 + ATTN_COLS
        o3 = o2 + SSD_COLS
        rw_cols = wl[:, :o1]
        mu = mu_shift[l][None, :]
        if l > 0:
            rw_cols = jnp.concatenate([rw_cols, w_in_vres[l - 1].astype(BF16)], axis=1)
            mu = jnp.concatenate([mu, mu_vres[l - 1][None, :]], axis=1)
        w_cat = jnp.concatenate([_pad_cols(rw_cols, RW_W), wl[:, o1:o2], _pad_cols(wl[:, o2:o3], SD_W),
                                 wl[:, o3:]], axis=1).astype(BF16)
        rw, at, sd, hg = _project(h, w_cat)

        vec_rows = [rwkv_w0[l], rwkv_a0[l], rwkv_k_k[l], rwkv_k_a[l], rwkv_lnx_w[l], rwkv_lnx_b[l],
                    rwkv_r_k[l].reshape(-1)]
        if l > 0:
            vec_rows.append(rwkv_v0[l - 1])
        y_a, v_first = _rwkv(
            rw, v_first, _pad_cols(mu, RW_W), _rows8(vec_rows, D_GROUP),
            _pad_rows(rwkv_w2[l], 0, 128), _pad_rows(rwkv_a2[l], 32, 128), _pad_rows(rwkv_g2[l], 64, 128),
            _pad_rows(rwkv_v2[l - 1], 0, 128) if l > 0 else None, batch, seq)
        y_b = _attention(at, batch, seq)
        ssd_vec = _rows8([_rep_heads(-jnp.exp(ssd_A_log[l].astype(F32))), _rep_heads(ssd_D[l]),
                          ssd_norm_w[l]], D_GROUP)
        y_c = _ssd(sd, ssd_conv_w[l].astype(F32), ssd_conv_b[l][None, :],
                   _pad_cols(ssd_dt_bias[l][None, :], 128), ssd_vec, expand, batch, seq)
        y_d = _hgrn(hg, _rows8([lb[l], hgrn_norm_w[l]], D_GROUP), batch, seq)
        ln = _rows8([ln1_w[l], ln1_b[l], ln2_w[l], ln2_b[l]], D_MODEL)
        h = _post(h, (y_a, y_b, y_c, y_d), w_out[l].astype(BF16), w_up[l].astype(BF16),
                  w_down[l].astype(BF16), ln)
    return h.reshape(batch, seq, D_MODEL)
```

```python
import functools
import math

import jax
import jax.numpy as jnp
from jax import lax
from jax.experimental import pallas as pl
from jax.experimental.pallas import tpu as pltpu

F32 = jnp.float32
BF16 = jnp.bfloat16

D_MODEL = 1024
DEPTH = 2
D_GROUP = 256
HEAD_DIM = 64
N_HEADS = 4
D_FF = 4 * D_MODEL
ALPHA = (2.0 * DEPTH) ** 0.25
LN_EPS = 1e-5
RMS_EPS = 1e-5
RWKV_GN_EPS = HEAD_DIM * 1e-5
RWKV_COLS = 896
ATTN_COLS = 768
SSD_STATE = 128
SSD_XBC = 768
SSD_COLS = 1028
HGRN_COLS = 1024
DILATIONS = (1, 4, 16)
ATTN_BLK = 128
ALIBI_SLOPES = tuple(2.0 ** (-8.0 * (h + 1) / N_HEADS) for h in range(N_HEADS))

RW_W, AT_W, SD_W, HG_W = 1024, 768, 1152, 1024
PROJ_W = RW_W + AT_W + SD_W + HG_W

RW_T = 64
RW_NCH = 2
RW_NB = 4
RW_ROWS = RW_T * RW_NCH
SSD_T = 128
HG_C = 16
HG_T = 256
PROJ_TM = 512
POST_TM = 512
FF_CHUNK = 512

VMEM_LIMIT = 56 * 1024 * 1024


def _dot(a, b, dims):
    return lax.dot_general(a, b, (dims, ((), ())), preferred_element_type=F32)


_NN = ((1,), (0,))
_NT = ((1,), (1,))
_TN = ((0,), (0,))


def _mm(a, b, dims=_NN):
    return _dot(a.astype(BF16), b.astype(BF16), dims)


def _split(x, pieces):
    out = []
    for i in range(pieces):
        hi = x.astype(BF16)
        out.append(hi)
        if i + 1 < pieces:
            x = x - hi.astype(F32)
    return out


def _mmx_l(m01, x, dims=_NN, pieces=3):
    return sum(_dot(m01, xp, dims) for xp in _split(x, pieces))


def _mmx_r(x, m01, dims=_NN, pieces=3):
    return sum(_dot(xp, m01, dims) for xp in _split(x, pieces))


def _mm3(a, b, dims=_NN):
    ah = a.astype(BF16)
    al = (a - ah.astype(F32)).astype(BF16)
    bh = b.astype(BF16)
    bl = (b - bh.astype(F32)).astype(BF16)
    return _dot(ah, bh, dims) + _dot(ah, bl, dims) + _dot(al, bh, dims)


def _iota(shape, dim):
    return lax.broadcasted_iota(jnp.int32, shape, dim)


def _head_masks(width=D_GROUP):
    lane = _iota((1, width), 1) // HEAD_DIM
    return [(lane == h).astype(F32) for h in range(N_HEADS)]


def _same_head(n=D_GROUP):
    return (_iota((n, n), 0) // HEAD_DIM) == (_iota((n, n), 1) // HEAD_DIM)


def _sigmoid(x):
    return 0.5 * jnp.tanh(0.5 * x) + 0.5


def _sigmoid_tail(x):
    return 1.0 / (1.0 + jnp.exp(-x))


def _silu(x):
    return x * _sigmoid(x)


def _softplus(x):
    return jnp.maximum(x, 0.0) + jnp.log(1.0 + jnp.exp(-jnp.abs(x)))


def _layer_norm(x, w, b):
    mu = jnp.mean(x, axis=-1, keepdims=True)
    d = x - mu
    var = jnp.mean(d * d, axis=-1, keepdims=True)
    return d * lax.rsqrt(var + LN_EPS) * w + b


def _shift_rows(x, prev_row):
    row = _iota((x.shape[0], 1), 0)
    return jnp.where(row == 0, prev_row, pltpu.roll(x, 1, axis=0))


def _proj_kernel(has_vres, *refs):
    if has_vres:
        x_ref, w_ref, wv_ref, rw_ref, at_ref, sd_ref, hg_ref = refs
    else:
        x_ref, w_ref, rw_ref, at_ref, sd_ref, hg_ref = refs
    xb = x_ref[...].astype(BF16)
    n_in = w_ref.shape[1]

    def emit(ref, c_out, c_in, width):
        wb = w_ref[:, c_in:c_in + width].astype(BF16)
        ref[:, c_out:c_out + width] = jnp.dot(xb, wb, preferred_element_type=F32)

    src = 0
    for ref, cols, padded in ((rw_ref, RWKV_COLS, RW_W), (at_ref, ATTN_COLS, AT_W),
                              (sd_ref, SSD_COLS, SD_W), (hg_ref, HGRN_COLS, HG_W)):
        for c0 in range(0, padded, 256):
            width = min(256, padded - c0)
            if c0 + width <= cols:
                emit(ref, c0, src + c0, width)
                continue
            real = max(cols - c0, 0)
            real_pad = -(-real // 128) * 128
            if real:
                take = min(real_pad, n_in - (src + c0))
                wb = w_ref[:, src + c0:src + c0 + take].astype(BF16)
                lane = _iota((1, take), 1)
                wb = jnp.where(lane < real, wb, jnp.zeros_like(wb))
                ref[:, c0:c0 + take] = jnp.dot(xb, wb, preferred_element_type=F32)
                if take < real_pad:
                    ref[:, c0 + take:c0 + real_pad] = jnp.zeros((xb.shape[0], real_pad - take), F32)
            rest = width - real_pad
            if rest:
                if has_vres and ref is rw_ref:
                    ref[:, c0 + real_pad:c0 + width] = jnp.dot(xb, wv_ref[...], preferred_element_type=F32)
                else:
                    ref[:, c0 + real_pad:c0 + width] = jnp.zeros((xb.shape[0], rest), F32)
        src += cols


def _project(x2d, w_in, layer, w_vres):
    m = x2d.shape[0]
    n_in = w_in.shape[2]
    has_vres = w_vres is not None
    outs = [jax.ShapeDtypeStruct((m, w), F32) for w in (RW_W, AT_W, SD_W, HG_W)]
    in_specs = [pl.BlockSpec((PROJ_TM, D_MODEL), lambda i: (i, 0)),
                pl.BlockSpec((None, D_MODEL, n_in), lambda i: (layer, 0, 0), pipeline_mode=pl.Buffered(1))]
    args = [x2d, w_in]
    if has_vres:
        in_specs.append(pl.BlockSpec((D_MODEL, 128), lambda i: (0, 0)))
        args.append(w_vres)
    return pl.pallas_call(
        functools.partial(_proj_kernel, has_vres),
        grid=(m // PROJ_TM,),
        in_specs=in_specs,
        out_specs=[pl.BlockSpec((PROJ_TM, w), lambda i: (i, 0)) for w in (RW_W, AT_W, SD_W, HG_W)],
        out_shape=outs,
        compiler_params=pltpu.CompilerParams(dimension_semantics=("parallel",),
                                             vmem_limit_bytes=VMEM_LIMIT),
    )(*args)


def _stack_heads(x, head_sel):
    xb = x.astype(BF16)
    zero = jnp.zeros_like(xb)
    return jnp.concatenate([jnp.where(m, xb, zero) for m in head_sel], axis=0)


def _rwkv_chunks(at, bh, kh, rt, v, head_sel):
    n = len(at)
    rng = range(n)
    T = at[0].shape[0]
    G = D_GROUP
    row = _iota((T, G), 0)
    lane_t = _iota((T, G), 1) % T
    strict = lane_t < row
    incl = lane_t <= row
    st = lambda x: _stack_heads(x, head_sel)
    lhs = [jnp.concatenate([at[j], rt[j]], axis=0).astype(BF16) for j in rng]
    ab = [_dot(lhs[j], st(bh[j]), _NT) for j in rng]
    ak = [_dot(lhs[j], st(kh[j]), _NT) for j in rng]
    n_u = [jnp.where(strict, ab[j][0:T], 0.0) for j in rng]
    a_ak = [jnp.where(strict, ak[j][0:T], 0.0) for j in rng]
    m_rb = [jnp.where(incl, ab[j][T:2 * T], 0.0).astype(BF16) for j in rng]
    m_rk = [jnp.where(incl, ak[j][T:2 * T], 0.0).astype(BF16) for j in rng]

    eye = jnp.where(lane_t == row, 1.0, 0.0)
    x_u = [eye + n_u[j] for j in rng]
    pw = [_dot(n_u[j].astype(BF16), st(n_u[j]), _NN) for j in rng]
    steps = T.bit_length() - 2
    for it in range(steps):
        w = [st(pw[j]) for j in rng]
        if it + 1 < steps:
            res = [_dot(jnp.concatenate([pw[j], x_u[j]], axis=0).astype(BF16), w[j], _NN) for j in rng]
            pw = [res[j][0:T] for j in rng]
            x_u = [x_u[j] + res[j][T:2 * T] for j in rng]
        else:
            x_u = [x_u[j] + _dot(x_u[j].astype(BF16), w[j], _NN) for j in rng]

    v_s = [st(v[j]) for j in rng]
    akv = [_dot(a_ak[j].astype(BF16), v_s[j], _NN) for j in rng]
    xb = [x_u[j].astype(BF16) for j in rng]
    p_u = [_dot(xb[j], st(at[j]), _NN) for j in rng]
    q_u = [_dot(xb[j], st(akv[j]), _NN) for j in rng]
    mkv = [_dot(m_rk[j], v_s[j], _NN) for j in rng]
    return p_u, q_u, m_rb, mkv


def _rwkv_kernel(has_vres, *refs):
    if has_vres:
        (f_ref, fprev_ref, vf_ref, mu_ref, vec_ref, w2_ref, a2_ref, g2_ref, v2_ref,
         y_ref, s_ref) = refs
    else:
        (f_ref, fprev_ref, mu_ref, vec_ref, w2_ref, a2_ref, g2_ref,
         y_ref, vout_ref, s_ref) = refs
    c = pl.program_id(1)
    T = RW_T
    R = RW_ROWS * RW_NB
    G = D_GROUP
    nchunks = RW_NCH * RW_NB

    @pl.when(c == 0)
    def _():
        s_ref[...] = jnp.zeros_like(s_ref)

    f = f_ref[...].reshape(R, RW_W)
    row = _iota((R, 1), 0)
    shifted = pltpu.roll(f, 1, axis=0)
    for b in range(RW_NB):
        prev_row = jnp.where(c == 0, 0.0, fprev_ref[b, 7:8, :])
        shifted = jnp.where(row == b * RW_ROWS, prev_row, shifted)
    x = f + (shifted - f) * mu_ref[...]
    r = x[:, 0:G]
    k = x[:, G:2 * G]
    v = x[:, 2 * G:3 * G]
    seg = x[:, 3 * G:3 * G + 128]
    w0, a0, k_k, k_a = (vec_ref[i:i + 1, :] for i in range(4))
    lnx_w, lnx_b, r_k = (vec_ref[i:i + 1, :] for i in range(4, 7))

    p = w0 + _mm3(jnp.tanh(seg), w2_ref[...])
    lw = -math.exp(-0.5) * _sigmoid(p)
    a = _sigmoid(a0 + _mm(seg, a2_ref[...]))
    g = _mm(_sigmoid(seg), g2_ref[...])
    if has_vres:
        seg2 = x[:, 3 * G + 128:3 * G + 256]
        v0 = vec_ref[7:8, :]
        v = v + (vf_ref[...].reshape(R, G) - v) * _sigmoid(v0 + _mm(seg2, v2_ref[...]))
    else:
        vout_ref[...] = v.reshape(RW_NB, RW_ROWS, G)

    head_sel = [(_iota((1, G), 1) // HEAD_DIM) == h for h in range(N_HEADS)]
    same = _same_head()
    bo = same.astype(BF16)
    kk = k * k_k
    kk = kk * lax.rsqrt(jnp.maximum(_mm(kk * kk, bo), 1e-24))
    k2 = k * (1.0 + (a - 1.0) * k_a)
    kka = kk * a

    ltri = (_iota((T, T), 1) <= _iota((T, T), 0)).astype(BF16)
    cs = jnp.concatenate([_mmx_l(ltri, lw[j * T:(j + 1) * T], pieces=2) for j in range(nchunks)], axis=0)
    c_last = jnp.concatenate(
        [jnp.broadcast_to(cs[(j + 1) * T - 1:(j + 1) * T, :], (T, G)) for j in range(nchunks)], axis=0)
    e_neg = jnp.exp(-cs)
    e_dec = jnp.exp(c_last - cs)
    at = -kk * jnp.exp(cs - lw)
    bh = kka * e_neg
    kh = k2 * e_neg
    rt = r * jnp.exp(cs)
    bdec = kka * e_dec
    kdec = k2 * e_dec

    chunks = lambda z: [z[j * T:(j + 1) * T] for j in range(nchunks)]
    rt_c, v_c, bdec_c, kdec_c = chunks(rt), chunks(v), chunks(bdec), chunks(kdec)
    p_u, q_u, m_rb, mkv = _rwkv_chunks(chunks(at), chunks(bh), chunks(kh), rt_c, v_c, head_sel)
    rp_lhs = [jnp.concatenate([rt_c[i], p_u[i]], axis=0).astype(BF16) for i in range(nchunks)]
    dec_t = [jnp.concatenate([bdec_c[i], kdec_c[i]], axis=0).T.astype(BF16) for i in range(nchunks)]
    w_col = [jnp.exp(jnp.broadcast_to(c_last[i * T:i * T + 1, :], (128, G))).T[:, 0:1]
             for i in range(nchunks)]
    s = [s_ref[b] for b in range(RW_NB)]
    outs = [None] * nchunks
    for j in range(RW_NCH):
        for b in range(RW_NB):
            i = b * RW_NCH + j
            rp = _dot(rp_lhs[i], s[b].astype(BF16), _NN)
            sa = rp[T:2 * T] + q_u[i]
            outs[i] = rp[0:T] + _dot(m_rb[i], _stack_heads(sa, head_sel), _NN) + mkv[i]
            upd = _dot(dec_t[i], jnp.concatenate([sa, v_c[i]], axis=0).astype(BF16), _NN)
            s[b] = s[b] * w_col[i] + jnp.where(same, upd, 0.0)
    for b in range(RW_NB):
        s_ref[b] = s[b]
    o = jnp.concatenate(outs, axis=0)

    mean = _mmx_r(o, bo, pieces=2) * (1.0 / HEAD_DIM)
    d = o - mean
    var = _mm(d * d, bo) * (1.0 / HEAD_DIM)
    yn = d * lax.rsqrt(var + RWKV_GN_EPS) * lnx_w + lnx_b
    bonus = _mm(r * k2 * r_k, bo) * v
    y_ref[...] = ((yn + bonus) * g).reshape(RW_NB, RW_ROWS, G)


def _rwkv(rw, v_first, mu, vec, w2p, a2p, g2p, v2p, batch, seq):
    has_vres = v_first is not None
    nc = seq // RW_ROWS
    m = batch * seq
    row_spec = lambda w: pl.BlockSpec((RW_NB, RW_ROWS, w), lambda b, c: (b, c, 0))
    const = lambda shape: pl.BlockSpec(shape, lambda b, c: (0, 0))
    prev_spec = pl.BlockSpec(
        (RW_NB, 8, RW_W), lambda b, c: (b, jnp.maximum(c * (RW_ROWS // 8) - 1, 0), 0))
    rw3 = rw.reshape(batch, seq, RW_W)
    in_specs = [row_spec(RW_W), prev_spec]
    args = [rw3, rw3]
    if has_vres:
        in_specs.append(row_spec(D_GROUP))
        args.append(v_first.reshape(batch, seq, D_GROUP))
    in_specs += [const((1, RW_W)), const((8, D_GROUP)), const((128, D_GROUP)),
                 const((128, D_GROUP)), const((128, D_GROUP))]
    args += [mu, vec, w2p, a2p, g2p]
    if has_vres:
        in_specs.append(const((128, D_GROUP)))
        args.append(v2p)
    out_shape = [jax.ShapeDtypeStruct((batch, seq, D_GROUP), F32)]
    out_specs = [row_spec(D_GROUP)]
    if not has_vres:
        out_shape.append(jax.ShapeDtypeStruct((batch, seq, D_GROUP), F32))
        out_specs.append(row_spec(D_GROUP))
    outs = pl.pallas_call(
        functools.partial(_rwkv_kernel, has_vres),
        grid=(batch // RW_NB, nc),
        in_specs=in_specs,
        out_specs=out_specs,
        out_shape=out_shape,
        scratch_shapes=[pltpu.VMEM((RW_NB, D_GROUP, D_GROUP), F32)],
        compiler_params=pltpu.CompilerParams(dimension_semantics=("parallel", "arbitrary"),
                                             vmem_limit_bytes=VMEM_LIMIT),
    )(*args)
    y = outs[0].reshape(m, D_GROUP)
    if has_vres:
        return y, v_first
    return y, outs[1].reshape(m, D_GROUP)


def _attn_kernel(qkv_ref, o_ref, perm_ref, acc_ref, l_ref, m_ref, bias_ref):
    G = D_GROUP
    blk = ATTN_BLK
    seq = qkv_ref.shape[0]
    nres = DILATIONS[-1]
    sub = seq // nres
    head_sel = [(_iota((1, G), 1) // HEAD_DIM) == h for h in range(N_HEADS)]
    scale = HEAD_DIM ** -0.5

    def jmap(a, d):
        run = blk * d // nres
        return (nres // d) * (a % run) + a // run

    to_perm = (_iota((blk, blk), 1) == jmap(_iota((blk, blk), 0), 1)).astype(BF16)
    from_perm = (_iota((blk, blk), 0) == jmap(_iota((blk, blk), 1), 1)).astype(BF16)
    run1 = blk // nres
    col_scale = jnp.where(_iota((1, AT_W), 1) < G, scale, 1.0)

    def permute(n, carry):
        r0 = pl.multiple_of(n * blk, blk)
        pb = jnp.dot(to_perm, (qkv_ref[pl.ds(r0, blk), :] * col_scale).astype(BF16),
                     preferred_element_type=F32)
        for e in range(nres):
            dst = pl.multiple_of(e * sub + n * run1, run1)
            perm_ref[pl.ds(dst, run1), :] = pb[e * run1:(e + 1) * run1, :]
        return carry

    lax.fori_loop(0, seq // blk, permute, 0)

    branches = tuple(reversed(DILATIONS))
    for bi, d in enumerate(branches):
        jq = jmap(_iota((blk, 2 * blk), 0), d)
        kb = _iota((blk, 2 * blk), 1)
        dist = blk + jq - (jmap(kb % blk, d) + (kb // blk) * blk)
        in_band = (dist >= 0) & (dist <= blk)
        for h in range(N_HEADS):
            bias_ref[bi * N_HEADS + h] = jnp.where(
                in_band, dist.astype(F32) * (-ALIBI_SLOPES[h] * d), -jnp.inf)

    for bi, d in enumerate(branches):
        runs = nres // d
        run = blk // runs
        nb = seq // (d * blk)
        first = bi == 0
        last = bi == len(branches) - 1

        def blocks(items, with_prev, bi=bi, d=d, runs=runs, run=run, first=first, last=last):
            its = range(len(items))
            heads = range(N_HEADS)

            def starts(rho, nn):
                return [pl.multiple_of((rho + d * e) * sub + nn * run, run) for e in range(runs)]

            def gather(st, c0):
                return jnp.concatenate([perm_ref[pl.ds(s0, run), c0:c0 + G] for s0 in st],
                                       axis=0).astype(BF16)

            cur = [starts(rho, n) for rho, n in items]
            qb = [gather(cur[i], 0) for i in its]
            if with_prev:
                prv = [starts(rho, n - 1) for rho, n in items]
                kcat = [jnp.concatenate([gather(prv[i], G), gather(cur[i], G)], axis=0) for i in its]
                vcat = [jnp.concatenate([gather(prv[i], 2 * G), gather(cur[i], 2 * G)], axis=0)
                        for i in its]
            else:
                kcat = [gather(cur[i], G) for i in its]
                vcat = [gather(cur[i], 2 * G) for i in its]
            zq = jnp.zeros_like(qb[0])
            kc0 = 0 if with_prev else blk
            pairs = [(0, 1), (2, 3)]
            s_p = [[_dot(jnp.concatenate([jnp.where(head_sel[h], qb[i], zq) for h in pr], axis=0),
                         kcat[i], _NT) for pr in pairs] for i in its]
            s = [[s_p[i][h // 2][(h % 2) * blk:(h % 2 + 1) * blk]
                  + bias_ref[bi * N_HEADS + h, :, kc0:2 * blk] for h in heads] for i in its]
            mh = [[jnp.max(s[i][h], axis=-1, keepdims=True) for h in heads] for i in its]
            ph = [[jnp.exp(s[i][h] - mh[i][h]) for h in heads] for i in its]
            lh = [[jnp.sum(ph[i][h], axis=-1, keepdims=True) for h in heads] for i in its]
            pv_p = [[_dot(jnp.concatenate([ph[i][h].astype(BF16) for h in pr], axis=0), vcat[i], _NN)
                     for pr in pairs] for i in its]
            def by_head(parts):
                out = jnp.broadcast_to(parts[N_HEADS - 1], (blk, G))
                for h in reversed(range(N_HEADS - 1)):
                    out = jnp.where(head_sel[h], parts[h], out)
                return out

            acc_b = [by_head([pv_p[i][h // 2][(h % 2) * blk:(h % 2 + 1) * blk] for h in heads])
                     for i in its]
            l_b = [by_head(lh[i]) for i in its]
            m_b = [by_head(mh[i]) for i in its]
            for i in its:
                outs = []
                for e, s0 in enumerate(cur[i]):
                    idx = pl.ds(s0, run)
                    sl = slice(e * run, (e + 1) * run)
                    if first:
                        acc_ref[idx, :] = acc_b[i][sl]
                        l_ref[idx, :] = l_b[i][sl]
                        m_ref[idx, :] = m_b[i][sl]
                    else:
                        m_old = m_ref[idx, :]
                        m_new = jnp.maximum(m_old, m_b[i][sl])
                        w_old = jnp.exp(m_old - m_new)
                        w_new = jnp.exp(m_b[i][sl] - m_new)
                        acc_n = acc_ref[idx, :] * w_old + acc_b[i][sl] * w_new
                        l_n = l_ref[idx, :] * w_old + l_b[i][sl] * w_new
                        if last:
                            outs.append(acc_n / l_n)
                        else:
                            acc_ref[idx, :] = acc_n
                            l_ref[idx, :] = l_n
                            m_ref[idx, :] = m_new
                if last:
                    r0 = pl.multiple_of(items[i][1] * blk, blk)
                    o_ref[pl.ds(r0, blk), :] = _mmx_l(from_perm, jnp.concatenate(outs, axis=0))

        def run_blocks(count, index, with_prev, blocks=blocks):
            def two(it, carry):
                blocks([index(2 * it), index(2 * it + 1)], with_prev)
                return carry

            lax.fori_loop(0, count // 2, two, 0)
            if count % 2:
                blocks([index(count - 1)], with_prev)

        run_blocks(d, lambda i: (i, 0), False)
        if nb > 1:
            run_blocks(d * (nb - 1), lambda i, nb=nb: (i // (nb - 1), i % (nb - 1) + 1), True)


def _attention(at, batch, seq):
    m = batch * seq
    return pl.pallas_call(
        _attn_kernel,
        grid=(batch,),
        in_specs=[pl.BlockSpec((seq, AT_W), lambda b: (b, 0))],
        out_specs=pl.BlockSpec((seq, D_GROUP), lambda b: (b, 0)),
        out_shape=jax.ShapeDtypeStruct((m, D_GROUP), F32),
        scratch_shapes=[pltpu.VMEM((seq, AT_W), F32)] + [pltpu.VMEM((seq, D_GROUP), F32)] * 3
        + [pltpu.VMEM((len(DILATIONS) * N_HEADS, ATTN_BLK, 2 * ATTN_BLK), F32)],
        compiler_params=pltpu.CompilerParams(dimension_semantics=("parallel",),
                                             vmem_limit_bytes=VMEM_LIMIT),
    )(at)


def _ssd_kernel(f_ref, fprev_ref, cw_ref, cb_ref, dtb_ref, vec_ref, ex_ref, y_ref, st_ref):
    c = pl.program_id(1)
    T = SSD_T
    G = D_GROUP
    N = SSD_STATE

    @pl.when(c == 0)
    def _():
        st_ref[...] = jnp.zeros_like(st_ref)

    xbc_raw = f_ref[:, G:G + SSD_XBC]
    prev8 = jnp.where(c == 0, 0.0, fprev_ref[:, G:G + SSD_XBC])
    row8 = _iota((8, 1), 0)
    conv = xbc_raw * cw_ref[3:4, :] + cb_ref[...]
    for j in range(1, 4):
        rolled = pltpu.roll(xbc_raw, j, axis=0)
        head = jnp.where(row8 < j, pltpu.roll(prev8, j, axis=0), rolled[0:8])
        conv = conv + jnp.concatenate([head, rolled[8:]], axis=0) * cw_ref[3 - j:4 - j, :]
    xbc = _silu(conv)
    xs = xbc[:, 0:G]
    bm = xbc[:, G:G + 2 * N]
    cm = xbc[:, G + 2 * N:G + 4 * N]
    z = f_ref[:, 0:G]

    a_dense, d_dense, norm_w = (vec_ref[i:i + 1, :] for i in range(3))
    dt = _softplus(f_ref[:, G + SSD_XBC:G + SSD_XBC + 128] + dtb_ref[...])
    dt_dense = _mmx_r(dt, ex_ref[...], pieces=2)
    da = dt_dense * a_dense
    ltri = (_iota((T, T), 1) <= _iota((T, T), 0))
    ltri_b = ltri.astype(BF16)
    cs = _mmx_l(ltri_b, da, pieces=2)
    cs_last = cs[T - 1:T, :]
    xdt = xs * dt_dense
    umat = (_iota((T, T), 0) > _iota((T, T), 1)).astype(F32)

    masks = _head_masks()
    y_diag = jnp.zeros((T, G), F32)
    scores = [_mm(cm[:, g * N:(g + 1) * N], bm[:, g * N:(g + 1) * N], _NT) for g in range(2)]
    for h in range(N_HEADS):
        da_col = jnp.broadcast_to(da[:, h * HEAD_DIM:h * HEAD_DIM + 1], (T, T))
        seg = _mmx_l(ltri_b, da_col * umat, pieces=2)
        dec = jnp.where(ltri, jnp.exp(jnp.where(ltri, seg, 0.0)), 0.0)
        y_diag = y_diag + _mm(scores[h // 2] * dec, xdt * masks[h])

    st = st_ref[...]
    y_off = jnp.concatenate([_mm(cm[:, 0:N], st[:, 0:N]), _mm(cm[:, N:2 * N], st[:, N:2 * N])],
                            axis=1) * jnp.exp(cs)
    xd = xdt * jnp.exp(cs_last - cs)
    st_ref[...] = st * jnp.exp(cs_last) + jnp.concatenate(
        [_mm(bm[:, 0:N], xd[:, 0:N], _TN), _mm(bm[:, N:2 * N], xd[:, N:2 * N], _TN)], axis=1)

    y = (y_diag + y_off + xs * d_dense) * _silu(z)
    halves = []
    for g in range(2):
        yg = y[:, g * N:(g + 1) * N]
        halves.append(yg * lax.rsqrt(jnp.mean(yg * yg, axis=-1, keepdims=True) + RMS_EPS))
    y_ref[...] = jnp.concatenate(halves, axis=1) * norm_w


def _ssd(sd, cw, cb, dtb, vec, ex, batch, seq):
    nc = seq // SSD_T
    m = batch * seq
    const = lambda shape: pl.BlockSpec(shape, lambda b, c: (0, 0))
    return pl.pallas_call(
        _ssd_kernel,
        grid=(batch, nc),
        in_specs=[pl.BlockSpec((SSD_T, SD_W), lambda b, c: (b * nc + c, 0)),
                  pl.BlockSpec((8, SD_W),
                               lambda b, c: (jnp.maximum(b * (seq // 8) + c * (SSD_T // 8) - 1, 0), 0)),
                  const((4, SSD_XBC)), const((1, SSD_XBC)), const((1, 128)), const((8, D_GROUP)),
                  const((128, D_GROUP))],
        out_specs=pl.BlockSpec((SSD_T, D_GROUP), lambda b, c: (b * nc + c, 0)),
        out_shape=jax.ShapeDtypeStruct((m, D_GROUP), F32),
        scratch_shapes=[pltpu.VMEM((SSD_STATE, D_GROUP), F32)],
        compiler_params=pltpu.CompilerParams(dimension_semantics=("parallel", "arbitrary"),
                                             vmem_limit_bytes=VMEM_LIMIT),
    )(sd, sd, cw, cb, dtb, vec, ex)


def _hgrn_kernel(f_ref, vec_ref, y_ref, st_ref):
    c = pl.program_id(1)
    T = HG_T
    C = HG_C
    G = D_GROUP
    nblk = T // C

    @pl.when(c == 0)
    def _():
        st_ref[...] = jnp.zeros_like(st_ref)

    lb = vec_ref[0:1, :]
    norm_w = vec_ref[1:2, :]
    forget = lb + (1.0 - lb) * _sigmoid_tail(f_ref[:, G:2 * G])
    q = _silu(f_ref[:, 0:G])
    k = 1.0 - forget
    v = f_ref[:, 2 * G:3 * G]
    blk_tri = ((_iota((T, T), 0) // C == _iota((T, T), 1) // C)
               & (_iota((T, T), 1) <= _iota((T, T), 0))).astype(BF16)
    b = _mmx_l(blk_tri, jnp.log(forget), pieces=2)

    head_sel = [(_iota((1, G), 1) // HEAD_DIM) == h for h in range(N_HEADS)]
    bo = _same_head().astype(BF16)
    t_idx = _iota((C, 1), 0)
    blocks = lambda z: [z[i * C:(i + 1) * C] for i in range(nblk)]
    bq, qq, kq, vv = blocks(b), blocks(q), blocks(k), blocks(v)
    b_last = [bq[i][C - 1:C, :] for i in range(nblk)]

    upd = [_dot(_stack_heads(vv[i], head_sel), _stack_heads(kq[i] * jnp.exp(b_last[i] - bq[i]), head_sel),
                _TN) for i in range(nblk)]
    q_in = [qq[i] * jnp.exp(bq[i]) for i in range(nblk)]
    half = C // 2
    o_intra = []
    for i in range(nblk):
        parts = []
        for s in range(C):
            lo = 0 if s < half else half
            e = jnp.exp(bq[i][lo:C] - bq[i][s:s + 1, :])
            parts.append(jnp.where(t_idx[lo:C] >= s, qq[i][lo:C] * e * kq[i][s:s + 1, :], 0.0))
        z = _mm(jnp.concatenate(parts, axis=0), bo)
        o_lo = z[0:half] * vv[i][0:1, :]
        o_hi = z[half:C] * vv[i][0:1, :]
        for s in range(1, half):
            o_lo = o_lo + z[s * C:s * C + half] * vv[i][s:s + 1, :]
            o_hi = o_hi + z[s * C + half:(s + 1) * C] * vv[i][s:s + 1, :]
        for s in range(half, C):
            r0 = half * C + (s - half) * half
            o_hi = o_hi + z[r0:r0 + half] * vv[i][s:s + 1, :]
        o_intra.append(jnp.concatenate([o_lo, o_hi], axis=0))

    st = st_ref[...]
    outs = []
    for i in range(nblk):
        outs.append(_mm(q_in[i], st, _NT) + o_intra[i])
        st = st * jnp.exp(b_last[i]) + upd[i]
    st_ref[...] = st
    o = jnp.concatenate(outs, axis=0)
    ms = _mmx_r(o * o, bo, pieces=2) * (1.0 / HEAD_DIM)
    y_ref[...] = o * lax.rsqrt(ms + RMS_EPS) * norm_w * _silu(f_ref[:, 3 * G:4 * G])


def _hgrn(hg, vec, batch, seq):
    nc = seq // HG_T
    m = batch * seq
    return pl.pallas_call(
        _hgrn_kernel,
        grid=(batch, nc),
        in_specs=[pl.BlockSpec((HG_T, HG_W), lambda b, c: (b * nc + c, 0)),
                  pl.BlockSpec((8, D_GROUP), lambda b, c: (0, 0))],
        out_specs=pl.BlockSpec((HG_T, D_GROUP), lambda b, c: (b * nc + c, 0)),
        out_shape=jax.ShapeDtypeStruct((m, D_GROUP), F32),
        scratch_shapes=[pltpu.VMEM((D_GROUP, D_GROUP), F32)],
        compiler_params=pltpu.CompilerParams(dimension_semantics=("parallel", "arbitrary"),
                                             vmem_limit_bytes=VMEM_LIMIT),
    )(hg, vec)


def _post_kernel(x_ref, ya_ref, yb_ref, yc_ref, yd_ref, wo_ref, wu_ref, wd_ref, ln_ref, o_ref):
    x = x_ref[...]
    mix = jnp.zeros(x.shape, F32)
    for i, ref in enumerate((ya_ref, yb_ref, yc_ref, yd_ref)):
        mix = mix + jnp.dot(ref[...].astype(BF16), wo_ref[i * D_GROUP:(i + 1) * D_GROUP, :],
                            preferred_element_type=F32)
    x1 = _layer_norm(ALPHA * x + mix, ln_ref[0:1, :], ln_ref[1:2, :])
    xb = x1.astype(BF16)
    acc = ALPHA * x1
    for c0 in range(0, D_FF, FF_CHUNK):
        h = jnp.maximum(jnp.dot(xb, wu_ref[:, c0:c0 + FF_CHUNK], preferred_element_type=F32), 0.0)
        acc = acc + jnp.dot((h * h).astype(BF16), wd_ref[c0:c0 + FF_CHUNK, :],
                            preferred_element_type=F32)
    o_ref[...] = _layer_norm(acc, ln_ref[2:3, :], ln_ref[3:4, :])


def _post(x2d, ys, wo, wu, wd, ln):
    m = x2d.shape[0]
    row = lambda w: pl.BlockSpec((POST_TM, w), lambda i: (i, 0))
    res = lambda shape: pl.BlockSpec(shape, lambda i: (0, 0), pipeline_mode=pl.Buffered(1))
    return pl.pallas_call(
        _post_kernel,
        grid=(m // POST_TM,),
        in_specs=[row(D_MODEL)] + [row(D_GROUP)] * 4
        + [res((D_MODEL, D_MODEL)), res((D_MODEL, D_FF)), res((D_FF, D_MODEL)), res((8, D_MODEL))],
        out_specs=row(D_MODEL),
        out_shape=jax.ShapeDtypeStruct((m, D_MODEL), F32),
        compiler_params=pltpu.CompilerParams(dimension_semantics=("parallel",),
                                             vmem_limit_bytes=VMEM_LIMIT),
    )(x2d, *ys, wo, wu, wd, ln)


def _pad_cols(a, width):
    return jnp.pad(a, ((0, 0), (0, width - a.shape[1])))


def _pad_rows(a, top, total):
    return jnp.pad(a, ((top, total - top - a.shape[0]), (0, 0)))


def _rep_heads(v):
    return jnp.repeat(v.astype(F32), HEAD_DIM)[None, :]


def _rows8(rows, width):
    out = jnp.concatenate([r.reshape(1, width).astype(F32) for r in rows], axis=0)
    return jnp.pad(out, ((0, 8 - out.shape[0]), (0, 0)))


def kernel(x, lower_bounds, w_in, w_in_vres, mu_shift, mu_vres, rwkv_w0, rwkv_w2, rwkv_a0, rwkv_a2, rwkv_g2, rwkv_k_k, rwkv_k_a, rwkv_r_k, rwkv_lnx_w, rwkv_lnx_b, rwkv_v0, rwkv_v2, ssd_conv_w, ssd_conv_b, ssd_dt_bias, ssd_A_log, ssd_D, ssd_norm_w, hgrn_norm_w, w_out, ln1_w, ln1_b, w_up, w_down, ln2_w, ln2_b):
    batch, seq, _ = x.shape
    lb = jax.nn.softmax(lower_bounds.astype(F32), axis=0)
    lb = jnp.cumsum(lb, axis=0) - lb[0]
    expand = (jnp.arange(128)[:, None] == (jnp.arange(D_GROUP)[None, :] // HEAD_DIM)).astype(BF16)

    h = x.reshape(batch * seq, D_MODEL)
    v_first = None
    for l in range(DEPTH):
        mu = mu_shift[l][None, :]
        w_vres = None
        if l > 0:
            w_vres = _pad_cols(w_in_vres[l - 1], 128).astype(BF16)
            mu = jnp.concatenate([mu, mu_vres[l - 1][None, :]], axis=1)
        rw, at, sd, hg = _project(h, w_in, l, w_vres)

        vec_rows = [rwkv_w0[l], rwkv_a0[l], rwkv_k_k[l], rwkv_k_a[l], rwkv_lnx_w[l], rwkv_lnx_b[l],
                    rwkv_r_k[l].reshape(-1)]
        if l > 0:
            vec_rows.append(rwkv_v0[l - 1])
        y_a, v_first = _rwkv(
            rw, v_first, _pad_cols(mu, RW_W), _rows8(vec_rows, D_GROUP),
            _pad_rows(rwkv_w2[l], 0, 128), _pad_rows(rwkv_a2[l], 32, 128), _pad_rows(rwkv_g2[l], 64, 128),
            _pad_rows(rwkv_v2[l - 1], 0, 128) if l > 0 else None, batch, seq)
        y_b = _attention(at, batch, seq)
        ssd_vec = _rows8([_rep_heads(-jnp.exp(ssd_A_log[l].astype(F32))), _rep_heads(ssd_D[l]),
                          ssd_norm_w[l]], D_GROUP)
        y_c = _ssd(sd, ssd_conv_w[l].astype(F32), ssd_conv_b[l][None, :],
                   _pad_cols(ssd_dt_bias[l][None, :], 128), ssd_vec, expand, batch, seq)
        y_d = _hgrn(hg, _rows8([lb[l], hgrn_norm_w[l]], D_GROUP), batch, seq)
        ln = _rows8([ln1_w[l], ln1_b[l], ln2_w[l], ln2_b[l]], D_MODEL)
        h = _post(h, (y_a, y_b, y_c, y_d), w_out[l].astype(BF16), w_up[l].astype(BF16),
                  w_down[l].astype(BF16), ln)
    return h.reshape(batch, seq, D_MODEL)
```

```python
import functools
import math

import jax
import jax.numpy as jnp
from jax import lax
from jax.experimental import pallas as pl
from jax.experimental.pallas import tpu as pltpu

F32 = jnp.float32
BF16 = jnp.bfloat16

D_MODEL = 1024
DEPTH = 2
D_GROUP = 256
HEAD_DIM = 64
N_HEADS = 4
D_FF = 4 * D_MODEL
ALPHA = (2.0 * DEPTH) ** 0.25
LN_EPS = 1e-5
RMS_EPS = 1e-5
RWKV_GN_EPS = HEAD_DIM * 1e-5
RWKV_COLS = 896
ATTN_COLS = 768
SSD_STATE = 128
SSD_XBC = 768
SSD_COLS = 1028
HGRN_COLS = 1024
DILATIONS = (1, 4, 16)
ATTN_BLK = 128
ALIBI_SLOPES = tuple(2.0 ** (-8.0 * (h + 1) / N_HEADS) for h in range(N_HEADS))

RW_W, AT_W, SD_W, HG_W = 1024, 768, 1152, 1024
PROJ_W = RW_W + AT_W + SD_W + HG_W

RW_T = 64
RW_NCH = 2
RW_NB = 4
RW_ROWS = RW_T * RW_NCH
SSD_T = 128
SSD_NB = 4
HG_C = 16
HG_T = 256
PROJ_TM = 512
POST_TM = 1024
FF_CHUNK = 512

VMEM_LIMIT = 56 * 1024 * 1024


def _dot(a, b, dims):
    return lax.dot_general(a, b, (dims, ((), ())), preferred_element_type=F32)


_NN = ((1,), (0,))
_NT = ((1,), (1,))
_TN = ((0,), (0,))


def _mm(a, b, dims=_NN):
    return _dot(a.astype(BF16), b.astype(BF16), dims)


def _split(x, pieces):
    out = []
    for i in range(pieces):
        hi = x.astype(BF16)
        out.append(hi)
        if i + 1 < pieces:
            x = x - hi.astype(F32)
    return out


def _mmx_l(m01, x, dims=_NN, pieces=3):
    return sum(_dot(m01, xp, dims) for xp in _split(x, pieces))


def _mmx_r(x, m01, dims=_NN, pieces=3):
    return sum(_dot(xp, m01, dims) for xp in _split(x, pieces))


def _mm3(a, b, dims=_NN):
    ah = a.astype(BF16)
    al = (a - ah.astype(F32)).astype(BF16)
    bh = b.astype(BF16)
    bl = (b - bh.astype(F32)).astype(BF16)
    return _dot(ah, bh, dims) + _dot(ah, bl, dims) + _dot(al, bh, dims)


def _iota(shape, dim):
    return lax.broadcasted_iota(jnp.int32, shape, dim)


def _head_masks(width=D_GROUP):
    lane = _iota((1, width), 1) // HEAD_DIM
    return [(lane == h).astype(F32) for h in range(N_HEADS)]


def _same_head(n=D_GROUP):
    return (_iota((n, n), 0) // HEAD_DIM) == (_iota((n, n), 1) // HEAD_DIM)


def _sigmoid(x):
    return 0.5 * jnp.tanh(0.5 * x) + 0.5


def _sigmoid_tail(x):
    return 1.0 / (1.0 + jnp.exp(-x))


def _silu(x):
    return x * _sigmoid(x)


def _softplus(x):
    return jnp.maximum(x, 0.0) + jnp.log(1.0 + jnp.exp(-jnp.abs(x)))


def _layer_norm(x, w, b):
    mu = jnp.mean(x, axis=-1, keepdims=True)
    d = x - mu
    var = jnp.mean(d * d, axis=-1, keepdims=True)
    return d * lax.rsqrt(var + LN_EPS) * w + b


def _shift_rows(x, prev_row):
    row = _iota((x.shape[0], 1), 0)
    return jnp.where(row == 0, prev_row, pltpu.roll(x, 1, axis=0))


def _proj_kernel(has_vres, *refs):
    if has_vres:
        x_ref, w_ref, wv_ref, rw_ref, at_ref, sd_ref, hg_ref = refs
    else:
        x_ref, w_ref, rw_ref, at_ref, sd_ref, hg_ref = refs
    xb = x_ref[...].astype(BF16)
    n_in = w_ref.shape[1]

    def emit(ref, c_out, c_in, width):
        wb = w_ref[:, c_in:c_in + width].astype(BF16)
        ref[:, c_out:c_out + width] = jnp.dot(xb, wb, preferred_element_type=F32)

    src = 0
    for ref, cols, padded in ((rw_ref, RWKV_COLS, RW_W), (at_ref, ATTN_COLS, AT_W),
                              (sd_ref, SSD_COLS, SD_W), (hg_ref, HGRN_COLS, HG_W)):
        for c0 in range(0, padded, 256):
            width = min(256, padded - c0)
            if c0 + width <= cols:
                emit(ref, c0, src + c0, width)
                continue
            real = max(cols - c0, 0)
            real_pad = -(-real // 128) * 128
            if real:
                take = min(real_pad, n_in - (src + c0))
                wb = w_ref[:, src + c0:src + c0 + take].astype(BF16)
                lane = _iota((1, take), 1)
                wb = jnp.where(lane < real, wb, jnp.zeros_like(wb))
                ref[:, c0:c0 + take] = jnp.dot(xb, wb, preferred_element_type=F32)
                if take < real_pad:
                    ref[:, c0 + take:c0 + real_pad] = jnp.zeros((xb.shape[0], real_pad - take), F32)
            rest = width - real_pad
            if rest:
                if has_vres and ref is rw_ref:
                    ref[:, c0 + real_pad:c0 + width] = jnp.dot(xb, wv_ref[...], preferred_element_type=F32)
                else:
                    ref[:, c0 + real_pad:c0 + width] = jnp.zeros((xb.shape[0], rest), F32)
        src += cols


def _project(x2d, w_in, layer, w_vres):
    m = x2d.shape[0]
    n_in = w_in.shape[2]
    has_vres = w_vres is not None
    outs = [jax.ShapeDtypeStruct((m, w), F32) for w in (RW_W, AT_W, SD_W, HG_W)]
    in_specs = [pl.BlockSpec((PROJ_TM, D_MODEL), lambda i: (i, 0)),
                pl.BlockSpec((None, D_MODEL, n_in), lambda i: (layer, 0, 0), pipeline_mode=pl.Buffered(1))]
    args = [x2d, w_in]
    if has_vres:
        in_specs.append(pl.BlockSpec((D_MODEL, 128), lambda i: (0, 0)))
        args.append(w_vres)
    return pl.pallas_call(
        functools.partial(_proj_kernel, has_vres),
        grid=(m // PROJ_TM,),
        in_specs=in_specs,
        out_specs=[pl.BlockSpec((PROJ_TM, w), lambda i: (i, 0)) for w in (RW_W, AT_W, SD_W, HG_W)],
        out_shape=outs,
        compiler_params=pltpu.CompilerParams(dimension_semantics=("parallel",),
                                             vmem_limit_bytes=VMEM_LIMIT),
    )(*args)


def _stack_heads(x, head_sel):
    xb = x.astype(BF16)
    zero = jnp.zeros_like(xb)
    return jnp.concatenate([jnp.where(m, xb, zero) for m in head_sel], axis=0)


def _rwkv_chunks(at, bh, kh, rt, v, head_sel):
    n = len(at)
    rng = range(n)
    T = at[0].shape[0]
    G = D_GROUP
    row = _iota((T, G), 0)
    lane_t = _iota((T, G), 1) % T
    strict = lane_t < row
    incl = lane_t <= row
    st = lambda x: _stack_heads(x, head_sel)
    lhs = [jnp.concatenate([at[j], rt[j]], axis=0).astype(BF16) for j in rng]
    ab = [_dot(lhs[j], st(bh[j]), _NT) for j in rng]
    ak = [_dot(lhs[j], st(kh[j]), _NT) for j in rng]
    n_u = [jnp.where(strict, ab[j][0:T], 0.0) for j in rng]
    a_ak = [jnp.where(strict, ak[j][0:T], 0.0) for j in rng]
    m_rb = [jnp.where(incl, ab[j][T:2 * T], 0.0).astype(BF16) for j in rng]
    m_rk = [jnp.where(incl, ak[j][T:2 * T], 0.0).astype(BF16) for j in rng]

    eye = jnp.where(lane_t == row, 1.0, 0.0)
    x_u = [eye + n_u[j] for j in rng]
    pw = [_dot(n_u[j].astype(BF16), st(n_u[j]), _NN) for j in rng]
    steps = T.bit_length() - 2
    for it in range(steps):
        w = [st(pw[j]) for j in rng]
        if it + 1 < steps:
            res = [_dot(jnp.concatenate([pw[j], x_u[j]], axis=0).astype(BF16), w[j], _NN) for j in rng]
            pw = [res[j][0:T] for j in rng]
            x_u = [x_u[j] + res[j][T:2 * T] for j in rng]
        else:
            x_u = [x_u[j] + _dot(x_u[j].astype(BF16), w[j], _NN) for j in rng]

    v_s = [st(v[j]) for j in rng]
    akv = [_dot(a_ak[j].astype(BF16), v_s[j], _NN) for j in rng]
    xb = [x_u[j].astype(BF16) for j in rng]
    p_u = [_dot(xb[j], st(at[j]), _NN) for j in rng]
    q_u = [_dot(xb[j], st(akv[j]), _NN) for j in rng]
    mkv = [_dot(m_rk[j], v_s[j], _NN) for j in rng]
    return p_u, q_u, m_rb, mkv


def _rwkv_kernel(has_vres, *refs):
    if has_vres:
        (f_ref, fprev_ref, vf_ref, mu_ref, vec_ref, w2_ref, a2_ref, g2_ref, v2_ref,
         y_ref, s_ref) = refs
    else:
        (f_ref, fprev_ref, mu_ref, vec_ref, w2_ref, a2_ref, g2_ref,
         y_ref, vout_ref, s_ref) = refs
    c = pl.program_id(1)
    T = RW_T
    R = RW_ROWS * RW_NB
    G = D_GROUP
    nchunks = RW_NCH * RW_NB

    @pl.when(c == 0)
    def _():
        s_ref[...] = jnp.zeros_like(s_ref)

    f = f_ref[...].reshape(R, RW_W)
    row = _iota((R, 1), 0)
    shifted = pltpu.roll(f, 1, axis=0)
    for b in range(RW_NB):
        prev_row = jnp.where(c == 0, 0.0, fprev_ref[b, 7:8, :])
        shifted = jnp.where(row == b * RW_ROWS, prev_row, shifted)
    x = f + (shifted - f) * mu_ref[...]
    r = x[:, 0:G]
    k = x[:, G:2 * G]
    v = x[:, 2 * G:3 * G]
    seg = x[:, 3 * G:3 * G + 128]
    w0, a0, k_k, k_a = (vec_ref[i:i + 1, :] for i in range(4))
    lnx_w, lnx_b, r_k = (vec_ref[i:i + 1, :] for i in range(4, 7))

    p = w0 + _mm3(jnp.tanh(seg), w2_ref[...])
    lw = -math.exp(-0.5) * _sigmoid(p)
    a = _sigmoid(a0 + _mm(seg, a2_ref[...]))
    g = _mm(_sigmoid(seg), g2_ref[...])
    if has_vres:
        seg2 = x[:, 3 * G + 128:3 * G + 256]
        v0 = vec_ref[7:8, :]
        v = v + (vf_ref[...].reshape(R, G) - v) * _sigmoid(v0 + _mm(seg2, v2_ref[...]))
    else:
        vout_ref[...] = v.reshape(RW_NB, RW_ROWS, G)

    head_sel = [(_iota((1, G), 1) // HEAD_DIM) == h for h in range(N_HEADS)]
    same = _same_head()
    bo = same.astype(BF16)
    kk = k * k_k
    kk = kk * lax.rsqrt(jnp.maximum(_mm(kk * kk, bo), 1e-24))
    k2 = k * (1.0 + (a - 1.0) * k_a)
    kka = kk * a

    ltri = (_iota((T, T), 1) <= _iota((T, T), 0)).astype(BF16)
    cs = jnp.concatenate([_mmx_l(ltri, lw[j * T:(j + 1) * T], pieces=2) for j in range(nchunks)], axis=0)
    c_last = jnp.concatenate(
        [jnp.broadcast_to(cs[(j + 1) * T - 1:(j + 1) * T, :], (T, G)) for j in range(nchunks)], axis=0)
    e_neg = jnp.exp(-cs)
    e_dec = jnp.exp(c_last - cs)
    at = -kk * jnp.exp(cs - lw)
    bh = kka * e_neg
    kh = k2 * e_neg
    rt = r * jnp.exp(cs)
    bdec = kka * e_dec
    kdec = k2 * e_dec

    chunks = lambda z: [z[j * T:(j + 1) * T] for j in range(nchunks)]
    rt_c, v_c, bdec_c, kdec_c = chunks(rt), chunks(v), chunks(bdec), chunks(kdec)
    p_u, q_u, m_rb, mkv = _rwkv_chunks(chunks(at), chunks(bh), chunks(kh), rt_c, v_c, head_sel)
    rp_lhs = [jnp.concatenate([rt_c[i], p_u[i]], axis=0).astype(BF16) for i in range(nchunks)]
    dec_t = [jnp.concatenate([bdec_c[i], kdec_c[i]], axis=0).T.astype(BF16) for i in range(nchunks)]
    w_col = [jnp.exp(jnp.broadcast_to(c_last[i * T:i * T + 1, :], (128, G))).T[:, 0:1]
             for i in range(nchunks)]
    s = [s_ref[b] for b in range(RW_NB)]
    outs = [None] * nchunks
    for j in range(RW_NCH):
        for b in range(RW_NB):
            i = b * RW_NCH + j
            rp = _dot(rp_lhs[i], s[b].astype(BF16), _NN)
            sa = rp[T:2 * T] + q_u[i]
            outs[i] = rp[0:T] + _dot(m_rb[i], _stack_heads(sa, head_sel), _NN) + mkv[i]
            upd = _dot(dec_t[i], jnp.concatenate([sa, v_c[i]], axis=0).astype(BF16), _NN)
            s[b] = s[b] * w_col[i] + jnp.where(same, upd, 0.0)
    for b in range(RW_NB):
        s_ref[b] = s[b]
    o = jnp.concatenate(outs, axis=0)

    mean = _mmx_r(o, bo, pieces=2) * (1.0 / HEAD_DIM)
    d = o - mean
    var = _mm(d * d, bo) * (1.0 / HEAD_DIM)
    yn = d * lax.rsqrt(var + RWKV_GN_EPS) * lnx_w + lnx_b
    bonus = _mm(r * k2 * r_k, bo) * v
    y_ref[...] = ((yn + bonus) * g).reshape(RW_NB, RW_ROWS, G)


def _rwkv(rw, v_first, mu, vec, w2p, a2p, g2p, v2p, batch, seq):
    has_vres = v_first is not None
    nc = seq // RW_ROWS
    m = batch * seq
    row_spec = lambda w: pl.BlockSpec((RW_NB, RW_ROWS, w), lambda b, c: (b, c, 0))
    const = lambda shape: pl.BlockSpec(shape, lambda b, c: (0, 0))
    prev_spec = pl.BlockSpec(
        (RW_NB, 8, RW_W), lambda b, c: (b, jnp.maximum(c * (RW_ROWS // 8) - 1, 0), 0))
    rw3 = rw.reshape(batch, seq, RW_W)
    in_specs = [row_spec(RW_W), prev_spec]
    args = [rw3, rw3]
    if has_vres:
        in_specs.append(row_spec(D_GROUP))
        args.append(v_first.reshape(batch, seq, D_GROUP))
    in_specs += [const((1, RW_W)), const((8, D_GROUP)), const((128, D_GROUP)),
                 const((128, D_GROUP)), const((128, D_GROUP))]
    args += [mu, vec, w2p, a2p, g2p]
    if has_vres:
        in_specs.append(const((128, D_GROUP)))
        args.append(v2p)
    out_shape = [jax.ShapeDtypeStruct((batch, seq, D_GROUP), F32)]
    out_specs = [row_spec(D_GROUP)]
    if not has_vres:
        out_shape.append(jax.ShapeDtypeStruct((batch, seq, D_GROUP), F32))
        out_specs.append(row_spec(D_GROUP))
    outs = pl.pallas_call(
        functools.partial(_rwkv_kernel, has_vres),
        grid=(batch // RW_NB, nc),
        in_specs=in_specs,
        out_specs=out_specs,
        out_shape=out_shape,
        scratch_shapes=[pltpu.VMEM((RW_NB, D_GROUP, D_GROUP), F32)],
        compiler_params=pltpu.CompilerParams(dimension_semantics=("parallel", "arbitrary"),
                                             vmem_limit_bytes=VMEM_LIMIT),
    )(*args)
    y = outs[0].reshape(m, D_GROUP)
    if has_vres:
        return y, v_first
    return y, outs[1].reshape(m, D_GROUP)


def _attn_kernel(qkv_ref, o_ref, perm_ref, acc_ref, l_ref, m_ref, bias_ref):
    G = D_GROUP
    blk = ATTN_BLK
    seq = qkv_ref.shape[0]
    nres = DILATIONS[-1]
    sub = seq // nres
    head_sel = [(_iota((1, G), 1) // HEAD_DIM) == h for h in range(N_HEADS)]
    scale = HEAD_DIM ** -0.5

    def jmap(a, d):
        run = blk * d // nres
        return (nres // d) * (a % run) + a // run

    to_perm = (_iota((blk, blk), 1) == jmap(_iota((blk, blk), 0), 1)).astype(BF16)
    from_perm = (_iota((blk, blk), 0) == jmap(_iota((blk, blk), 1), 1)).astype(BF16)
    run1 = blk // nres
    col_scale = jnp.where(_iota((1, AT_W), 1) < G, scale, 1.0)

    def permute(n, carry):
        r0 = pl.multiple_of(n * blk, blk)
        pb = jnp.dot(to_perm, (qkv_ref[pl.ds(r0, blk), :] * col_scale).astype(BF16),
                     preferred_element_type=F32)
        for e in range(nres):
            dst = pl.multiple_of(e * sub + n * run1, run1)
            perm_ref[pl.ds(dst, run1), :] = pb[e * run1:(e + 1) * run1, :]
        return carry

    lax.fori_loop(0, seq // blk, permute, 0)

    branches = tuple(reversed(DILATIONS))
    for bi, d in enumerate(branches):
        jq = jmap(_iota((blk, 2 * blk), 0), d)
        kb = _iota((blk, 2 * blk), 1)
        dist = blk + jq - (jmap(kb % blk, d) + (kb // blk) * blk)
        in_band = (dist >= 0) & (dist <= blk)
        for h in range(N_HEADS):
            bias_ref[bi * N_HEADS + h] = jnp.where(
                in_band, dist.astype(F32) * (-ALIBI_SLOPES[h] * d), -jnp.inf)

    for bi, d in enumerate(branches):
        runs = nres // d
        run = blk // runs
        nb = seq // (d * blk)
        first = bi == 0
        last = bi == len(branches) - 1

        def blocks(items, with_prev, bi=bi, d=d, runs=runs, run=run, first=first, last=last):
            its = range(len(items))
            heads = range(N_HEADS)

            def starts(rho, nn):
                return [pl.multiple_of((rho + d * e) * sub + nn * run, run) for e in range(runs)]

            def gather(st, c0):
                return jnp.concatenate([perm_ref[pl.ds(s0, run), c0:c0 + G] for s0 in st],
                                       axis=0).astype(BF16)

            cur = [starts(rho, n) for rho, n in items]
            qb = [gather(cur[i], 0) for i in its]
            if with_prev:
                prv = [starts(rho, n - 1) for rho, n in items]
                kcat = [jnp.concatenate([gather(prv[i], G), gather(cur[i], G)], axis=0) for i in its]
                vcat = [jnp.concatenate([gather(prv[i], 2 * G), gather(cur[i], 2 * G)], axis=0)
                        for i in its]
            else:
                kcat = [gather(cur[i], G) for i in its]
                vcat = [gather(cur[i], 2 * G) for i in its]
            zq = jnp.zeros_like(qb[0])
            kc0 = 0 if with_prev else blk
            pairs = [(0, 1), (2, 3)]
            s_p = [[_dot(jnp.concatenate([jnp.where(head_sel[h], qb[i], zq) for h in pr], axis=0),
                         kcat[i], _NT) for pr in pairs] for i in its]
            s = [[s_p[i][h // 2][(h % 2) * blk:(h % 2 + 1) * blk]
                  + bias_ref[bi * N_HEADS + h, :, kc0:2 * blk] for h in heads] for i in its]
            mh = [[jnp.max(s[i][h], axis=-1, keepdims=True) for h in heads] for i in its]
            ph = [[jnp.exp(s[i][h] - mh[i][h]) for h in heads] for i in its]
            lh = [[jnp.sum(ph[i][h], axis=-1, keepdims=True) for h in heads] for i in its]
            pv_p = [[_dot(jnp.concatenate([ph[i][h].astype(BF16) for h in pr], axis=0), vcat[i], _NN)
                     for pr in pairs] for i in its]
            def by_head(parts):
                out = jnp.broadcast_to(parts[N_HEADS - 1], (blk, G))
                for h in reversed(range(N_HEADS - 1)):
                    out = jnp.where(head_sel[h], parts[h], out)
                return out

            acc_b = [by_head([pv_p[i][h // 2][(h % 2) * blk:(h % 2 + 1) * blk] for h in heads])
                     for i in its]
            l_b = [by_head(lh[i]) for i in its]
            m_b = [by_head(mh[i]) for i in its]
            for i in its:
                outs = []
                for e, s0 in enumerate(cur[i]):
                    idx = pl.ds(s0, run)
                    sl = slice(e * run, (e + 1) * run)
                    if first:
                        acc_ref[idx, :] = acc_b[i][sl]
                        l_ref[idx, :] = l_b[i][sl]
                        m_ref[idx, :] = m_b[i][sl]
                    else:
                        m_old = m_ref[idx, :]
                        m_new = jnp.maximum(m_old, m_b[i][sl])
                        w_old = jnp.exp(m_old - m_new)
                        w_new = jnp.exp(m_b[i][sl] - m_new)
                        acc_n = acc_ref[idx, :] * w_old + acc_b[i][sl] * w_new
                        l_n = l_ref[idx, :] * w_old + l_b[i][sl] * w_new
                        if last:
                            outs.append(acc_n / l_n)
                        else:
                            acc_ref[idx, :] = acc_n
                            l_ref[idx, :] = l_n
                            m_ref[idx, :] = m_new
                if last:
                    r0 = pl.multiple_of(items[i][1] * blk, blk)
                    o_ref[pl.ds(r0, blk), :] = _mmx_l(from_perm, jnp.concatenate(outs, axis=0))

        def run_blocks(count, index, with_prev, blocks=blocks):
            def two(it, carry):
                blocks([index(2 * it), index(2 * it + 1)], with_prev)
                return carry

            lax.fori_loop(0, count // 2, two, 0)
            if count % 2:
                blocks([index(count - 1)], with_prev)

        run_blocks(d, lambda i: (i, 0), False)
        if nb > 1:
            run_blocks(d * (nb - 1), lambda i, nb=nb: (i // (nb - 1), i % (nb - 1) + 1), True)


def _attention(at, batch, seq):
    m = batch * seq
    return pl.pallas_call(
        _attn_kernel,
        grid=(batch,),
        in_specs=[pl.BlockSpec((seq, AT_W), lambda b: (b, 0))],
        out_specs=pl.BlockSpec((seq, D_GROUP), lambda b: (b, 0)),
        out_shape=jax.ShapeDtypeStruct((m, D_GROUP), F32),
        scratch_shapes=[pltpu.VMEM((seq, AT_W), F32)] + [pltpu.VMEM((seq, D_GROUP), F32)] * 3
        + [pltpu.VMEM((len(DILATIONS) * N_HEADS, ATTN_BLK, 2 * ATTN_BLK), F32)],
        compiler_params=pltpu.CompilerParams(dimension_semantics=("parallel",),
                                             vmem_limit_bytes=VMEM_LIMIT),
    )(at)


def _ssd_kernel(f_ref, fprev_ref, cw_ref, cb_ref, dtb_ref, vec_ref, ex_ref, y_ref, st_ref):
    c = pl.program_id(1)
    T = SSD_T
    G = D_GROUP
    N = SSD_STATE
    nb = SSD_NB
    R = nb * T
    seqs = range(nb)

    @pl.when(c == 0)
    def _():
        st_ref[...] = jnp.zeros_like(st_ref)

    f = f_ref[...].reshape(R, SD_W)
    xbc_raw = f[:, G:G + SSD_XBC]
    prev8 = [jnp.where(c == 0, 0.0, fprev_ref[b, :, G:G + SSD_XBC]) for b in seqs]
    row8 = _iota((8, 1), 0)
    conv = xbc_raw * cw_ref[3:4, :] + cb_ref[...]
    for j in range(1, 4):
        rolled = pltpu.roll(xbc_raw, j, axis=0)
        pieces = []
        for b in seqs:
            pieces.append(jnp.where(row8 < j, pltpu.roll(prev8[b], j, axis=0), rolled[b * T:b * T + 8]))
            pieces.append(rolled[b * T + 8:(b + 1) * T])
        conv = conv + jnp.concatenate(pieces, axis=0) * cw_ref[3 - j:4 - j, :]
    xbc = _silu(conv)
    xs = xbc[:, 0:G]
    bm = xbc[:, G:G + 2 * N]
    cm = xbc[:, G + 2 * N:G + 4 * N]
    z = f[:, 0:G]

    a_dense, d_dense, norm_w = (vec_ref[i:i + 1, :] for i in range(3))
    dt = _softplus(f[:, G + SSD_XBC:G + SSD_XBC + 128] + dtb_ref[...])
    dt_dense = _mmx_r(dt, ex_ref[...], pieces=2)
    da = dt_dense * a_dense
    xdt = xs * dt_dense
    ltri = (_iota((T, T), 1) <= _iota((T, T), 0))
    ltri_b = ltri.astype(BF16)
    umat = (_iota((T, T), 0) > _iota((T, T), 1)).astype(F32)
    masks = _head_masks()
    rows = [slice(b * T, (b + 1) * T) for b in seqs]

    cs = [_mmx_l(ltri_b, da[rows[b]], pieces=2) for b in seqs]
    scores = [[_mm(cm[rows[b], g * N:(g + 1) * N], bm[rows[b], g * N:(g + 1) * N], _NT)
               for g in range(2)] for b in seqs]
    y_diag = []
    for b in seqs:
        acc = None
        for h in range(N_HEADS):
            da_col = jnp.broadcast_to(da[rows[b], h * HEAD_DIM:h * HEAD_DIM + 1], (T, T))
            seg = _mmx_l(ltri_b, da_col * umat, pieces=2)
            dec = jnp.where(ltri, jnp.exp(jnp.where(ltri, seg, 0.0)), 0.0)
            term = _mm(scores[b][h // 2] * dec, xdt[rows[b]] * masks[h])
            acc = term if acc is None else acc + term
        y_diag.append(acc)

    ys = []
    for b in seqs:
        st = st_ref[b]
        cmb = cm[rows[b]]
        y_off = jnp.concatenate([_mm(cmb[:, 0:N], st[:, 0:N]), _mm(cmb[:, N:2 * N], st[:, N:2 * N])],
                                axis=1) * jnp.exp(cs[b])
        cs_last = cs[b][T - 1:T, :]
        xd = xdt[rows[b]] * jnp.exp(cs_last - cs[b])
        bmb = bm[rows[b]]
        st_ref[b] = st * jnp.exp(cs_last) + jnp.concatenate(
            [_mm(bmb[:, 0:N], xd[:, 0:N], _TN), _mm(bmb[:, N:2 * N], xd[:, N:2 * N], _TN)], axis=1)
        ys.append(y_diag[b] + y_off)

    y = (jnp.concatenate(ys, axis=0) + xs * d_dense) * _silu(z)
    halves = []
    for g in range(2):
        yg = y[:, g * N:(g + 1) * N]
        halves.append(yg * lax.rsqrt(jnp.mean(yg * yg, axis=-1, keepdims=True) + RMS_EPS))
    y_ref[...] = (jnp.concatenate(halves, axis=1) * norm_w).reshape(nb, T, G)


def _ssd(sd, cw, cb, dtb, vec, ex, batch, seq):
    nc = seq // SSD_T
    m = batch * seq
    const = lambda shape: pl.BlockSpec(shape, lambda b, c: (0, 0))
    sd3 = sd.reshape(batch, seq, SD_W)
    out = pl.pallas_call(
        _ssd_kernel,
        grid=(batch // SSD_NB, nc),
        in_specs=[pl.BlockSpec((SSD_NB, SSD_T, SD_W), lambda b, c: (b, c, 0)),
                  pl.BlockSpec((SSD_NB, 8, SD_W),
                               lambda b, c: (b, jnp.maximum(c * (SSD_T // 8) - 1, 0), 0)),
                  const((4, SSD_XBC)), const((1, SSD_XBC)), const((1, 128)), const((8, D_GROUP)),
                  const((128, D_GROUP))],
        out_specs=pl.BlockSpec((SSD_NB, SSD_T, D_GROUP), lambda b, c: (b, c, 0)),
        out_shape=jax.ShapeDtypeStruct((batch, seq, D_GROUP), F32),
        scratch_shapes=[pltpu.VMEM((SSD_NB, SSD_STATE, D_GROUP), F32)],
        compiler_params=pltpu.CompilerParams(dimension_semantics=("parallel", "arbitrary"),
                                             vmem_limit_bytes=VMEM_LIMIT),
    )(sd3, sd3, cw, cb, dtb, vec, ex)
    return out.reshape(m, D_GROUP)


def _hgrn_kernel(f_ref, vec_ref, y_ref, st_ref):
    c = pl.program_id(1)
    T = HG_T
    C = HG_C
    G = D_GROUP
    nblk = T // C

    @pl.when(c == 0)
    def _():
        st_ref[...] = jnp.zeros_like(st_ref)

    lb = vec_ref[0:1, :]
    norm_w = vec_ref[1:2, :]
    forget = lb + (1.0 - lb) * _sigmoid_tail(f_ref[:, G:2 * G])
    q = _silu(f_ref[:, 0:G])
    k = 1.0 - forget
    v = f_ref[:, 2 * G:3 * G]
    blk_tri = ((_iota((T, T), 0) // C == _iota((T, T), 1) // C)
               & (_iota((T, T), 1) <= _iota((T, T), 0))).astype(BF16)
    b = _mmx_l(blk_tri, jnp.log(forget), pieces=2)

    head_sel = [(_iota((1, G), 1) // HEAD_DIM) == h for h in range(N_HEADS)]
    bo = _same_head().astype(BF16)
    t_idx = _iota((C, 1), 0)
    blocks = lambda z: [z[i * C:(i + 1) * C] for i in range(nblk)]
    bq, qq, kq, vv = blocks(b), blocks(q), blocks(k), blocks(v)
    b_last = [bq[i][C - 1:C, :] for i in range(nblk)]

    upd = [_dot(_stack_heads(vv[i], head_sel), _stack_heads(kq[i] * jnp.exp(b_last[i] - bq[i]), head_sel),
                _TN) for i in range(nblk)]
    q_in = [qq[i] * jnp.exp(bq[i]) for i in range(nblk)]
    half = C // 2
    o_intra = []
    for i in range(nblk):
        parts = []
        for s in range(C):
            lo = 0 if s < half else half
            e = jnp.exp(bq[i][lo:C] - bq[i][s:s + 1, :])
            parts.append(jnp.where(t_idx[lo:C] >= s, qq[i][lo:C] * e * kq[i][s:s + 1, :], 0.0))
        z = _mm(jnp.concatenate(parts, axis=0), bo)
        o_lo = z[0:half] * vv[i][0:1, :]
        o_hi = z[half:C] * vv[i][0:1, :]
        for s in range(1, half):
            o_lo = o_lo + z[s * C:s * C + half] * vv[i][s:s + 1, :]
            o_hi = o_hi + z[s * C + half:(s + 1) * C] * vv[i][s:s + 1, :]
        for s in range(half, C):
            r0 = half * C + (s - half) * half
            o_hi = o_hi + z[r0:r0 + half] * vv[i][s:s + 1, :]
        o_intra.append(jnp.concatenate([o_lo, o_hi], axis=0))

    st = st_ref[...]
    outs = []
    for i in range(nblk):
        outs.append(_mm(q_in[i], st, _NT) + o_intra[i])
        st = st * jnp.exp(b_last[i]) + upd[i]
    st_ref[...] = st
    o = jnp.concatenate(outs, axis=0)
    ms = _mmx_r(o * o, bo, pieces=2) * (1.0 / HEAD_DIM)
    y_ref[...] = o * lax.rsqrt(ms + RMS_EPS) * norm_w * _silu(f_ref[:, 3 * G:4 * G])


def _hgrn(hg, vec, batch, seq):
    nc = seq // HG_T
    m = batch * seq
    return pl.pallas_call(
        _hgrn_kernel,
        grid=(batch, nc),
        in_specs=[pl.BlockSpec((HG_T, HG_W), lambda b, c: (b * nc + c, 0)),
                  pl.BlockSpec((8, D_GROUP), lambda b, c: (0, 0))],
        out_specs=pl.BlockSpec((HG_T, D_GROUP), lambda b, c: (b * nc + c, 0)),
        out_shape=jax.ShapeDtypeStruct((m, D_GROUP), F32),
        scratch_shapes=[pltpu.VMEM((D_GROUP, D_GROUP), F32)],
        compiler_params=pltpu.CompilerParams(dimension_semantics=("parallel", "arbitrary"),
                                             vmem_limit_bytes=VMEM_LIMIT),
    )(hg, vec)


def _post_kernel(x_ref, ya_ref, yb_ref, yc_ref, yd_ref, wo_ref, wu_ref, wd_ref, ln_ref, o_ref):
    x = x_ref[...]
    mix = jnp.zeros(x.shape, F32)
    for i, ref in enumerate((ya_ref, yb_ref, yc_ref, yd_ref)):
        mix = mix + jnp.dot(ref[...].astype(BF16), wo_ref[i * D_GROUP:(i + 1) * D_GROUP, :],
                            preferred_element_type=F32)
    x1 = _layer_norm(ALPHA * x + mix, ln_ref[0:1, :], ln_ref[1:2, :])
    xb = x1.astype(BF16)
    acc = ALPHA * x1
    for c0 in range(0, D_FF, FF_CHUNK):
        h = jnp.maximum(jnp.dot(xb, wu_ref[:, c0:c0 + FF_CHUNK], preferred_element_type=F32), 0.0)
        acc = acc + jnp.dot((h * h).astype(BF16), wd_ref[c0:c0 + FF_CHUNK, :],
                            preferred_element_type=F32)
    o_ref[...] = _layer_norm(acc, ln_ref[2:3, :], ln_ref[3:4, :])


def _post(x2d, ys, wo, wu, wd, ln):
    m = x2d.shape[0]
    row = lambda w: pl.BlockSpec((POST_TM, w), lambda i: (i, 0))
    res = lambda shape: pl.BlockSpec(shape, lambda i: (0, 0), pipeline_mode=pl.Buffered(1))
    return pl.pallas_call(
        _post_kernel,
        grid=(m // POST_TM,),
        in_specs=[row(D_MODEL)] + [row(D_GROUP)] * 4
        + [res((D_MODEL, D_MODEL)), res((D_MODEL, D_FF)), res((D_FF, D_MODEL)), res((8, D_MODEL))],
        out_specs=row(D_MODEL),
        out_shape=jax.ShapeDtypeStruct((m, D_MODEL), F32),
        compiler_params=pltpu.CompilerParams(dimension_semantics=("parallel",),
                                             vmem_limit_bytes=VMEM_LIMIT),
    )(x2d, *ys, wo, wu, wd, ln)


def _pad_cols(a, width):
    return jnp.pad(a, ((0, 0), (0, width - a.shape[1])))


def _pad_rows(a, top, total):
    return jnp.pad(a, ((top, total - top - a.shape[0]), (0, 0)))


def _rep_heads(v):
    return jnp.repeat(v.astype(F32), HEAD_DIM)[None, :]


def _rows8(rows, width):
    out = jnp.concatenate([r.reshape(1, width).astype(F32) for r in rows], axis=0)
    return jnp.pad(out, ((0, 8 - out.shape[0]), (0, 0)))


def kernel(x, lower_bounds, w_in, w_in_vres, mu_shift, mu_vres, rwkv_w0, rwkv_w2, rwkv_a0, rwkv_a2, rwkv_g2, rwkv_k_k, rwkv_k_a, rwkv_r_k, rwkv_lnx_w, rwkv_lnx_b, rwkv_v0, rwkv_v2, ssd_conv_w, ssd_conv_b, ssd_dt_bias, ssd_A_log, ssd_D, ssd_norm_w, hgrn_norm_w, w_out, ln1_w, ln1_b, w_up, w_down, ln2_w, ln2_b):
    batch, seq, _ = x.shape
    lb = jax.nn.softmax(lower_bounds.astype(F32), axis=0)
    lb = jnp.cumsum(lb, axis=0) - lb[0]
    expand = (jnp.arange(128)[:, None] == (jnp.arange(D_GROUP)[None, :] // HEAD_DIM)).astype(BF16)

    h = x.reshape(batch * seq, D_MODEL)
    v_first = None
    for l in range(DEPTH):
        mu = mu_shift[l][None, :]
        w_vres = None
        if l > 0:
            w_vres = _pad_cols(w_in_vres[l - 1], 128).astype(BF16)
            mu = jnp.concatenate([mu, mu_vres[l - 1][None, :]], axis=1)
        rw, at, sd, hg = _project(h, w_in, l, w_vres)

        vec_rows = [rwkv_w0[l], rwkv_a0[l], rwkv_k_k[l], rwkv_k_a[l], rwkv_lnx_w[l], rwkv_lnx_b[l],
                    rwkv_r_k[l].reshape(-1)]
        if l > 0:
            vec_rows.append(rwkv_v0[l - 1])
        y_a, v_first = _rwkv(
            rw, v_first, _pad_cols(mu, RW_W), _rows8(vec_rows, D_GROUP),
            _pad_rows(rwkv_w2[l], 0, 128), _pad_rows(rwkv_a2[l], 32, 128), _pad_rows(rwkv_g2[l], 64, 128),
            _pad_rows(rwkv_v2[l - 1], 0, 128) if l > 0 else None, batch, seq)
        y_b = _attention(at, batch, seq)
        ssd_vec = _rows8([_rep_heads(-jnp.exp(ssd_A_log[l].astype(F32))), _rep_heads(ssd_D[l]),
                          ssd_norm_w[l]], D_GROUP)
        y_c = _ssd(sd, ssd_conv_w[l].astype(F32), ssd_conv_b[l][None, :],
                   _pad_cols(ssd_dt_bias[l][None, :], 128), ssd_vec, expand, batch, seq)
        y_d = _hgrn(hg, _rows8([lb[l], hgrn_norm_w[l]], D_GROUP), batch, seq)
        ln = _rows8([ln1_w[l], ln1_b[l], ln2_w[l], ln2_b[l]], D_MODEL)
        h = _post(h, (y_a, y_b, y_c, y_d), w_out[l].astype(BF16), w_up[l].astype(BF16),
                  w_down[l].astype(BF16), ln)
    return h.reshape(batch, seq, D_MODEL)
```

```python
import functools
import math

import jax
import jax.numpy as jnp
from jax import lax
from jax.experimental import pallas as pl
from jax.experimental.pallas import tpu as pltpu

F32 = jnp.float32
BF16 = jnp.bfloat16

D_MODEL = 1024
DEPTH = 2
D_GROUP = 256
HEAD_DIM = 64
N_HEADS = 4
D_FF = 4 * D_MODEL
ALPHA = (2.0 * DEPTH) ** 0.25
LN_EPS = 1e-5
RMS_EPS = 1e-5
RWKV_GN_EPS = HEAD_DIM * 1e-5
RWKV_COLS = 896
ATTN_COLS = 768
SSD_STATE = 128
SSD_XBC = 768
SSD_COLS = 1028
HGRN_COLS = 1024
DILATIONS = (1, 4, 16)
ATTN_BLK = 128
ATTN_GROUP = 4
ALIBI_SLOPES = tuple(2.0 ** (-8.0 * (h + 1) / N_HEADS) for h in range(N_HEADS))

RW_W, AT_W, SD_W, HG_W = 1024, 768, 1152, 1024
PROJ_W = RW_W + AT_W + SD_W + HG_W

RW_T = 64
RW_NCH = 2
RW_NB = 4
RW_ROWS = RW_T * RW_NCH
SSD_T = 128
SSD_NB = 4
HG_C = 16
HG_T = 256
PROJ_TM = 512
POST_TM = 1024
FF_CHUNK = 512

VMEM_LIMIT = 56 * 1024 * 1024


def _dot(a, b, dims):
    return lax.dot_general(a, b, (dims, ((), ())), preferred_element_type=F32)


_NN = ((1,), (0,))
_NT = ((1,), (1,))
_TN = ((0,), (0,))


def _mm(a, b, dims=_NN):
    return _dot(a.astype(BF16), b.astype(BF16), dims)


def _split(x, pieces):
    out = []
    for i in range(pieces):
        hi = x.astype(BF16)
        out.append(hi)
        if i + 1 < pieces:
            x = x - hi.astype(F32)
    return out


def _mmx_l(m01, x, dims=_NN, pieces=3):
    return sum(_dot(m01, xp, dims) for xp in _split(x, pieces))


def _mmx_r(x, m01, dims=_NN, pieces=3):
    return sum(_dot(xp, m01, dims) for xp in _split(x, pieces))


def _mm3(a, b, dims=_NN):
    ah = a.astype(BF16)
    al = (a - ah.astype(F32)).astype(BF16)
    bh = b.astype(BF16)
    bl = (b - bh.astype(F32)).astype(BF16)
    return _dot(ah, bh, dims) + _dot(ah, bl, dims) + _dot(al, bh, dims)


def _iota(shape, dim):
    return lax.broadcasted_iota(jnp.int32, shape, dim)


def _head_masks(width=D_GROUP):
    lane = _iota((1, width), 1) // HEAD_DIM
    return [(lane == h).astype(F32) for h in range(N_HEADS)]


def _same_head(n=D_GROUP):
    return (_iota((n, n), 0) // HEAD_DIM) == (_iota((n, n), 1) // HEAD_DIM)


def _sigmoid(x):
    return 0.5 * jnp.tanh(0.5 * x) + 0.5


def _sigmoid_tail(x):
    return 1.0 / (1.0 + jnp.exp(-x))


def _silu(x):
    return x * _sigmoid(x)


def _softplus(x):
    return jnp.maximum(x, 0.0) + jnp.log(1.0 + jnp.exp(-jnp.abs(x)))


def _layer_norm(x, w, b):
    mu = jnp.mean(x, axis=-1, keepdims=True)
    d = x - mu
    var = jnp.mean(d * d, axis=-1, keepdims=True)
    return d * lax.rsqrt(var + LN_EPS) * w + b


def _shift_rows(x, prev_row):
    row = _iota((x.shape[0], 1), 0)
    return jnp.where(row == 0, prev_row, pltpu.roll(x, 1, axis=0))


def _proj_kernel(has_vres, *refs):
    if has_vres:
        x_ref, w_ref, wv_ref, rw_ref, at_ref, sd_ref, hg_ref = refs
    else:
        x_ref, w_ref, rw_ref, at_ref, sd_ref, hg_ref = refs
    xb = x_ref[...].astype(BF16)
    n_in = w_ref.shape[1]

    def emit(ref, c_out, c_in, width):
        wb = w_ref[:, c_in:c_in + width].astype(BF16)
        ref[:, c_out:c_out + width] = jnp.dot(xb, wb, preferred_element_type=F32)

    src = 0
    for ref, cols, padded in ((rw_ref, RWKV_COLS, RW_W), (at_ref, ATTN_COLS, AT_W),
                              (sd_ref, SSD_COLS, SD_W), (hg_ref, HGRN_COLS, HG_W)):
        for c0 in range(0, padded, 256):
            width = min(256, padded - c0)
            if c0 + width <= cols:
                emit(ref, c0, src + c0, width)
                continue
            real = max(cols - c0, 0)
            real_pad = -(-real // 128) * 128
            if real:
                take = min(real_pad, n_in - (src + c0))
                wb = w_ref[:, src + c0:src + c0 + take].astype(BF16)
                lane = _iota((1, take), 1)
                wb = jnp.where(lane < real, wb, jnp.zeros_like(wb))
                ref[:, c0:c0 + take] = jnp.dot(xb, wb, preferred_element_type=F32)
                if take < real_pad:
                    ref[:, c0 + take:c0 + real_pad] = jnp.zeros((xb.shape[0], real_pad - take), F32)
            rest = width - real_pad
            if rest:
                if has_vres and ref is rw_ref:
                    ref[:, c0 + real_pad:c0 + width] = jnp.dot(xb, wv_ref[...], preferred_element_type=F32)
                else:
                    ref[:, c0 + real_pad:c0 + width] = jnp.zeros((xb.shape[0], rest), F32)
        src += cols


def _project(x2d, w_in, layer, w_vres):
    m = x2d.shape[0]
    n_in = w_in.shape[2]
    has_vres = w_vres is not None
    outs = [jax.ShapeDtypeStruct((m, w), F32) for w in (RW_W, AT_W, SD_W, HG_W)]
    in_specs = [pl.BlockSpec((PROJ_TM, D_MODEL), lambda i: (i, 0)),
                pl.BlockSpec((None, D_MODEL, n_in), lambda i: (layer, 0, 0), pipeline_mode=pl.Buffered(1))]
    args = [x2d, w_in]
    if has_vres:
        in_specs.append(pl.BlockSpec((D_MODEL, 128), lambda i: (0, 0)))
        args.append(w_vres)
    return pl.pallas_call(
        functools.partial(_proj_kernel, has_vres),
        grid=(m // PROJ_TM,),
        in_specs=in_specs,
        out_specs=[pl.BlockSpec((PROJ_TM, w), lambda i: (i, 0)) for w in (RW_W, AT_W, SD_W, HG_W)],
        out_shape=outs,
        compiler_params=pltpu.CompilerParams(dimension_semantics=("parallel",),
                                             vmem_limit_bytes=VMEM_LIMIT),
    )(*args)


def _stack_heads(x, head_sel):
    xb = x.astype(BF16)
    zero = jnp.zeros_like(xb)
    return jnp.concatenate([jnp.where(m, xb, zero) for m in head_sel], axis=0)


def _rwkv_chunks(at, bh, kh, rt, v, head_sel):
    n = len(at)
    rng = range(n)
    T = at[0].shape[0]
    G = D_GROUP
    row = _iota((T, G), 0)
    lane_t = _iota((T, G), 1) % T
    strict = lane_t < row
    incl = lane_t <= row
    st = lambda x: _stack_heads(x, head_sel)
    lhs = [jnp.concatenate([at[j], rt[j]], axis=0).astype(BF16) for j in rng]
    ab = [_dot(lhs[j], st(bh[j]), _NT) for j in rng]
    ak = [_dot(lhs[j], st(kh[j]), _NT) for j in rng]
    n_u = [jnp.where(strict, ab[j][0:T], 0.0) for j in rng]
    a_ak = [jnp.where(strict, ak[j][0:T], 0.0) for j in rng]
    m_rb = [jnp.where(incl, ab[j][T:2 * T], 0.0).astype(BF16) for j in rng]
    m_rk = [jnp.where(incl, ak[j][T:2 * T], 0.0).astype(BF16) for j in rng]

    eye = jnp.where(lane_t == row, 1.0, 0.0)
    x_u = [eye + n_u[j] for j in rng]
    pw = [_dot(n_u[j].astype(BF16), st(n_u[j]), _NN) for j in rng]
    steps = T.bit_length() - 2
    for it in range(steps):
        w = [st(pw[j]) for j in rng]
        if it + 1 < steps:
            res = [_dot(jnp.concatenate([pw[j], x_u[j]], axis=0).astype(BF16), w[j], _NN) for j in rng]
            pw = [res[j][0:T] for j in rng]
            x_u = [x_u[j] + res[j][T:2 * T] for j in rng]
        else:
            x_u = [x_u[j] + _dot(x_u[j].astype(BF16), w[j], _NN) for j in rng]

    v_s = [st(v[j]) for j in rng]
    akv = [_dot(a_ak[j].astype(BF16), v_s[j], _NN) for j in rng]
    xb = [x_u[j].astype(BF16) for j in rng]
    p_u = [_dot(xb[j], st(at[j]), _NN) for j in rng]
    q_u = [_dot(xb[j], st(akv[j]), _NN) for j in rng]
    mkv = [_dot(m_rk[j], v_s[j], _NN) for j in rng]
    return p_u, q_u, m_rb, mkv


def _rwkv_kernel(has_vres, *refs):
    if has_vres:
        (f_ref, fprev_ref, vf_ref, mu_ref, vec_ref, w2_ref, a2_ref, g2_ref, v2_ref,
         y_ref, s_ref) = refs
    else:
        (f_ref, fprev_ref, mu_ref, vec_ref, w2_ref, a2_ref, g2_ref,
         y_ref, vout_ref, s_ref) = refs
    c = pl.program_id(1)
    T = RW_T
    R = RW_ROWS * RW_NB
    G = D_GROUP
    nchunks = RW_NCH * RW_NB

    @pl.when(c == 0)
    def _():
        s_ref[...] = jnp.zeros_like(s_ref)

    f = f_ref[...].reshape(R, RW_W)
    row = _iota((R, 1), 0)
    shifted = pltpu.roll(f, 1, axis=0)
    for b in range(RW_NB):
        prev_row = jnp.where(c == 0, 0.0, fprev_ref[b, 7:8, :])
        shifted = jnp.where(row == b * RW_ROWS, prev_row, shifted)
    x = f + (shifted - f) * mu_ref[...]
    r = x[:, 0:G]
    k = x[:, G:2 * G]
    v = x[:, 2 * G:3 * G]
    seg = x[:, 3 * G:3 * G + 128]
    w0, a0, k_k, k_a = (vec_ref[i:i + 1, :] for i in range(4))
    lnx_w, lnx_b, r_k = (vec_ref[i:i + 1, :] for i in range(4, 7))

    p = w0 + _mm3(jnp.tanh(seg), w2_ref[...])
    lw = -math.exp(-0.5) * _sigmoid(p)
    a = _sigmoid(a0 + _mm(seg, a2_ref[...]))
    g = _mm(_sigmoid(seg), g2_ref[...])
    if has_vres:
        seg2 = x[:, 3 * G + 128:3 * G + 256]
        v0 = vec_ref[7:8, :]
        v = v + (vf_ref[...].reshape(R, G) - v) * _sigmoid(v0 + _mm(seg2, v2_ref[...]))
    else:
        vout_ref[...] = v.reshape(RW_NB, RW_ROWS, G)

    head_sel = [(_iota((1, G), 1) // HEAD_DIM) == h for h in range(N_HEADS)]
    same = _same_head()
    bo = same.astype(BF16)
    kk = k * k_k
    kk = kk * lax.rsqrt(jnp.maximum(_mm(kk * kk, bo), 1e-24))
    k2 = k * (1.0 + (a - 1.0) * k_a)
    kka = kk * a

    ltri = (_iota((T, T), 1) <= _iota((T, T), 0)).astype(BF16)
    cs = jnp.concatenate([_mmx_l(ltri, lw[j * T:(j + 1) * T], pieces=2) for j in range(nchunks)], axis=0)
    c_last = jnp.concatenate(
        [jnp.broadcast_to(cs[(j + 1) * T - 1:(j + 1) * T, :], (T, G)) for j in range(nchunks)], axis=0)
    e_neg = jnp.exp(-cs)
    e_dec = jnp.exp(c_last - cs)
    at = -kk * jnp.exp(cs - lw)
    bh = kka * e_neg
    kh = k2 * e_neg
    rt = r * jnp.exp(cs)
    bdec = kka * e_dec
    kdec = k2 * e_dec

    chunks = lambda z: [z[j * T:(j + 1) * T] for j in range(nchunks)]
    rt_c, v_c, bdec_c, kdec_c = chunks(rt), chunks(v), chunks(bdec), chunks(kdec)
    p_u, q_u, m_rb, mkv = _rwkv_chunks(chunks(at), chunks(bh), chunks(kh), rt_c, v_c, head_sel)
    rp_lhs = [jnp.concatenate([rt_c[i], p_u[i]], axis=0).astype(BF16) for i in range(nchunks)]
    dec_t = [jnp.concatenate([bdec_c[i], kdec_c[i]], axis=0).T.astype(BF16) for i in range(nchunks)]
    w_col = [jnp.exp(jnp.broadcast_to(c_last[i * T:i * T + 1, :], (128, G))).T[:, 0:1]
             for i in range(nchunks)]
    s = [s_ref[b] for b in range(RW_NB)]
    outs = [None] * nchunks
    for j in range(RW_NCH):
        for b in range(RW_NB):
            i = b * RW_NCH + j
            rp = _dot(rp_lhs[i], s[b].astype(BF16), _NN)
            sa = rp[T:2 * T] + q_u[i]
            outs[i] = rp[0:T] + _dot(m_rb[i], _stack_heads(sa, head_sel), _NN) + mkv[i]
            upd = _dot(dec_t[i], jnp.concatenate([sa, v_c[i]], axis=0).astype(BF16), _NN)
            s[b] = s[b] * w_col[i] + jnp.where(same, upd, 0.0)
    for b in range(RW_NB):
        s_ref[b] = s[b]
    o = jnp.concatenate(outs, axis=0)

    mean = _mmx_r(o, bo, pieces=2) * (1.0 / HEAD_DIM)
    d = o - mean
    var = _mm(d * d, bo) * (1.0 / HEAD_DIM)
    yn = d * lax.rsqrt(var + RWKV_GN_EPS) * lnx_w + lnx_b
    bonus = _mm(r * k2 * r_k, bo) * v
    y_ref[...] = ((yn + bonus) * g).reshape(RW_NB, RW_ROWS, G)


def _rwkv(rw, v_first, mu, vec, w2p, a2p, g2p, v2p, batch, seq):
    has_vres = v_first is not None
    nc = seq // RW_ROWS
    m = batch * seq
    row_spec = lambda w: pl.BlockSpec((RW_NB, RW_ROWS, w), lambda b, c: (b, c, 0))
    const = lambda shape: pl.BlockSpec(shape, lambda b, c: (0, 0))
    prev_spec = pl.BlockSpec(
        (RW_NB, 8, RW_W), lambda b, c: (b, jnp.maximum(c * (RW_ROWS // 8) - 1, 0), 0))
    rw3 = rw.reshape(batch, seq, RW_W)
    in_specs = [row_spec(RW_W), prev_spec]
    args = [rw3, rw3]
    if has_vres:
        in_specs.append(row_spec(D_GROUP))
        args.append(v_first.reshape(batch, seq, D_GROUP))
    in_specs += [const((1, RW_W)), const((8, D_GROUP)), const((128, D_GROUP)),
                 const((128, D_GROUP)), const((128, D_GROUP))]
    args += [mu, vec, w2p, a2p, g2p]
    if has_vres:
        in_specs.append(const((128, D_GROUP)))
        args.append(v2p)
    out_shape = [jax.ShapeDtypeStruct((batch, seq, D_GROUP), F32)]
    out_specs = [row_spec(D_GROUP)]
    if not has_vres:
        out_shape.append(jax.ShapeDtypeStruct((batch, seq, D_GROUP), F32))
        out_specs.append(row_spec(D_GROUP))
    outs = pl.pallas_call(
        functools.partial(_rwkv_kernel, has_vres),
        grid=(batch // RW_NB, nc),
        in_specs=in_specs,
        out_specs=out_specs,
        out_shape=out_shape,
        scratch_shapes=[pltpu.VMEM((RW_NB, D_GROUP, D_GROUP), F32)],
        compiler_params=pltpu.CompilerParams(dimension_semantics=("parallel", "arbitrary"),
                                             vmem_limit_bytes=VMEM_LIMIT),
    )(*args)
    y = outs[0].reshape(m, D_GROUP)
    if has_vres:
        return y, v_first
    return y, outs[1].reshape(m, D_GROUP)


def _attn_kernel(qkv_ref, o_ref, perm_ref, acc_ref, l_ref, m_ref, bias_ref):
    G = D_GROUP
    blk = ATTN_BLK
    seq = qkv_ref.shape[0]
    nres = DILATIONS[-1]
    sub = seq // nres
    head_sel = [(_iota((1, G), 1) // HEAD_DIM) == h for h in range(N_HEADS)]
    scale = HEAD_DIM ** -0.5 * math.log2(math.e)

    def jmap(a, d):
        run = blk * d // nres
        return (nres // d) * (a % run) + a // run

    to_perm = (_iota((blk, blk), 1) == jmap(_iota((blk, blk), 0), 1)).astype(BF16)
    from_perm = (_iota((blk, blk), 0) == jmap(_iota((blk, blk), 1), 1)).astype(BF16)
    run1 = blk // nres
    col_scale = jnp.where(_iota((1, AT_W), 1) < G, scale, 1.0)

    def permute(n, carry):
        r0 = pl.multiple_of(n * blk, blk)
        pb = jnp.dot(to_perm, (qkv_ref[pl.ds(r0, blk), :] * col_scale).astype(BF16),
                     preferred_element_type=F32)
        for e in range(nres):
            dst = pl.multiple_of(e * sub + n * run1, run1)
            perm_ref[pl.ds(dst, run1), :] = pb[e * run1:(e + 1) * run1, :]
        return carry

    lax.fori_loop(0, seq // blk, permute, 0, unroll=4)

    branches = tuple(reversed(DILATIONS))
    for bi, d in enumerate(branches):
        jq = jmap(_iota((blk, 2 * blk), 0), d)
        kb = _iota((blk, 2 * blk), 1)
        dist = blk + jq - (jmap(kb % blk, d) + (kb // blk) * blk)
        in_band = (dist >= 0) & (dist <= blk)
        for h in range(N_HEADS):
            bias_ref[bi * N_HEADS + h] = jnp.where(
                in_band, dist.astype(F32) * (-ALIBI_SLOPES[h] * d * math.log2(math.e)), -jnp.inf)

    for bi, d in enumerate(branches):
        runs = nres // d
        run = blk // runs
        nb = seq // (d * blk)
        first = bi == 0
        last = bi == len(branches) - 1

        def blocks(items, with_prev, bi=bi, d=d, runs=runs, run=run, first=first, last=last):
            its = range(len(items))
            heads = range(N_HEADS)

            def starts(rho, nn):
                return [pl.multiple_of((rho + d * e) * sub + nn * run, run) for e in range(runs)]

            def gather(st, c0):
                return jnp.concatenate([perm_ref[pl.ds(s0, run), c0:c0 + G] for s0 in st],
                                       axis=0).astype(BF16)

            cur = [starts(rho, n) for rho, n in items]
            qb = [gather(cur[i], 0) for i in its]
            if with_prev:
                prv = [starts(rho, n - 1) for rho, n in items]
                kcat = [jnp.concatenate([gather(prv[i], G), gather(cur[i], G)], axis=0) for i in its]
                vcat = [jnp.concatenate([gather(prv[i], 2 * G), gather(cur[i], 2 * G)], axis=0)
                        for i in its]
            else:
                kcat = [gather(cur[i], G) for i in its]
                vcat = [gather(cur[i], 2 * G) for i in its]
            zq = jnp.zeros_like(qb[0])
            kc0 = 0 if with_prev else blk
            pairs = [(0, 1), (2, 3)]
            s_p = [[_dot(jnp.concatenate([jnp.where(head_sel[h], qb[i], zq) for h in pr], axis=0),
                         kcat[i], _NT) for pr in pairs] for i in its]
            s = [[s_p[i][h // 2][(h % 2) * blk:(h % 2 + 1) * blk]
                  + bias_ref[bi * N_HEADS + h, :, kc0:2 * blk] for h in heads] for i in its]
            mh = [[jnp.max(s[i][h], axis=-1, keepdims=True) for h in heads] for i in its]
            ph = [[jnp.exp2(s[i][h] - mh[i][h]) for h in heads] for i in its]
            lh = [[jnp.sum(ph[i][h], axis=-1, keepdims=True) for h in heads] for i in its]
            pv_p = [[_dot(jnp.concatenate([ph[i][h].astype(BF16) for h in pr], axis=0), vcat[i], _NN)
                     for pr in pairs] for i in its]
            def by_head(parts):
                out = jnp.broadcast_to(parts[N_HEADS - 1], (blk, G))
                for h in reversed(range(N_HEADS - 1)):
                    out = jnp.where(head_sel[h], parts[h], out)
                return out

            acc_b = [by_head([pv_p[i][h // 2][(h % 2) * blk:(h % 2 + 1) * blk] for h in heads])
                     for i in its]
            l_b = [by_head(lh[i]) for i in its]
            m_b = [by_head(mh[i]) for i in its]
            for i in its:
                outs = []
                for e, s0 in enumerate(cur[i]):
                    idx = pl.ds(s0, run)
                    sl = slice(e * run, (e + 1) * run)
                    if first:
                        acc_ref[idx, :] = acc_b[i][sl]
                        l_ref[idx, :] = l_b[i][sl]
                        m_ref[idx, :] = m_b[i][sl]
                    else:
                        m_old = m_ref[idx, :]
                        m_new = jnp.maximum(m_old, m_b[i][sl])
                        w_old = jnp.exp2(m_old - m_new)
                        w_new = jnp.exp2(m_b[i][sl] - m_new)
                        acc_n = acc_ref[idx, :] * w_old + acc_b[i][sl] * w_new
                        l_n = l_ref[idx, :] * w_old + l_b[i][sl] * w_new
                        if last:
                            outs.append(acc_n / l_n)
                        else:
                            acc_ref[idx, :] = acc_n
                            l_ref[idx, :] = l_n
                            m_ref[idx, :] = m_new
                if last:
                    r0 = pl.multiple_of(items[i][1] * blk, blk)
                    o_ref[pl.ds(r0, blk), :] = _mmx_l(from_perm, jnp.concatenate(outs, axis=0))

        def run_blocks(count, index, with_prev, blocks=blocks):
            grp = ATTN_GROUP - 1 if with_prev else ATTN_GROUP

            def group(it, carry):
                blocks([index(grp * it + u) for u in range(grp)], with_prev)
                return carry

            if count >= grp:
                lax.fori_loop(0, count // grp, group, 0)
            rest = count % grp
            if rest:
                blocks([index(count - rest + u) for u in range(rest)], with_prev)

        run_blocks(d, lambda i: (i, 0), False)
        if nb > 1:
            run_blocks(d * (nb - 1), lambda i, nb=nb: (i // (nb - 1), i % (nb - 1) + 1), True)


def _attention(at, batch, seq):
    m = batch * seq
    return pl.pallas_call(
        _attn_kernel,
        grid=(batch,),
        in_specs=[pl.BlockSpec((seq, AT_W), lambda b: (b, 0))],
        out_specs=pl.BlockSpec((seq, D_GROUP), lambda b: (b, 0)),
        out_shape=jax.ShapeDtypeStruct((m, D_GROUP), F32),
        scratch_shapes=[pltpu.VMEM((seq, AT_W), F32)] + [pltpu.VMEM((seq, D_GROUP), F32)] * 3
        + [pltpu.VMEM((len(DILATIONS) * N_HEADS, ATTN_BLK, 2 * ATTN_BLK), F32)],
        compiler_params=pltpu.CompilerParams(dimension_semantics=("parallel",),
                                             vmem_limit_bytes=VMEM_LIMIT),
    )(at)


def _ssd_kernel(f_ref, fprev_ref, cw_ref, cb_ref, dtb_ref, vec_ref, ex_ref, y_ref, st_ref):
    c = pl.program_id(1)
    T = SSD_T
    G = D_GROUP
    N = SSD_STATE
    nb = SSD_NB
    R = nb * T
    seqs = range(nb)

    @pl.when(c == 0)
    def _():
        st_ref[...] = jnp.zeros_like(st_ref)

    f = f_ref[...].reshape(R, SD_W)
    xbc_raw = f[:, G:G + SSD_XBC]
    prev8 = [jnp.where(c == 0, 0.0, fprev_ref[b, :, G:G + SSD_XBC]) for b in seqs]
    row8 = _iota((8, 1), 0)
    conv = xbc_raw * cw_ref[3:4, :] + cb_ref[...]
    for j in range(1, 4):
        rolled = pltpu.roll(xbc_raw, j, axis=0)
        pieces = []
        for b in seqs:
            pieces.append(jnp.where(row8 < j, pltpu.roll(prev8[b], j, axis=0), rolled[b * T:b * T + 8]))
            pieces.append(rolled[b * T + 8:(b + 1) * T])
        conv = conv + jnp.concatenate(pieces, axis=0) * cw_ref[3 - j:4 - j, :]
    xbc = _silu(conv)
    xs = xbc[:, 0:G]
    bm = xbc[:, G:G + 2 * N]
    cm = xbc[:, G + 2 * N:G + 4 * N]
    z = f[:, 0:G]

    a_dense, d_dense, norm_w = (vec_ref[i:i + 1, :] for i in range(3))
    dt = _softplus(f[:, G + SSD_XBC:G + SSD_XBC + 128] + dtb_ref[...])
    dt_dense = _mmx_r(dt, ex_ref[...], pieces=2)
    da = dt_dense * a_dense
    xdt = xs * dt_dense
    ltri = (_iota((T, T), 1) <= _iota((T, T), 0))
    ltri_b = ltri.astype(BF16)
    umat = (_iota((T, T), 0) > _iota((T, T), 1)).astype(F32)
    masks = _head_masks()
    rows = [slice(b * T, (b + 1) * T) for b in seqs]

    cs = [_mmx_l(ltri_b, da[rows[b]], pieces=2) for b in seqs]
    scores = [[_mm(cm[rows[b], g * N:(g + 1) * N], bm[rows[b], g * N:(g + 1) * N], _NT)
               for g in range(2)] for b in seqs]
    y_diag = []
    for b in seqs:
        acc = None
        for h in range(N_HEADS):
            da_col = jnp.broadcast_to(da[rows[b], h * HEAD_DIM:h * HEAD_DIM + 1], (T, T))
            seg = _mmx_l(ltri_b, da_col * umat, pieces=2)
            dec = jnp.where(ltri, jnp.exp(jnp.where(ltri, seg, 0.0)), 0.0)
            term = _mm(scores[b][h // 2] * dec, xdt[rows[b]] * masks[h])
            acc = term if acc is None else acc + term
        y_diag.append(acc)

    ys = []
    for b in seqs:
        st = st_ref[b]
        cmb = cm[rows[b]]
        y_off = jnp.concatenate([_mm(cmb[:, 0:N], st[:, 0:N]), _mm(cmb[:, N:2 * N], st[:, N:2 * N])],
                                axis=1) * jnp.exp(cs[b])
        cs_last = cs[b][T - 1:T, :]
        xd = xdt[rows[b]] * jnp.exp(cs_last - cs[b])
        bmb = bm[rows[b]]
        st_ref[b] = st * jnp.exp(cs_last) + jnp.concatenate(
            [_mm(bmb[:, 0:N], xd[:, 0:N], _TN), _mm(bmb[:, N:2 * N], xd[:, N:2 * N], _TN)], axis=1)
        ys.append(y_diag[b] + y_off)

    y = (jnp.concatenate(ys, axis=0) + xs * d_dense) * _silu(z)
    halves = []
    for g in range(2):
        yg = y[:, g * N:(g + 1) * N]
        halves.append(yg * lax.rsqrt(jnp.mean(yg * yg, axis=-1, keepdims=True) + RMS_EPS))
    y_ref[...] = (jnp.concatenate(halves, axis=1) * norm_w).reshape(nb, T, G)


def _ssd(sd, cw, cb, dtb, vec, ex, batch, seq):
    nc = seq // SSD_T
    m = batch * seq
    const = lambda shape: pl.BlockSpec(shape, lambda b, c: (0, 0))
    sd3 = sd.reshape(batch, seq, SD_W)
    out = pl.pallas_call(
        _ssd_kernel,
        grid=(batch // SSD_NB, nc),
        in_specs=[pl.BlockSpec((SSD_NB, SSD_T, SD_W), lambda b, c: (b, c, 0)),
                  pl.BlockSpec((SSD_NB, 8, SD_W),
                               lambda b, c: (b, jnp.maximum(c * (SSD_T // 8) - 1, 0), 0)),
                  const((4, SSD_XBC)), const((1, SSD_XBC)), const((1, 128)), const((8, D_GROUP)),
                  const((128, D_GROUP))],
        out_specs=pl.BlockSpec((SSD_NB, SSD_T, D_GROUP), lambda b, c: (b, c, 0)),
        out_shape=jax.ShapeDtypeStruct((batch, seq, D_GROUP), F32),
        scratch_shapes=[pltpu.VMEM((SSD_NB, SSD_STATE, D_GROUP), F32)],
        compiler_params=pltpu.CompilerParams(dimension_semantics=("parallel", "arbitrary"),
                                             vmem_limit_bytes=VMEM_LIMIT),
    )(sd3, sd3, cw, cb, dtb, vec, ex)
    return out.reshape(m, D_GROUP)


def _hgrn_kernel(f_ref, vec_ref, y_ref, st_ref):
    c = pl.program_id(1)
    T = HG_T
    C = HG_C
    G = D_GROUP
    nblk = T // C

    @pl.when(c == 0)
    def _():
        st_ref[...] = jnp.zeros_like(st_ref)

    lb = vec_ref[0:1, :]
    norm_w = vec_ref[1:2, :]
    forget = lb + (1.0 - lb) * _sigmoid_tail(f_ref[:, G:2 * G])
    q = _silu(f_ref[:, 0:G])
    k = 1.0 - forget
    v = f_ref[:, 2 * G:3 * G]
    blk_tri = ((_iota((T, T), 0) // C == _iota((T, T), 1) // C)
               & (_iota((T, T), 1) <= _iota((T, T), 0))).astype(BF16)
    b = _mmx_l(blk_tri, jnp.log(forget), pieces=2)
    b2 = b * math.log2(math.e)

    head_sel = [(_iota((1, G), 1) // HEAD_DIM) == h for h in range(N_HEADS)]
    bo = _same_head().astype(BF16)
    t_idx = _iota((C, 1), 0)
    blocks = lambda z: [z[i * C:(i + 1) * C] for i in range(nblk)]
    bq, qq, kq, vv, b2q = blocks(b), blocks(q), blocks(k), blocks(v), blocks(b2)
    b_last = [bq[i][C - 1:C, :] for i in range(nblk)]

    upd = [_dot(_stack_heads(vv[i], head_sel), _stack_heads(kq[i] * jnp.exp(b_last[i] - bq[i]), head_sel),
                _TN) for i in range(nblk)]
    q_in = [qq[i] * jnp.exp(bq[i]) for i in range(nblk)]
    half = C // 2
    o_intra = []
    for i in range(nblk):
        parts = []
        for s in range(C):
            lo = 0 if s < half else half
            e = jnp.exp2(b2q[i][lo:C] - b2q[i][s:s + 1, :])
            parts.append(jnp.where(t_idx[lo:C] >= s, qq[i][lo:C] * e * kq[i][s:s + 1, :], 0.0))
        z = _mm(jnp.concatenate(parts, axis=0), bo)
        o_lo = z[0:half] * vv[i][0:1, :]
        o_hi = z[half:C] * vv[i][0:1, :]
        for s in range(1, half):
            o_lo = o_lo + z[s * C:s * C + half] * vv[i][s:s + 1, :]
            o_hi = o_hi + z[s * C + half:(s + 1) * C] * vv[i][s:s + 1, :]
        for s in range(half, C):
            r0 = half * C + (s - half) * half
            o_hi = o_hi + z[r0:r0 + half] * vv[i][s:s + 1, :]
        o_intra.append(jnp.concatenate([o_lo, o_hi], axis=0))

    st = st_ref[...]
    outs = []
    for i in range(nblk):
        outs.append(_mm(q_in[i], st, _NT) + o_intra[i])
        st = st * jnp.exp(b_last[i]) + upd[i]
    st_ref[...] = st
    o = jnp.concatenate(outs, axis=0)
    ms = _mmx_r(o * o, bo, pieces=2) * (1.0 / HEAD_DIM)
    y_ref[...] = o * lax.rsqrt(ms + RMS_EPS) * norm_w * _silu(f_ref[:, 3 * G:4 * G])


def _hgrn(hg, vec, batch, seq):
    nc = seq // HG_T
    m = batch * seq
    return pl.pallas_call(
        _hgrn_kernel,
        grid=(batch, nc),
        in_specs=[pl.BlockSpec((HG_T, HG_W), lambda b, c: (b * nc + c, 0)),
                  pl.BlockSpec((8, D_GROUP), lambda b, c: (0, 0))],
        out_specs=pl.BlockSpec((HG_T, D_GROUP), lambda b, c: (b * nc + c, 0)),
        out_shape=jax.ShapeDtypeStruct((m, D_GROUP), F32),
        scratch_shapes=[pltpu.VMEM((D_GROUP, D_GROUP), F32)],
        compiler_params=pltpu.CompilerParams(dimension_semantics=("parallel", "arbitrary"),
                                             vmem_limit_bytes=VMEM_LIMIT),
    )(hg, vec)


def _post_kernel(x_ref, ya_ref, yb_ref, yc_ref, yd_ref, wo_ref, wu_ref, wd_ref, ln_ref, o_ref):
    x = x_ref[...]
    mix = jnp.zeros(x.shape, F32)
    for i, ref in enumerate((ya_ref, yb_ref, yc_ref, yd_ref)):
        mix = mix + jnp.dot(ref[...].astype(BF16), wo_ref[i * D_GROUP:(i + 1) * D_GROUP, :],
                            preferred_element_type=F32)
    x1 = _layer_norm(ALPHA * x + mix, ln_ref[0:1, :], ln_ref[1:2, :])
    xb = x1.astype(BF16)
    acc = ALPHA * x1
    for c0 in range(0, D_FF, FF_CHUNK):
        h = jnp.maximum(jnp.dot(xb, wu_ref[:, c0:c0 + FF_CHUNK], preferred_element_type=F32), 0.0)
        acc = acc + jnp.dot((h * h).astype(BF16), wd_ref[c0:c0 + FF_CHUNK, :],
                            preferred_element_type=F32)
    o_ref[...] = _layer_norm(acc, ln_ref[2:3, :], ln_ref[3:4, :])


def _post(x2d, ys, wo, wu, wd, ln):
    m = x2d.shape[0]
    row = lambda w: pl.BlockSpec((POST_TM, w), lambda i: (i, 0))
    res = lambda shape: pl.BlockSpec(shape, lambda i: (0, 0), pipeline_mode=pl.Buffered(1))
    return pl.pallas_call(
        _post_kernel,
        grid=(m // POST_TM,),
        in_specs=[row(D_MODEL)] + [row(D_GROUP)] * 4
        + [res((D_MODEL, D_MODEL)), res((D_MODEL, D_FF)), res((D_FF, D_MODEL)), res((8, D_MODEL))],
        out_specs=row(D_MODEL),
        out_shape=jax.ShapeDtypeStruct((m, D_MODEL), F32),
        compiler_params=pltpu.CompilerParams(dimension_semantics=("parallel",),
                                             vmem_limit_bytes=VMEM_LIMIT),
    )(x2d, *ys, wo, wu, wd, ln)


def _pad_cols(a, width):
    return jnp.pad(a, ((0, 0), (0, width - a.shape[1])))


def _pad_rows(a, top, total):
    return jnp.pad(a, ((top, total - top - a.shape[0]), (0, 0)))


def _rep_heads(v):
    return jnp.repeat(v.astype(F32), HEAD_DIM)[None, :]


def _rows8(rows, width):
    out = jnp.concatenate([r.reshape(1, width).astype(F32) for r in rows], axis=0)
    return jnp.pad(out, ((0, 8 - out.shape[0]), (0, 0)))


def kernel(x, lower_bounds, w_in, w_in_vres, mu_shift, mu_vres, rwkv_w0, rwkv_w2, rwkv_a0, rwkv_a2, rwkv_g2, rwkv_k_k, rwkv_k_a, rwkv_r_k, rwkv_lnx_w, rwkv_lnx_b, rwkv_v0, rwkv_v2, ssd_conv_w, ssd_conv_b, ssd_dt_bias, ssd_A_log, ssd_D, ssd_norm_w, hgrn_norm_w, w_out, ln1_w, ln1_b, w_up, w_down, ln2_w, ln2_b):
    batch, seq, _ = x.shape
    lb = jax.nn.softmax(lower_bounds.astype(F32), axis=0)
    lb = jnp.cumsum(lb, axis=0) - lb[0]
    expand = (jnp.arange(128)[:, None] == (jnp.arange(D_GROUP)[None, :] // HEAD_DIM)).astype(BF16)

    h = x.reshape(batch * seq, D_MODEL)
    v_first = None
    for l in range(DEPTH):
        mu = mu_shift[l][None, :]
        w_vres = None
        if l > 0:
            w_vres = _pad_cols(w_in_vres[l - 1], 128).astype(BF16)
            mu = jnp.concatenate([mu, mu_vres[l - 1][None, :]], axis=1)
        rw, at, sd, hg = _project(h, w_in, l, w_vres)

        vec_rows = [rwkv_w0[l], rwkv_a0[l], rwkv_k_k[l], rwkv_k_a[l], rwkv_lnx_w[l], rwkv_lnx_b[l],
                    rwkv_r_k[l].reshape(-1)]
        if l > 0:
            vec_rows.append(rwkv_v0[l - 1])
        y_a, v_first = _rwkv(
            rw, v_first, _pad_cols(mu, RW_W), _rows8(vec_rows, D_GROUP),
            _pad_rows(rwkv_w2[l], 0, 128), _pad_rows(rwkv_a2[l], 32, 128), _pad_rows(rwkv_g2[l], 64, 128),
            _pad_rows(rwkv_v2[l - 1], 0, 128) if l > 0 else None, batch, seq)
        y_b = _attention(at, batch, seq)
        ssd_vec = _rows8([_rep_heads(-jnp.exp(ssd_A_log[l].astype(F32))), _rep_heads(ssd_D[l]),
                          ssd_norm_w[l]], D_GROUP)
        y_c = _ssd(sd, ssd_conv_w[l].astype(F32), ssd_conv_b[l][None, :],
                   _pad_cols(ssd_dt_bias[l][None, :], 128), ssd_vec, expand, batch, seq)
        y_d = _hgrn(hg, _rows8([lb[l], hgrn_norm_w[l]], D_GROUP), batch, seq)
        ln = _rows8([ln1_w[l], ln1_b[l], ln2_w[l], ln2_b[l]], D_MODEL)
        h = _post(h, (y_a, y_b, y_c, y_d), w_out[l].astype(BF16), w_up[l].astype(BF16),
                  w_down[l].astype(BF16), ln)
    return h.reshape(batch, seq, D_MODEL)
```

```python
import functools
import math

import jax
import jax.numpy as jnp
from jax import lax
from jax.experimental import pallas as pl
from jax.experimental.pallas import tpu as pltpu

F32 = jnp.float32
BF16 = jnp.bfloat16

D_MODEL = 1024
DEPTH = 2
D_GROUP = 256
HEAD_DIM = 64
N_HEADS = 4
D_FF = 4 * D_MODEL
ALPHA = (2.0 * DEPTH) ** 0.25
LN_EPS = 1e-5
RMS_EPS = 1e-5
RWKV_GN_EPS = HEAD_DIM * 1e-5
RWKV_COLS = 896
ATTN_COLS = 768
SSD_STATE = 128
SSD_XBC = 768
SSD_COLS = 1028
HGRN_COLS = 1024
DILATIONS = (1, 4, 16)
ATTN_BLK = 128
ATTN_GROUP = 4
ALIBI_SLOPES = tuple(2.0 ** (-8.0 * (h + 1) / N_HEADS) for h in range(N_HEADS))

RW_W, AT_W, SD_W, HG_W = 1024, 768, 1152, 1024
PROJ_W = RW_W + AT_W + SD_W + HG_W

RW_T = 64
RW_NCH = 2
RW_NB = 8
RW_ROWS = RW_T * RW_NCH
SSD_T = 128
SSD_NB = 4
HG_C = 16
HG_T = 256
PROJ_TM = 512
POST_TM = 1024
FF_CHUNK = 512

VMEM_LIMIT = 56 * 1024 * 1024


def _dot(a, b, dims):
    return lax.dot_general(a, b, (dims, ((), ())), preferred_element_type=F32)


_NN = ((1,), (0,))
_NT = ((1,), (1,))
_TN = ((0,), (0,))


def _mm(a, b, dims=_NN):
    return _dot(a.astype(BF16), b.astype(BF16), dims)


def _split(x, pieces):
    out = []
    for i in range(pieces):
        hi = x.astype(BF16)
        out.append(hi)
        if i + 1 < pieces:
            x = x - hi.astype(F32)
    return out


def _mmx_l(m01, x, dims=_NN, pieces=3):
    return sum(_dot(m01, xp, dims) for xp in _split(x, pieces))


def _mmx_r(x, m01, dims=_NN, pieces=3):
    return sum(_dot(xp, m01, dims) for xp in _split(x, pieces))


def _mm3(a, b, dims=_NN):
    ah = a.astype(BF16)
    al = (a - ah.astype(F32)).astype(BF16)
    bh = b.astype(BF16)
    bl = (b - bh.astype(F32)).astype(BF16)
    return _dot(ah, bh, dims) + _dot(ah, bl, dims) + _dot(al, bh, dims)


def _iota(shape, dim):
    return lax.broadcasted_iota(jnp.int32, shape, dim)


def _head_masks(width=D_GROUP):
    lane = _iota((1, width), 1) // HEAD_DIM
    return [(lane == h).astype(F32) for h in range(N_HEADS)]


def _same_head(n=D_GROUP):
    return (_iota((n, n), 0) // HEAD_DIM) == (_iota((n, n), 1) // HEAD_DIM)


def _sigmoid(x):
    return 0.5 * jnp.tanh(0.5 * x) + 0.5


def _sigmoid_tail(x):
    return 1.0 / (1.0 + jnp.exp(-x))


def _silu(x):
    return x * _sigmoid(x)


def _softplus(x):
    return jnp.maximum(x, 0.0) + jnp.log(1.0 + jnp.exp(-jnp.abs(x)))


def _layer_norm(x, w, b):
    mu = jnp.mean(x, axis=-1, keepdims=True)
    d = x - mu
    var = jnp.mean(d * d, axis=-1, keepdims=True)
    return d * lax.rsqrt(var + LN_EPS) * w + b


def _shift_rows(x, prev_row):
    row = _iota((x.shape[0], 1), 0)
    return jnp.where(row == 0, prev_row, pltpu.roll(x, 1, axis=0))


def _proj_kernel(has_vres, *refs):
    if has_vres:
        x_ref, w_ref, wv_ref, rw_ref, at_ref, sd_ref, hg_ref = refs
    else:
        x_ref, w_ref, rw_ref, at_ref, sd_ref, hg_ref = refs
    xb = x_ref[...].astype(BF16)
    n_in = w_ref.shape[1]

    def emit(ref, c_out, c_in, width):
        wb = w_ref[:, c_in:c_in + width].astype(BF16)
        ref[:, c_out:c_out + width] = jnp.dot(xb, wb, preferred_element_type=F32)

    src = 0
    for ref, cols, padded in ((rw_ref, RWKV_COLS, RW_W), (at_ref, ATTN_COLS, AT_W),
                              (sd_ref, SSD_COLS, SD_W), (hg_ref, HGRN_COLS, HG_W)):
        for c0 in range(0, padded, 256):
            width = min(256, padded - c0)
            if c0 + width <= cols:
                emit(ref, c0, src + c0, width)
                continue
            real = max(cols - c0, 0)
            real_pad = -(-real // 128) * 128
            if real:
                take = min(real_pad, n_in - (src + c0))
                wb = w_ref[:, src + c0:src + c0 + take].astype(BF16)
                lane = _iota((1, take), 1)
                wb = jnp.where(lane < real, wb, jnp.zeros_like(wb))
                ref[:, c0:c0 + take] = jnp.dot(xb, wb, preferred_element_type=F32)
                if take < real_pad:
                    ref[:, c0 + take:c0 + real_pad] = jnp.zeros((xb.shape[0], real_pad - take), F32)
            rest = width - real_pad
            if rest:
                if has_vres and ref is rw_ref:
                    ref[:, c0 + real_pad:c0 + width] = jnp.dot(xb, wv_ref[...], preferred_element_type=F32)
                else:
                    ref[:, c0 + real_pad:c0 + width] = jnp.zeros((xb.shape[0], rest), F32)
        src += cols


def _project(x2d, w_in, layer, w_vres):
    m = x2d.shape[0]
    n_in = w_in.shape[2]
    has_vres = w_vres is not None
    outs = [jax.ShapeDtypeStruct((m, w), F32) for w in (RW_W, AT_W, SD_W, HG_W)]
    in_specs = [pl.BlockSpec((PROJ_TM, D_MODEL), lambda i: (i, 0)),
                pl.BlockSpec((None, D_MODEL, n_in), lambda i: (layer, 0, 0), pipeline_mode=pl.Buffered(1))]
    args = [x2d, w_in]
    if has_vres:
        in_specs.append(pl.BlockSpec((D_MODEL, 128), lambda i: (0, 0)))
        args.append(w_vres)
    return pl.pallas_call(
        functools.partial(_proj_kernel, has_vres),
        grid=(m // PROJ_TM,),
        in_specs=in_specs,
        out_specs=[pl.BlockSpec((PROJ_TM, w), lambda i: (i, 0)) for w in (RW_W, AT_W, SD_W, HG_W)],
        out_shape=outs,
        compiler_params=pltpu.CompilerParams(dimension_semantics=("parallel",),
                                             vmem_limit_bytes=VMEM_LIMIT),
    )(*args)


def _stack_heads(x, head_sel):
    xb = x.astype(BF16)
    zero = jnp.zeros_like(xb)
    return jnp.concatenate([jnp.where(m, xb, zero) for m in head_sel], axis=0)


def _rwkv_chunks(at, bh, kh, rt, v, head_sel):
    n = len(at)
    rng = range(n)
    T = at[0].shape[0]
    G = D_GROUP
    row = _iota((T, G), 0)
    lane_t = _iota((T, G), 1) % T
    strict = lane_t < row
    incl = lane_t <= row
    st = lambda x: _stack_heads(x, head_sel)
    lhs = [jnp.concatenate([at[j], rt[j]], axis=0).astype(BF16) for j in rng]
    ab = [_dot(lhs[j], st(bh[j]), _NT) for j in rng]
    ak = [_dot(lhs[j], st(kh[j]), _NT) for j in rng]
    n_u = [jnp.where(strict, ab[j][0:T], 0.0) for j in rng]
    a_ak = [jnp.where(strict, ak[j][0:T], 0.0) for j in rng]
    m_rb = [jnp.where(incl, ab[j][T:2 * T], 0.0).astype(BF16) for j in rng]
    m_rk = [jnp.where(incl, ak[j][T:2 * T], 0.0).astype(BF16) for j in rng]

    eye = jnp.where(lane_t == row, 1.0, 0.0)
    x_u = [eye + n_u[j] for j in rng]
    pw = [_dot(n_u[j].astype(BF16), st(n_u[j]), _NN) for j in rng]
    steps = T.bit_length() - 2
    for it in range(steps):
        w = [st(pw[j]) for j in rng]
        if it + 1 < steps:
            res = [_dot(jnp.concatenate([pw[j], x_u[j]], axis=0).astype(BF16), w[j], _NN) for j in rng]
            pw = [res[j][0:T] for j in rng]
            x_u = [x_u[j] + res[j][T:2 * T] for j in rng]
        else:
            x_u = [x_u[j] + _dot(x_u[j].astype(BF16), w[j], _NN) for j in rng]

    v_s = [st(v[j]) for j in rng]
    akv = [_dot(a_ak[j].astype(BF16), v_s[j], _NN) for j in rng]
    xb = [x_u[j].astype(BF16) for j in rng]
    p_u = [_dot(xb[j], st(at[j]), _NN) for j in rng]
    q_u = [_dot(xb[j], st(akv[j]), _NN) for j in rng]
    mkv = [_dot(m_rk[j], v_s[j], _NN) for j in rng]
    return p_u, q_u, m_rb, mkv


def _rwkv_kernel(has_vres, *refs):
    if has_vres:
        (f_ref, fprev_ref, vf_ref, mu_ref, vec_ref, w2_ref, a2_ref, g2_ref, v2_ref,
         y_ref, s_ref) = refs
    else:
        (f_ref, fprev_ref, mu_ref, vec_ref, w2_ref, a2_ref, g2_ref,
         y_ref, vout_ref, s_ref) = refs
    c = pl.program_id(1)
    T = RW_T
    R = RW_ROWS * RW_NB
    G = D_GROUP
    nchunks = RW_NCH * RW_NB

    @pl.when(c == 0)
    def _():
        s_ref[...] = jnp.zeros_like(s_ref)

    f = f_ref[...].reshape(R, RW_W)
    row = _iota((R, 1), 0)
    shifted = pltpu.roll(f, 1, axis=0)
    for b in range(RW_NB):
        prev_row = jnp.where(c == 0, 0.0, fprev_ref[b, 7:8, :])
        shifted = jnp.where(row == b * RW_ROWS, prev_row, shifted)
    x = f + (shifted - f) * mu_ref[...]
    r = x[:, 0:G]
    k = x[:, G:2 * G]
    v = x[:, 2 * G:3 * G]
    seg = x[:, 3 * G:3 * G + 128]
    w0, a0, k_k, k_a = (vec_ref[i:i + 1, :] for i in range(4))
    lnx_w, lnx_b, r_k = (vec_ref[i:i + 1, :] for i in range(4, 7))

    p = w0 + _mm3(jnp.tanh(seg), w2_ref[...])
    lw = -math.exp(-0.5) * _sigmoid(p)
    a = _sigmoid(a0 + _mm(seg, a2_ref[...]))
    g = _mm(_sigmoid(seg), g2_ref[...])
    if has_vres:
        seg2 = x[:, 3 * G + 128:3 * G + 256]
        v0 = vec_ref[7:8, :]
        v = v + (vf_ref[...].reshape(R, G) - v) * _sigmoid(v0 + _mm(seg2, v2_ref[...]))
    else:
        vout_ref[...] = v.reshape(RW_NB, RW_ROWS, G)

    head_sel = [(_iota((1, G), 1) // HEAD_DIM) == h for h in range(N_HEADS)]
    same = _same_head()
    bo = same.astype(BF16)
    kk = k * k_k
    kk = kk * lax.rsqrt(jnp.maximum(_mm(kk * kk, bo), 1e-24))
    k2 = k * (1.0 + (a - 1.0) * k_a)
    kka = kk * a

    ltri = (_iota((T, T), 1) <= _iota((T, T), 0)).astype(BF16)
    cs = jnp.concatenate([_mmx_l(ltri, lw[j * T:(j + 1) * T], pieces=2) for j in range(nchunks)], axis=0)
    c_last = jnp.concatenate(
        [jnp.broadcast_to(cs[(j + 1) * T - 1:(j + 1) * T, :], (T, G)) for j in range(nchunks)], axis=0)
    e_neg = jnp.exp(-cs)
    e_dec = jnp.exp(c_last - cs)
    at = -kk * jnp.exp(cs - lw)
    bh = kka * e_neg
    kh = k2 * e_neg
    rt = r * jnp.exp(cs)
    bdec = kka * e_dec
    kdec = k2 * e_dec

    chunks = lambda z: [z[j * T:(j + 1) * T] for j in range(nchunks)]
    rt_c, v_c, bdec_c, kdec_c = chunks(rt), chunks(v), chunks(bdec), chunks(kdec)
    p_u, q_u, m_rb, mkv = _rwkv_chunks(chunks(at), chunks(bh), chunks(kh), rt_c, v_c, head_sel)
    rp_lhs = [jnp.concatenate([rt_c[i], p_u[i]], axis=0).astype(BF16) for i in range(nchunks)]
    dec_t = [jnp.concatenate([bdec_c[i], kdec_c[i]], axis=0).T.astype(BF16) for i in range(nchunks)]
    w_col = [jnp.exp(jnp.broadcast_to(c_last[i * T:i * T + 1, :], (128, G))).T[:, 0:1]
             for i in range(nchunks)]
    s = [s_ref[b] for b in range(RW_NB)]
    rs = [None] * nchunks
    sas = [None] * nchunks
    seqs = range(RW_NB)
    for j in range(RW_NCH):
        ids = [b * RW_NCH + j for b in seqs]
        rp = [_dot(rp_lhs[i], s[b].astype(BF16), _NN) for b, i in zip(seqs, ids)]
        for b, i in zip(seqs, ids):
            rs[i] = rp[b][0:T]
            sas[i] = rp[b][T:2 * T] + q_u[i]
        upd = [_dot(dec_t[i], jnp.concatenate([sas[i], v_c[i]], axis=0).astype(BF16), _NN)
               for i in ids]
        s = [s[b] * w_col[i] + jnp.where(same, upd[b], 0.0) for b, i in zip(seqs, ids)]
    outs = [rs[i] + _dot(m_rb[i], _stack_heads(sas[i], head_sel), _NN) + mkv[i] for i in range(nchunks)]
    for b in range(RW_NB):
        s_ref[b] = s[b]
    o = jnp.concatenate(outs, axis=0)

    mean = _mmx_r(o, bo, pieces=2) * (1.0 / HEAD_DIM)
    d = o - mean
    var = _mm(d * d, bo) * (1.0 / HEAD_DIM)
    yn = d * lax.rsqrt(var + RWKV_GN_EPS) * lnx_w + lnx_b
    bonus = _mm(r * k2 * r_k, bo) * v
    y_ref[...] = ((yn + bonus) * g).reshape(RW_NB, RW_ROWS, G)


def _rwkv(rw, v_first, mu, vec, w2p, a2p, g2p, v2p, batch, seq):
    has_vres = v_first is not None
    nc = seq // RW_ROWS
    m = batch * seq
    row_spec = lambda w: pl.BlockSpec((RW_NB, RW_ROWS, w), lambda b, c: (b, c, 0))
    const = lambda shape: pl.BlockSpec(shape, lambda b, c: (0, 0))
    prev_spec = pl.BlockSpec(
        (RW_NB, 8, RW_W), lambda b, c: (b, jnp.maximum(c * (RW_ROWS // 8) - 1, 0), 0))
    rw3 = rw.reshape(batch, seq, RW_W)
    in_specs = [row_spec(RW_W), prev_spec]
    args = [rw3, rw3]
    if has_vres:
        in_specs.append(row_spec(D_GROUP))
        args.append(v_first.reshape(batch, seq, D_GROUP))
    in_specs += [const((1, RW_W)), const((8, D_GROUP)), const((128, D_GROUP)),
                 const((128, D_GROUP)), const((128, D_GROUP))]
    args += [mu, vec, w2p, a2p, g2p]
    if has_vres:
        in_specs.append(const((128, D_GROUP)))
        args.append(v2p)
    out_shape = [jax.ShapeDtypeStruct((batch, seq, D_GROUP), F32)]
    out_specs = [row_spec(D_GROUP)]
    if not has_vres:
        out_shape.append(jax.ShapeDtypeStruct((batch, seq, D_GROUP), F32))
        out_specs.append(row_spec(D_GROUP))
    outs = pl.pallas_call(
        functools.partial(_rwkv_kernel, has_vres),
        grid=(batch // RW_NB, nc),
        in_specs=in_specs,
        out_specs=out_specs,
        out_shape=out_shape,
        scratch_shapes=[pltpu.VMEM((RW_NB, D_GROUP, D_GROUP), F32)],
        compiler_params=pltpu.CompilerParams(dimension_semantics=("parallel", "arbitrary"),
                                             vmem_limit_bytes=VMEM_LIMIT),
    )(*args)
    y = outs[0].reshape(m, D_GROUP)
    if has_vres:
        return y, v_first
    return y, outs[1].reshape(m, D_GROUP)


def _attn_kernel(qkv_ref, o_ref, perm_ref, acc_ref, l_ref, m_ref, bias_ref):
    G = D_GROUP
    blk = ATTN_BLK
    seq = qkv_ref.shape[0]
    nres = DILATIONS[-1]
    sub = seq // nres
    head_sel = [(_iota((1, G), 1) // HEAD_DIM) == h for h in range(N_HEADS)]
    scale = HEAD_DIM ** -0.5 * math.log2(math.e)

    def jmap(a, d):
        run = blk * d // nres
        return (nres // d) * (a % run) + a // run

    to_perm = (_iota((blk, blk), 1) == jmap(_iota((blk, blk), 0), 1)).astype(BF16)
    from_perm = (_iota((blk, blk), 0) == jmap(_iota((blk, blk), 1), 1)).astype(BF16)
    run1 = blk // nres
    col_scale = jnp.where(_iota((1, AT_W), 1) < G, scale, 1.0)

    def permute(n, carry):
        r0 = pl.multiple_of(n * blk, blk)
        pb = jnp.dot(to_perm, (qkv_ref[pl.ds(r0, blk), :] * col_scale).astype(BF16),
                     preferred_element_type=F32)
        for e in range(nres):
            dst = pl.multiple_of(e * sub + n * run1, run1)
            perm_ref[pl.ds(dst, run1), :] = pb[e * run1:(e + 1) * run1, :]
        return carry

    lax.fori_loop(0, seq // blk, permute, 0, unroll=4)

    branches = tuple(reversed(DILATIONS))
    for bi, d in enumerate(branches):
        jq = jmap(_iota((blk, 2 * blk), 0), d)
        kb = _iota((blk, 2 * blk), 1)
        dist = blk + jq - (jmap(kb % blk, d) + (kb // blk) * blk)
        in_band = (dist >= 0) & (dist <= blk)
        for h in range(N_HEADS):
            bias_ref[bi * N_HEADS + h] = jnp.where(
                in_band, dist.astype(F32) * (-ALIBI_SLOPES[h] * d * math.log2(math.e)), -jnp.inf)

    for bi, d in enumerate(branches):
        runs = nres // d
        run = blk // runs
        nb = seq // (d * blk)
        first = bi == 0
        last = bi == len(branches) - 1

        def blocks(items, with_prev, bi=bi, d=d, runs=runs, run=run, first=first, last=last):
            its = range(len(items))
            heads = range(N_HEADS)

            def starts(rho, nn):
                return [pl.multiple_of((rho + d * e) * sub + nn * run, run) for e in range(runs)]

            def gather(st, c0):
                return jnp.concatenate([perm_ref[pl.ds(s0, run), c0:c0 + G] for s0 in st],
                                       axis=0).astype(BF16)

            cur = [starts(rho, n) for rho, n in items]
            qb = [gather(cur[i], 0) for i in its]
            if with_prev:
                prv = [starts(rho, n - 1) for rho, n in items]
                kcat = [jnp.concatenate([gather(prv[i], G), gather(cur[i], G)], axis=0) for i in its]
                vcat = [jnp.concatenate([gather(prv[i], 2 * G), gather(cur[i], 2 * G)], axis=0)
                        for i in its]
            else:
                kcat = [gather(cur[i], G) for i in its]
                vcat = [gather(cur[i], 2 * G) for i in its]
            zq = jnp.zeros_like(qb[0])
            kc0 = 0 if with_prev else blk
            pairs = [(0, 1), (2, 3)]
            s_p = [[_dot(jnp.concatenate([jnp.where(head_sel[h], qb[i], zq) for h in pr], axis=0),
                         kcat[i], _NT) for pr in pairs] for i in its]
            s = [[s_p[i][h // 2][(h % 2) * blk:(h % 2 + 1) * blk]
                  + bias_ref[bi * N_HEADS + h, :, kc0:2 * blk] for h in heads] for i in its]
            mh = [[jnp.max(s[i][h], axis=-1, keepdims=True) for h in heads] for i in its]
            ph = [[jnp.exp2(s[i][h] - mh[i][h]) for h in heads] for i in its]
            lh = [[jnp.sum(ph[i][h], axis=-1, keepdims=True) for h in heads] for i in its]
            pv_p = [[_dot(jnp.concatenate([ph[i][h].astype(BF16) for h in pr], axis=0), vcat[i], _NN)
                     for pr in pairs] for i in its]
            def by_head(parts):
                out = jnp.broadcast_to(parts[N_HEADS - 1], (blk, G))
                for h in reversed(range(N_HEADS - 1)):
                    out = jnp.where(head_sel[h], parts[h], out)
                return out

            acc_b = [by_head([pv_p[i][h // 2][(h % 2) * blk:(h % 2 + 1) * blk] for h in heads])
                     for i in its]
            l_b = [by_head(lh[i]) for i in its]
            m_b = [by_head(mh[i]) for i in its]
            for i in its:
                outs = []
                for e, s0 in enumerate(cur[i]):
                    idx = pl.ds(s0, run)
                    sl = slice(e * run, (e + 1) * run)
                    if first:
                        acc_ref[idx, :] = acc_b[i][sl]
                        l_ref[idx, :] = l_b[i][sl]
                        m_ref[idx, :] = m_b[i][sl]
                    else:
                        m_old = m_ref[idx, :]
                        m_new = jnp.maximum(m_old, m_b[i][sl])
                        w_old = jnp.exp2(m_old - m_new)
                        w_new = jnp.exp2(m_b[i][sl] - m_new)
                        acc_n = acc_ref[idx, :] * w_old + acc_b[i][sl] * w_new
                        l_n = l_ref[idx, :] * w_old + l_b[i][sl] * w_new
                        if last:
                            outs.append(acc_n / l_n)
                        else:
                            acc_ref[idx, :] = acc_n
                            l_ref[idx, :] = l_n
                            m_ref[idx, :] = m_new
                if last:
                    r0 = pl.multiple_of(items[i][1] * blk, blk)
                    o_ref[pl.ds(r0, blk), :] = _mmx_l(from_perm, jnp.concatenate(outs, axis=0))

        def run_blocks(count, index, with_prev, blocks=blocks):
            grp = ATTN_GROUP - 1 if with_prev else ATTN_GROUP

            def group(it, carry):
                blocks([index(grp * it + u) for u in range(grp)], with_prev)
                return carry

            if count >= grp:
                lax.fori_loop(0, count // grp, group, 0)
            rest = count % grp
            if rest:
                blocks([index(count - rest + u) for u in range(rest)], with_prev)

        run_blocks(d, lambda i: (i, 0), False)
        if nb > 1:
            run_blocks(d * (nb - 1), lambda i, nb=nb: (i // (nb - 1), i % (nb - 1) + 1), True)


def _attention(at, batch, seq):
    m = batch * seq
    return pl.pallas_call(
        _attn_kernel,
        grid=(batch,),
        in_specs=[pl.BlockSpec((seq, AT_W), lambda b: (b, 0))],
        out_specs=pl.BlockSpec((seq, D_GROUP), lambda b: (b, 0)),
        out_shape=jax.ShapeDtypeStruct((m, D_GROUP), F32),
        scratch_shapes=[pltpu.VMEM((seq, AT_W), F32)] + [pltpu.VMEM((seq, D_GROUP), F32)] * 3
        + [pltpu.VMEM((len(DILATIONS) * N_HEADS, ATTN_BLK, 2 * ATTN_BLK), F32)],
        compiler_params=pltpu.CompilerParams(dimension_semantics=("parallel",),
                                             vmem_limit_bytes=VMEM_LIMIT),
    )(at)


def _ssd_kernel(f_ref, fprev_ref, cw_ref, cb_ref, dtb_ref, vec_ref, ex_ref, y_ref, st_ref):
    c = pl.program_id(1)
    T = SSD_T
    G = D_GROUP
    N = SSD_STATE
    nb = SSD_NB
    R = nb * T
    seqs = range(nb)

    @pl.when(c == 0)
    def _():
        st_ref[...] = jnp.zeros_like(st_ref)

    f = f_ref[...].reshape(R, SD_W)
    xbc_raw = f[:, G:G + SSD_XBC]
    prev8 = [jnp.where(c == 0, 0.0, fprev_ref[b, :, G:G + SSD_XBC]) for b in seqs]
    row8 = _iota((8, 1), 0)
    conv = xbc_raw * cw_ref[3:4, :] + cb_ref[...]
    for j in range(1, 4):
        rolled = pltpu.roll(xbc_raw, j, axis=0)
        pieces = []
        for b in seqs:
            pieces.append(jnp.where(row8 < j, pltpu.roll(prev8[b], j, axis=0), rolled[b * T:b * T + 8]))
            pieces.append(rolled[b * T + 8:(b + 1) * T])
        conv = conv + jnp.concatenate(pieces, axis=0) * cw_ref[3 - j:4 - j, :]
    xbc = _silu(conv)
    xs = xbc[:, 0:G]
    bm = xbc[:, G:G + 2 * N]
    cm = xbc[:, G + 2 * N:G + 4 * N]
    z = f[:, 0:G]

    a_dense, d_dense, norm_w = (vec_ref[i:i + 1, :] for i in range(3))
    dt = _softplus(f[:, G + SSD_XBC:G + SSD_XBC + 128] + dtb_ref[...])
    dt_dense = _mmx_r(dt, ex_ref[...], pieces=2)
    da = dt_dense * a_dense
    xdt = xs * dt_dense
    ltri = (_iota((T, T), 1) <= _iota((T, T), 0))
    ltri_b = ltri.astype(BF16)
    umat = (_iota((T, T), 0) > _iota((T, T), 1)).astype(F32)
    masks = _head_masks()
    rows = [slice(b * T, (b + 1) * T) for b in seqs]

    cs = [_mmx_l(ltri_b, da[rows[b]], pieces=2) for b in seqs]
    scores = [[_mm(cm[rows[b], g * N:(g + 1) * N], bm[rows[b], g * N:(g + 1) * N], _NT)
               for g in range(2)] for b in seqs]
    y_diag = []
    for b in seqs:
        acc = None
        for h in range(N_HEADS):
            da_col = jnp.broadcast_to(da[rows[b], h * HEAD_DIM:h * HEAD_DIM + 1], (T, T))
            seg = _mmx_l(ltri_b, da_col * umat, pieces=2)
            dec = jnp.where(ltri, jnp.exp(jnp.where(ltri, seg, 0.0)), 0.0)
            term = _mm(scores[b][h // 2] * dec, xdt[rows[b]] * masks[h])
            acc = term if acc is None else acc + term
        y_diag.append(acc)

    ys = []
    for b in seqs:
        st = st_ref[b]
        cmb = cm[rows[b]]
        y_off = jnp.concatenate([_mm(cmb[:, 0:N], st[:, 0:N]), _mm(cmb[:, N:2 * N], st[:, N:2 * N])],
                                axis=1) * jnp.exp(cs[b])
        cs_last = cs[b][T - 1:T, :]
        xd = xdt[rows[b]] * jnp.exp(cs_last - cs[b])
        bmb = bm[rows[b]]
        st_ref[b] = st * jnp.exp(cs_last) + jnp.concatenate(
            [_mm(bmb[:, 0:N], xd[:, 0:N], _TN), _mm(bmb[:, N:2 * N], xd[:, N:2 * N], _TN)], axis=1)
        ys.append(y_diag[b] + y_off)

    y = (jnp.concatenate(ys, axis=0) + xs * d_dense) * _silu(z)
    halves = []
    for g in range(2):
        yg = y[:, g * N:(g + 1) * N]
        halves.append(yg * lax.rsqrt(jnp.mean(yg * yg, axis=-1, keepdims=True) + RMS_EPS))
    y_ref[...] = (jnp.concatenate(halves, axis=1) * norm_w).reshape(nb, T, G)


def _ssd(sd, cw, cb, dtb, vec, ex, batch, seq):
    nc = seq // SSD_T
    m = batch * seq
    const = lambda shape: pl.BlockSpec(shape, lambda b, c: (0, 0))
    sd3 = sd.reshape(batch, seq, SD_W)
    out = pl.pallas_call(
        _ssd_kernel,
        grid=(batch // SSD_NB, nc),
        in_specs=[pl.BlockSpec((SSD_NB, SSD_T, SD_W), lambda b, c: (b, c, 0)),
                  pl.BlockSpec((SSD_NB, 8, SD_W),
                               lambda b, c: (b, jnp.maximum(c * (SSD_T // 8) - 1, 0), 0)),
                  const((4, SSD_XBC)), const((1, SSD_XBC)), const((1, 128)), const((8, D_GROUP)),
                  const((128, D_GROUP))],
        out_specs=pl.BlockSpec((SSD_NB, SSD_T, D_GROUP), lambda b, c: (b, c, 0)),
        out_shape=jax.ShapeDtypeStruct((batch, seq, D_GROUP), F32),
        scratch_shapes=[pltpu.VMEM((SSD_NB, SSD_STATE, D_GROUP), F32)],
        compiler_params=pltpu.CompilerParams(dimension_semantics=("parallel", "arbitrary"),
                                             vmem_limit_bytes=VMEM_LIMIT),
    )(sd3, sd3, cw, cb, dtb, vec, ex)
    return out.reshape(m, D_GROUP)


def _hgrn_kernel(f_ref, vec_ref, y_ref, st_ref):
    c = pl.program_id(1)
    T = HG_T
    C = HG_C
    G = D_GROUP
    nblk = T // C

    @pl.when(c == 0)
    def _():
        st_ref[...] = jnp.zeros_like(st_ref)

    lb = vec_ref[0:1, :]
    norm_w = vec_ref[1:2, :]
    forget = lb + (1.0 - lb) * _sigmoid_tail(f_ref[:, G:2 * G])
    q = _silu(f_ref[:, 0:G])
    k = 1.0 - forget
    v = f_ref[:, 2 * G:3 * G]
    blk_tri = ((_iota((T, T), 0) // C == _iota((T, T), 1) // C)
               & (_iota((T, T), 1) <= _iota((T, T), 0))).astype(BF16)
    b = _mmx_l(blk_tri, jnp.log(forget), pieces=2)
    b2 = b * math.log2(math.e)

    head_sel = [(_iota((1, G), 1) // HEAD_DIM) == h for h in range(N_HEADS)]
    bo = _same_head().astype(BF16)
    t_idx = _iota((C, 1), 0)
    blocks = lambda z: [z[i * C:(i + 1) * C] for i in range(nblk)]
    bq, qq, kq, vv, b2q = blocks(b), blocks(q), blocks(k), blocks(v), blocks(b2)
    b_last = [bq[i][C - 1:C, :] for i in range(nblk)]

    upd = [_dot(_stack_heads(vv[i], head_sel), _stack_heads(kq[i] * jnp.exp(b_last[i] - bq[i]), head_sel),
                _TN) for i in range(nblk)]
    q_in = [qq[i] * jnp.exp(bq[i]) for i in range(nblk)]
    half = C // 2
    o_intra = []
    for i in range(nblk):
        parts = []
        for s in range(C):
            lo = 0 if s < half else half
            e = jnp.exp2(b2q[i][lo:C] - b2q[i][s:s + 1, :])
            parts.append(jnp.where(t_idx[lo:C] >= s, qq[i][lo:C] * e * kq[i][s:s + 1, :], 0.0))
        z = _mm(jnp.concatenate(parts, axis=0), bo)
        o_lo = z[0:half] * vv[i][0:1, :]
        o_hi = z[half:C] * vv[i][0:1, :]
        for s in range(1, half):
            o_lo = o_lo + z[s * C:s * C + half] * vv[i][s:s + 1, :]
            o_hi = o_hi + z[s * C + half:(s + 1) * C] * vv[i][s:s + 1, :]
        for s in range(half, C):
            r0 = half * C + (s - half) * half
            o_hi = o_hi + z[r0:r0 + half] * vv[i][s:s + 1, :]
        o_intra.append(jnp.concatenate([o_lo, o_hi], axis=0))

    st = st_ref[...]
    outs = []
    for i in range(nblk):
        outs.append(_mm(q_in[i], st, _NT) + o_intra[i])
        st = st * jnp.exp(b_last[i]) + upd[i]
    st_ref[...] = st
    o = jnp.concatenate(outs, axis=0)
    ms = _mmx_r(o * o, bo, pieces=2) * (1.0 / HEAD_DIM)
    y_ref[...] = o * lax.rsqrt(ms + RMS_EPS) * norm_w * _silu(f_ref[:, 3 * G:4 * G])


def _hgrn(hg, vec, batch, seq):
    nc = seq // HG_T
    m = batch * seq
    return pl.pallas_call(
        _hgrn_kernel,
        grid=(batch, nc),
        in_specs=[pl.BlockSpec((HG_T, HG_W), lambda b, c: (b * nc + c, 0)),
                  pl.BlockSpec((8, D_GROUP), lambda b, c: (0, 0))],
        out_specs=pl.BlockSpec((HG_T, D_GROUP), lambda b, c: (b * nc + c, 0)),
        out_shape=jax.ShapeDtypeStruct((m, D_GROUP), F32),
        scratch_shapes=[pltpu.VMEM((D_GROUP, D_GROUP), F32)],
        compiler_params=pltpu.CompilerParams(dimension_semantics=("parallel", "arbitrary"),
                                             vmem_limit_bytes=VMEM_LIMIT),
    )(hg, vec)


def _post_kernel(x_ref, ya_ref, yb_ref, yc_ref, yd_ref, wo_ref, wu_ref, wd_ref, ln_ref, o_ref):
    x = x_ref[...]
    mix = jnp.zeros(x.shape, F32)
    for i, ref in enumerate((ya_ref, yb_ref, yc_ref, yd_ref)):
        mix = mix + jnp.dot(ref[...].astype(BF16), wo_ref[i * D_GROUP:(i + 1) * D_GROUP, :],
                            preferred_element_type=F32)
    x1 = _layer_norm(ALPHA * x + mix, ln_ref[0:1, :], ln_ref[1:2, :])
    xb = x1.astype(BF16)
    acc = ALPHA * x1
    for c0 in range(0, D_FF, FF_CHUNK):
        h = jnp.maximum(jnp.dot(xb, wu_ref[:, c0:c0 + FF_CHUNK], preferred_element_type=F32), 0.0)
        acc = acc + jnp.dot((h * h).astype(BF16), wd_ref[c0:c0 + FF_CHUNK, :],
                            preferred_element_type=F32)
    o_ref[...] = _layer_norm(acc, ln_ref[2:3, :], ln_ref[3:4, :])


def _post(x2d, ys, wo, wu, wd, ln):
    m = x2d.shape[0]
    row = lambda w: pl.BlockSpec((POST_TM, w), lambda i: (i, 0))
    res = lambda shape: pl.BlockSpec(shape, lambda i: (0, 0), pipeline_mode=pl.Buffered(1))
    return pl.pallas_call(
        _post_kernel,
        grid=(m // POST_TM,),
        in_specs=[row(D_MODEL)] + [row(D_GROUP)] * 4
        + [res((D_MODEL, D_MODEL)), res((D_MODEL, D_FF)), res((D_FF, D_MODEL)), res((8, D_MODEL))],
        out_specs=row(D_MODEL),
        out_shape=jax.ShapeDtypeStruct((m, D_MODEL), F32),
        compiler_params=pltpu.CompilerParams(dimension_semantics=("parallel",),
                                             vmem_limit_bytes=VMEM_LIMIT),
    )(x2d, *ys, wo, wu, wd, ln)


def _pad_cols(a, width):
    return jnp.pad(a, ((0, 0), (0, width - a.shape[1])))


def _pad_rows(a, top, total):
    return jnp.pad(a, ((top, total - top - a.shape[0]), (0, 0)))


def _rep_heads(v):
    return jnp.repeat(v.astype(F32), HEAD_DIM)[None, :]


def _rows8(rows, width):
    out = jnp.concatenate([r.reshape(1, width).astype(F32) for r in rows], axis=0)
    return jnp.pad(out, ((0, 8 - out.shape[0]), (0, 0)))


def kernel(x, lower_bounds, w_in, w_in_vres, mu_shift, mu_vres, rwkv_w0, rwkv_w2, rwkv_a0, rwkv_a2, rwkv_g2, rwkv_k_k, rwkv_k_a, rwkv_r_k, rwkv_lnx_w, rwkv_lnx_b, rwkv_v0, rwkv_v2, ssd_conv_w, ssd_conv_b, ssd_dt_bias, ssd_A_log, ssd_D, ssd_norm_w, hgrn_norm_w, w_out, ln1_w, ln1_b, w_up, w_down, ln2_w, ln2_b):
    batch, seq, _ = x.shape
    lb = jax.nn.softmax(lower_bounds.astype(F32), axis=0)
    lb = jnp.cumsum(lb, axis=0) - lb[0]
    expand = (jnp.arange(128)[:, None] == (jnp.arange(D_GROUP)[None, :] // HEAD_DIM)).astype(BF16)

    h = x.reshape(batch * seq, D_MODEL)
    v_first = None
    for l in range(DEPTH):
        mu = mu_shift[l][None, :]
        w_vres = None
        if l > 0:
            w_vres = _pad_cols(w_in_vres[l - 1], 128).astype(BF16)
            mu = jnp.concatenate([mu, mu_vres[l - 1][None, :]], axis=1)
        rw, at, sd, hg = _project(h, w_in, l, w_vres)

        vec_rows = [rwkv_w0[l], rwkv_a0[l], rwkv_k_k[l], rwkv_k_a[l], rwkv_lnx_w[l], rwkv_lnx_b[l],
                    rwkv_r_k[l].reshape(-1)]
        if l > 0:
            vec_rows.append(rwkv_v0[l - 1])
        y_a, v_first = _rwkv(
            rw, v_first, _pad_cols(mu, RW_W), _rows8(vec_rows, D_GROUP),
            _pad_rows(rwkv_w2[l], 0, 128), _pad_rows(rwkv_a2[l], 32, 128), _pad_rows(rwkv_g2[l], 64, 128),
            _pad_rows(rwkv_v2[l - 1], 0, 128) if l > 0 else None, batch, seq)
        y_b = _attention(at, batch, seq)
        ssd_vec = _rows8([_rep_heads(-jnp.exp(ssd_A_log[l].astype(F32))), _rep_heads(ssd_D[l]),
                          ssd_norm_w[l]], D_GROUP)
        y_c = _ssd(sd, ssd_conv_w[l].astype(F32), ssd_conv_b[l][None, :],
                   _pad_cols(ssd_dt_bias[l][None, :], 128), ssd_vec, expand, batch, seq)
        y_d = _hgrn(hg, _rows8([lb[l], hgrn_norm_w[l]], D_GROUP), batch, seq)
        ln = _rows8([ln1_w[l], ln1_b[l], ln2_w[l], ln2_b[l]], D_MODEL)
        h = _post(h, (y_a, y_b, y_c, y_d), w_out[l].astype(BF16), w_up[l].astype(BF16),
                  w_down[l].astype(BF16), ln)
    return h.reshape(batch, seq, D_MODEL)
```

```python
import functools
import math

import jax
import jax.numpy as jnp
from jax import lax
from jax.experimental import pallas as pl
from jax.experimental.pallas import tpu as pltpu

F32 = jnp.float32
BF16 = jnp.bfloat16

D_MODEL = 1024
DEPTH = 2
D_GROUP = 256
HEAD_DIM = 64
N_HEADS = 4
D_FF = 4 * D_MODEL
ALPHA = (2.0 * DEPTH) ** 0.25
LN_EPS = 1e-5
RMS_EPS = 1e-5
RWKV_GN_EPS = HEAD_DIM * 1e-5
RWKV_COLS = 896
ATTN_COLS = 768
SSD_STATE = 128
SSD_XBC = 768
SSD_COLS = 1028
HGRN_COLS = 1024
DILATIONS = (1, 4, 16)
ATTN_BLK = 128
ATTN_GROUP = 4
ALIBI_SLOPES = tuple(2.0 ** (-8.0 * (h + 1) / N_HEADS) for h in range(N_HEADS))

RW_W, AT_W, SD_W, HG_W = 1024, 768, 1152, 1024
PROJ_W = RW_W + AT_W + SD_W + HG_W

RW_T = 64
RW_NCH = 2
RW_NB = 8
RW_ROWS = RW_T * RW_NCH
SSD_T = 128
SSD_NB = 4
HG_C = 16
HG_T = 256
PROJ_TM = 512
POST_TM = 1024
FF_CHUNK = 512

VMEM_LIMIT = 56 * 1024 * 1024


def _dot(a, b, dims):
    return lax.dot_general(a, b, (dims, ((), ())), preferred_element_type=F32)


_NN = ((1,), (0,))
_NT = ((1,), (1,))
_TN = ((0,), (0,))


def _mm(a, b, dims=_NN):
    return _dot(a.astype(BF16), b.astype(BF16), dims)


def _split(x, pieces):
    out = []
    for i in range(pieces):
        hi = x.astype(BF16)
        out.append(hi)
        if i + 1 < pieces:
            x = x - hi.astype(F32)
    return out


def _mmx_l(m01, x, dims=_NN, pieces=3):
    return sum(_dot(m01, xp, dims) for xp in _split(x, pieces))


def _mmx_r(x, m01, dims=_NN, pieces=3):
    return sum(_dot(xp, m01, dims) for xp in _split(x, pieces))


def _mm3(a, b, dims=_NN):
    ah = a.astype(BF16)
    al = (a - ah.astype(F32)).astype(BF16)
    bh = b.astype(BF16)
    bl = (b - bh.astype(F32)).astype(BF16)
    return _dot(ah, bh, dims) + _dot(ah, bl, dims) + _dot(al, bh, dims)


def _iota(shape, dim):
    return lax.broadcasted_iota(jnp.int32, shape, dim)


def _head_masks(width=D_GROUP):
    lane = _iota((1, width), 1) // HEAD_DIM
    return [(lane == h).astype(F32) for h in range(N_HEADS)]


def _same_head(n=D_GROUP):
    return (_iota((n, n), 0) // HEAD_DIM) == (_iota((n, n), 1) // HEAD_DIM)


def _sigmoid(x):
    return 0.5 * jnp.tanh(0.5 * x) + 0.5


def _sigmoid_tail(x):
    return 1.0 / (1.0 + jnp.exp(-x))


def _silu(x):
    return x * _sigmoid(x)


def _softplus(x):
    return jnp.maximum(x, 0.0) + jnp.log(1.0 + jnp.exp(-jnp.abs(x)))


def _layer_norm(x, w, b):
    mu = jnp.mean(x, axis=-1, keepdims=True)
    d = x - mu
    var = jnp.mean(d * d, axis=-1, keepdims=True)
    return d * lax.rsqrt(var + LN_EPS) * w + b


def _shift_rows(x, prev_row):
    row = _iota((x.shape[0], 1), 0)
    return jnp.where(row == 0, prev_row, pltpu.roll(x, 1, axis=0))


def _proj_kernel(has_vres, *refs):
    if has_vres:
        x_ref, w_ref, wv_ref, rw_ref, at_ref, sd_ref, hg_ref = refs
    else:
        x_ref, w_ref, rw_ref, at_ref, sd_ref, hg_ref = refs
    xb = x_ref[...].astype(BF16)
    n_in = w_ref.shape[1]

    def emit(ref, c_out, c_in, width):
        wb = w_ref[:, c_in:c_in + width].astype(BF16)
        ref[:, c_out:c_out + width] = jnp.dot(xb, wb, preferred_element_type=F32)

    src = 0
    for ref, cols, padded in ((rw_ref, RWKV_COLS, RW_W), (at_ref, ATTN_COLS, AT_W),
                              (sd_ref, SSD_COLS, SD_W), (hg_ref, HGRN_COLS, HG_W)):
        for c0 in range(0, padded, 256):
            width = min(256, padded - c0)
            if c0 + width <= cols:
                emit(ref, c0, src + c0, width)
                continue
            real = max(cols - c0, 0)
            real_pad = -(-real // 128) * 128
            if has_vres and ref is rw_ref and real == real_pad and real_pad + wv_ref.shape[1] == width:
                wb = jnp.concatenate([w_ref[:, src + c0:src + c0 + real].astype(BF16), wv_ref[...]], axis=1)
                ref[:, c0:c0 + width] = jnp.dot(xb, wb, preferred_element_type=F32)
                continue
            if real:
                take = min(real_pad, n_in - (src + c0))
                wb = w_ref[:, src + c0:src + c0 + take].astype(BF16)
                lane = _iota((1, take), 1)
                wb = jnp.where(lane < real, wb, jnp.zeros_like(wb))
                ref[:, c0:c0 + take] = jnp.dot(xb, wb, preferred_element_type=F32)
                if take < real_pad:
                    ref[:, c0 + take:c0 + real_pad] = jnp.zeros((xb.shape[0], real_pad - take), F32)
            rest = width - real_pad
            if rest:
                if has_vres and ref is rw_ref:
                    ref[:, c0 + real_pad:c0 + width] = jnp.dot(xb, wv_ref[...], preferred_element_type=F32)
                else:
                    ref[:, c0 + real_pad:c0 + width] = jnp.zeros((xb.shape[0], rest), F32)
        src += cols


def _project(x2d, w_in, layer, w_vres):
    m = x2d.shape[0]
    n_in = w_in.shape[2]
    has_vres = w_vres is not None
    outs = [jax.ShapeDtypeStruct((m, w), F32) for w in (RW_W, AT_W, SD_W, HG_W)]
    in_specs = [pl.BlockSpec((PROJ_TM, D_MODEL), lambda i: (i, 0)),
                pl.BlockSpec((None, D_MODEL, n_in), lambda i: (layer, 0, 0), pipeline_mode=pl.Buffered(1))]
    args = [x2d, w_in]
    if has_vres:
        in_specs.append(pl.BlockSpec((D_MODEL, 128), lambda i: (0, 0)))
        args.append(w_vres)
    return pl.pallas_call(
        functools.partial(_proj_kernel, has_vres),
        grid=(m // PROJ_TM,),
        in_specs=in_specs,
        out_specs=[pl.BlockSpec((PROJ_TM, w), lambda i: (i, 0)) for w in (RW_W, AT_W, SD_W, HG_W)],
        out_shape=outs,
        compiler_params=pltpu.CompilerParams(dimension_semantics=("parallel",),
                                             vmem_limit_bytes=VMEM_LIMIT),
    )(*args)


def _stack_heads(x, head_sel):
    xb = x.astype(BF16)
    zero = jnp.zeros_like(xb)
    return jnp.concatenate([jnp.where(m, xb, zero) for m in head_sel], axis=0)


def _rwkv_chunks(at, bh, kh, rt, v, head_sel):
    n = len(at)
    rng = range(n)
    T = at[0].shape[0]
    G = D_GROUP
    row = _iota((T, G), 0)
    lane_t = _iota((T, G), 1) % T
    strict = lane_t < row
    incl = lane_t <= row
    st = lambda x: _stack_heads(x, head_sel)
    lhs = [jnp.concatenate([at[j], rt[j]], axis=0).astype(BF16) for j in rng]
    ab = [_dot(lhs[j], st(bh[j]), _NT) for j in rng]
    ak = [_dot(lhs[j], st(kh[j]), _NT) for j in rng]
    n_u = [jnp.where(strict, ab[j][0:T], 0.0) for j in rng]
    a_ak = [jnp.where(strict, ak[j][0:T], 0.0) for j in rng]
    m_rb = [jnp.where(incl, ab[j][T:2 * T], 0.0).astype(BF16) for j in rng]
    m_rk = [jnp.where(incl, ak[j][T:2 * T], 0.0).astype(BF16) for j in rng]

    eye = jnp.where(lane_t == row, 1.0, 0.0)
    x_u = [eye + n_u[j] for j in rng]
    pw = [_dot(n_u[j].astype(BF16), st(n_u[j]), _NN) for j in rng]
    steps = T.bit_length() - 2
    for it in range(steps):
        w = [st(pw[j]) for j in rng]
        if it + 1 < steps:
            res = [_dot(jnp.concatenate([pw[j], x_u[j]], axis=0).astype(BF16), w[j], _NN) for j in rng]
            pw = [res[j][0:T] for j in rng]
            x_u = [x_u[j] + res[j][T:2 * T] for j in rng]
        else:
            x_u = [x_u[j] + _dot(x_u[j].astype(BF16), w[j], _NN) for j in rng]

    v_s = [st(v[j]) for j in rng]
    akv = [_dot(a_ak[j].astype(BF16), v_s[j], _NN) for j in rng]
    xb = [x_u[j].astype(BF16) for j in rng]
    p_u = [_dot(xb[j], st(at[j]), _NN) for j in rng]
    q_u = [_dot(xb[j], st(akv[j]), _NN) for j in rng]
    mkv = [_dot(m_rk[j], v_s[j], _NN) for j in rng]
    return p_u, q_u, m_rb, mkv


def _rwkv_kernel(has_vres, *refs):
    if has_vres:
        (f_ref, fprev_ref, vf_ref, mu_ref, vec_ref, w2_ref, a2_ref, g2_ref, v2_ref,
         y_ref, s_ref) = refs
    else:
        (f_ref, fprev_ref, mu_ref, vec_ref, w2_ref, a2_ref, g2_ref,
         y_ref, vout_ref, s_ref) = refs
    c = pl.program_id(1)
    T = RW_T
    G = D_GROUP

    @pl.when(c == 0)
    def _():
        s_ref[...] = jnp.zeros_like(s_ref)

    w0, a0, k_k, k_a = (vec_ref[i:i + 1, :] for i in range(4))
    lnx_w, lnx_b, r_k = (vec_ref[i:i + 1, :] for i in range(4, 7))
    head_sel = [(_iota((1, G), 1) // HEAD_DIM) == h for h in range(N_HEADS)]
    same = _same_head()
    bo = same.astype(BF16)
    ltri = (_iota((T, T), 1) <= _iota((T, T), 0)).astype(BF16)

    def prologue(b0, nb):
        R = RW_ROWS * nb
        nchunks = RW_NCH * nb
        f = f_ref[b0:b0 + nb].reshape(R, RW_W)
        row = _iota((R, 1), 0)
        shifted = pltpu.roll(f, 1, axis=0)
        for b in range(nb):
            prev_row = jnp.where(c == 0, 0.0, fprev_ref[b0 + b, 7:8, :])
            shifted = jnp.where(row == b * RW_ROWS, prev_row, shifted)
        x = f + (shifted - f) * mu_ref[...]
        r = x[:, 0:G]
        k = x[:, G:2 * G]
        v = x[:, 2 * G:3 * G]
        seg = x[:, 3 * G:3 * G + 128]
        p = w0 + _mm3(jnp.tanh(seg), w2_ref[...])
        lw = -math.exp(-0.5) * _sigmoid(p)
        a = _sigmoid(a0 + _mm(seg, a2_ref[...]))
        g = _mm(_sigmoid(seg), g2_ref[...])
        if has_vres:
            seg2 = x[:, 3 * G + 128:3 * G + 256]
            v0 = vec_ref[7:8, :]
            v = v + (vf_ref[b0:b0 + nb].reshape(R, G) - v) * _sigmoid(v0 + _mm(seg2, v2_ref[...]))
        else:
            vout_ref[b0:b0 + nb] = v.reshape(nb, RW_ROWS, G)
        kk = k * k_k
        kk = kk * lax.rsqrt(jnp.maximum(_mm(kk * kk, bo), 1e-24))
        k2 = k * (1.0 + (a - 1.0) * k_a)
        kka = kk * a
        cs = jnp.concatenate([_mmx_l(ltri, lw[j * T:(j + 1) * T], pieces=2) for j in range(nchunks)],
                             axis=0)
        c_last = jnp.concatenate(
            [jnp.broadcast_to(cs[(j + 1) * T - 1:(j + 1) * T, :], (T, G)) for j in range(nchunks)], axis=0)
        e_neg = jnp.exp(-cs)
        e_dec = jnp.exp(c_last - cs)
        chunks = lambda z: [z[j * T:(j + 1) * T] for j in range(nchunks)]
        return dict(b0=b0, nb=nb, nchunks=nchunks, r=r, k2=k2, v=v, g=g, c_last=c_last,
                    at=chunks(-kk * jnp.exp(cs - lw)), bh=chunks(kka * e_neg), kh=chunks(k2 * e_neg),
                    rt=chunks(r * jnp.exp(cs)), bdec=chunks(kka * e_dec), kdec=chunks(k2 * e_dec),
                    vc=chunks(v))

    def middle(ctx):
        n = ctx["nchunks"]
        p_u, q_u, m_rb, mkv = _rwkv_chunks(ctx["at"], ctx["bh"], ctx["kh"], ctx["rt"], ctx["vc"], head_sel)
        ctx.update(q_u=q_u, m_rb=m_rb, mkv=mkv)
        ctx["rp_lhs"] = [jnp.concatenate([ctx["rt"][i], p_u[i]], axis=0).astype(BF16) for i in range(n)]
        ctx["dec_t"] = [jnp.concatenate([ctx["bdec"][i], ctx["kdec"][i]], axis=0).T.astype(BF16)
                        for i in range(n)]
        ctx["w_col"] = [jnp.exp(jnp.broadcast_to(ctx["c_last"][i * T:i * T + 1, :], (128, G))).T[:, 0:1]
                        for i in range(n)]

    def chain(ctx):
        b0, nb, n = ctx["b0"], ctx["nb"], ctx["nchunks"]
        s = [s_ref[b0 + b] for b in range(nb)]
        rs = [None] * n
        sas = [None] * n
        seqs = range(nb)
        for j in range(RW_NCH):
            ids = [b * RW_NCH + j for b in seqs]
            rp = [_dot(ctx["rp_lhs"][i], s[b].astype(BF16), _NN) for b, i in zip(seqs, ids)]
            for b, i in zip(seqs, ids):
                rs[i] = rp[b][0:T]
                sas[i] = rp[b][T:2 * T] + ctx["q_u"][i]
            upd = [_dot(ctx["dec_t"][i], jnp.concatenate([sas[i], ctx["vc"][i]], axis=0).astype(BF16), _NN)
                   for i in ids]
            s = [s[b] * ctx["w_col"][i] + jnp.where(same, upd[b], 0.0) for b, i in zip(seqs, ids)]
        outs = [rs[i] + _dot(ctx["m_rb"][i], _stack_heads(sas[i], head_sel), _NN) + ctx["mkv"][i]
                for i in range(n)]
        for b in seqs:
            s_ref[b0 + b] = s[b]
        ctx["o"] = jnp.concatenate(outs, axis=0)

    def epilogue(ctx):
        o = ctx["o"]
        mean = _mmx_r(o, bo, pieces=2) * (1.0 / HEAD_DIM)
        d = o - mean
        var = _mm(d * d, bo) * (1.0 / HEAD_DIM)
        yn = d * lax.rsqrt(var + RWKV_GN_EPS) * lnx_w + lnx_b
        bonus = _mm(ctx["r"] * ctx["k2"] * r_k, bo) * ctx["v"]
        y_ref[ctx["b0"]:ctx["b0"] + ctx["nb"]] = ((yn + bonus) * ctx["g"]).reshape(ctx["nb"], RW_ROWS, G)

    ctx = prologue(0, RW_NB)
    middle(ctx)
    chain(ctx)
    epilogue(ctx)


def _rwkv(rw, v_first, mu, vec, w2p, a2p, g2p, v2p, batch, seq):
    has_vres = v_first is not None
    nc = seq // RW_ROWS
    m = batch * seq
    row_spec = lambda w: pl.BlockSpec((RW_NB, RW_ROWS, w), lambda b, c: (b, c, 0))
    const = lambda shape: pl.BlockSpec(shape, lambda b, c: (0, 0))
    prev_spec = pl.BlockSpec(
        (RW_NB, 8, RW_W), lambda b, c: (b, jnp.maximum(c * (RW_ROWS // 8) - 1, 0), 0))
    rw3 = rw.reshape(batch, seq, RW_W)
    in_specs = [row_spec(RW_W), prev_spec]
    args = [rw3, rw3]
    if has_vres:
        in_specs.append(row_spec(D_GROUP))
        args.append(v_first.reshape(batch, seq, D_GROUP))
    in_specs += [const((1, RW_W)), const((8, D_GROUP)), const((128, D_GROUP)),
                 const((128, D_GROUP)), const((128, D_GROUP))]
    args += [mu, vec, w2p, a2p, g2p]
    if has_vres:
        in_specs.append(const((128, D_GROUP)))
        args.append(v2p)
    out_shape = [jax.ShapeDtypeStruct((batch, seq, D_GROUP), F32)]
    out_specs = [row_spec(D_GROUP)]
    if not has_vres:
        out_shape.append(jax.ShapeDtypeStruct((batch, seq, D_GROUP), F32))
        out_specs.append(row_spec(D_GROUP))
    outs = pl.pallas_call(
        functools.partial(_rwkv_kernel, has_vres),
        grid=(batch // RW_NB, nc),
        in_specs=in_specs,
        out_specs=out_specs,
        out_shape=out_shape,
        scratch_shapes=[pltpu.VMEM((RW_NB, D_GROUP, D_GROUP), F32)],
        compiler_params=pltpu.CompilerParams(dimension_semantics=("parallel", "arbitrary"),
                                             vmem_limit_bytes=VMEM_LIMIT),
    )(*args)
    y = outs[0].reshape(m, D_GROUP)
    if has_vres:
        return y, v_first
    return y, outs[1].reshape(m, D_GROUP)


def _attn_kernel(qkv_ref, o_ref, perm_ref, acc_ref, l_ref, m_ref, bias_ref):
    G = D_GROUP
    blk = ATTN_BLK
    seq = qkv_ref.shape[0]
    nres = DILATIONS[-1]
    sub = seq // nres
    head_sel = [(_iota((1, G), 1) // HEAD_DIM) == h for h in range(N_HEADS)]
    scale = HEAD_DIM ** -0.5 * math.log2(math.e)

    def jmap(a, d):
        run = blk * d // nres
        return (nres // d) * (a % run) + a // run

    to_perm = (_iota((blk, blk), 1) == jmap(_iota((blk, blk), 0), 1)).astype(BF16)
    from_perm = (_iota((blk, blk), 0) == jmap(_iota((blk, blk), 1), 1)).astype(BF16)
    run1 = blk // nres
    col_scale = jnp.where(_iota((1, AT_W), 1) < G, scale, 1.0)

    def permute(n, carry):
        r0 = pl.multiple_of(n * blk, blk)
        pb = jnp.dot(to_perm, (qkv_ref[pl.ds(r0, blk), :] * col_scale).astype(BF16),
                     preferred_element_type=F32)
        for e in range(nres):
            dst = pl.multiple_of(e * sub + n * run1, run1)
            perm_ref[pl.ds(dst, run1), :] = pb[e * run1:(e + 1) * run1, :]
        return carry

    lax.fori_loop(0, seq // blk, permute, 0, unroll=4)

    branches = tuple(reversed(DILATIONS))
    for bi, d in enumerate(branches):
        jq = jmap(_iota((blk, 2 * blk), 0), d)
        kb = _iota((blk, 2 * blk), 1)
        dist = blk + jq - (jmap(kb % blk, d) + (kb // blk) * blk)
        in_band = (dist >= 0) & (dist <= blk)
        for h in range(N_HEADS):
            bias_ref[bi * N_HEADS + h] = jnp.where(
                in_band, dist.astype(F32) * (-ALIBI_SLOPES[h] * d * math.log2(math.e)), -jnp.inf)

    for bi, d in enumerate(branches):
        runs = nres // d
        run = blk // runs
        nb = seq // (d * blk)
        first = bi == 0
        last = bi == len(branches) - 1

        def blocks(items, with_prev, bi=bi, d=d, runs=runs, run=run, first=first, last=last):
            its = range(len(items))
            heads = range(N_HEADS)

            def starts(rho, nn):
                return [pl.multiple_of((rho + d * e) * sub + nn * run, run) for e in range(runs)]

            def gather(st, c0):
                return jnp.concatenate([perm_ref[pl.ds(s0, run), c0:c0 + G] for s0 in st],
                                       axis=0).astype(BF16)

            cur = [starts(rho, n) for rho, n in items]
            qb = [gather(cur[i], 0) for i in its]
            if with_prev:
                prv = [starts(rho, n - 1) for rho, n in items]
                kcat = [jnp.concatenate([gather(prv[i], G), gather(cur[i], G)], axis=0) for i in its]
                vcat = [jnp.concatenate([gather(prv[i], 2 * G), gather(cur[i], 2 * G)], axis=0)
                        for i in its]
            else:
                kcat = [gather(cur[i], G) for i in its]
                vcat = [gather(cur[i], 2 * G) for i in its]
            zq = jnp.zeros_like(qb[0])
            kc0 = 0 if with_prev else blk
            pairs = [(0, 1), (2, 3)]
            s_p = [[_dot(jnp.concatenate([jnp.where(head_sel[h], qb[i], zq) for h in pr], axis=0),
                         kcat[i], _NT) for pr in pairs] for i in its]
            s = [[s_p[i][h // 2][(h % 2) * blk:(h % 2 + 1) * blk]
                  + bias_ref[bi * N_HEADS + h, :, kc0:2 * blk] for h in heads] for i in its]
            mh = [[jnp.max(s[i][h], axis=-1, keepdims=True) for h in heads] for i in its]
            ph = [[jnp.exp2(s[i][h] - mh[i][h]) for h in heads] for i in its]
            lh = [[jnp.sum(ph[i][h], axis=-1, keepdims=True) for h in heads] for i in its]
            pv_p = [[_dot(jnp.concatenate([ph[i][h].astype(BF16) for h in pr], axis=0), vcat[i], _NN)
                     for pr in pairs] for i in its]
            def by_head(parts):
                out = jnp.broadcast_to(parts[N_HEADS - 1], (blk, G))
                for h in reversed(range(N_HEADS - 1)):
                    out = jnp.where(head_sel[h], parts[h], out)
                return out

            acc_b = [by_head([pv_p[i][h // 2][(h % 2) * blk:(h % 2 + 1) * blk] for h in heads])
                     for i in its]
            l_b = [by_head(lh[i]) for i in its]
            m_b = [by_head(mh[i]) for i in its]
            for i in its:
                outs = []
                for e, s0 in enumerate(cur[i]):
                    idx = pl.ds(s0, run)
                    sl = slice(e * run, (e + 1) * run)
                    if first:
                        acc_ref[idx, :] = acc_b[i][sl]
                        l_ref[idx, :] = l_b[i][sl]
                        m_ref[idx, :] = m_b[i][sl]
                    else:
                        m_old = m_ref[idx, :]
                        m_new = jnp.maximum(m_old, m_b[i][sl])
                        w_old = jnp.exp2(m_old - m_new)
                        w_new = jnp.exp2(m_b[i][sl] - m_new)
                        acc_n = acc_ref[idx, :] * w_old + acc_b[i][sl] * w_new
                        l_n = l_ref[idx, :] * w_old + l_b[i][sl] * w_new
                        if last:
                            outs.append(acc_n / l_n)
                        else:
                            acc_ref[idx, :] = acc_n
                            l_ref[idx, :] = l_n
                            m_ref[idx, :] = m_new
                if last:
                    r0 = pl.multiple_of(items[i][1] * blk, blk)
                    o_ref[pl.ds(r0, blk), :] = _mmx_l(from_perm, jnp.concatenate(outs, axis=0))

        def run_blocks(count, index, with_prev, blocks=blocks):
            grp = ATTN_GROUP - 1 if with_prev else ATTN_GROUP

            def group(it, carry):
                blocks([index(grp * it + u) for u in range(grp)], with_prev)
                return carry

            if count >= grp:
                lax.fori_loop(0, count // grp, group, 0)
            rest = count % grp
            if rest:
                blocks([index(count - rest + u) for u in range(rest)], with_prev)

        run_blocks(d, lambda i: (i, 0), False)
        if nb > 1:
            run_blocks(d * (nb - 1), lambda i, nb=nb: (i // (nb - 1), i % (nb - 1) + 1), True)


def _attention(at, batch, seq):
    m = batch * seq
    return pl.pallas_call(
        _attn_kernel,
        grid=(batch,),
        in_specs=[pl.BlockSpec((seq, AT_W), lambda b: (b, 0))],
        out_specs=pl.BlockSpec((seq, D_GROUP), lambda b: (b, 0)),
        out_shape=jax.ShapeDtypeStruct((m, D_GROUP), F32),
        scratch_shapes=[pltpu.VMEM((seq, AT_W), F32)] + [pltpu.VMEM((seq, D_GROUP), F32)] * 3
        + [pltpu.VMEM((len(DILATIONS) * N_HEADS, ATTN_BLK, 2 * ATTN_BLK), F32)],
        compiler_params=pltpu.CompilerParams(dimension_semantics=("parallel",),
                                             vmem_limit_bytes=VMEM_LIMIT),
    )(at)


def _ssd_kernel(f_ref, fprev_ref, cw_ref, cb_ref, dtb_ref, vec_ref, ex_ref, y_ref, st_ref):
    c = pl.program_id(1)
    T = SSD_T
    G = D_GROUP
    N = SSD_STATE
    nb = SSD_NB
    R = nb * T
    seqs = range(nb)

    @pl.when(c == 0)
    def _():
        st_ref[...] = jnp.zeros_like(st_ref)

    f = f_ref[...].reshape(R, SD_W)
    xbc_raw = f[:, G:G + SSD_XBC]
    prev8 = [jnp.where(c == 0, 0.0, fprev_ref[b, :, G:G + SSD_XBC]) for b in seqs]
    row8 = _iota((8, 1), 0)
    conv = xbc_raw * cw_ref[3:4, :] + cb_ref[...]
    for j in range(1, 4):
        rolled = pltpu.roll(xbc_raw, j, axis=0)
        pieces = []
        for b in seqs:
            pieces.append(jnp.where(row8 < j, pltpu.roll(prev8[b], j, axis=0), rolled[b * T:b * T + 8]))
            pieces.append(rolled[b * T + 8:(b + 1) * T])
        conv = conv + jnp.concatenate(pieces, axis=0) * cw_ref[3 - j:4 - j, :]
    xbc = _silu(conv)
    xs = xbc[:, 0:G]
    bm = xbc[:, G:G + 2 * N]
    cm = xbc[:, G + 2 * N:G + 4 * N]
    z = f[:, 0:G]

    a_dense, d_dense, norm_w = (vec_ref[i:i + 1, :] for i in range(3))
    dt = _softplus(f[:, G + SSD_XBC:G + SSD_XBC + 128] + dtb_ref[...])
    dt_dense = _mmx_r(dt, ex_ref[...], pieces=2)
    da = dt_dense * a_dense
    xdt = xs * dt_dense
    ltri = (_iota((T, T), 1) <= _iota((T, T), 0))
    ltri_b = ltri.astype(BF16)
    umat = (_iota((T, T), 0) > _iota((T, T), 1)).astype(F32)
    masks = _head_masks()
    rows = [slice(b * T, (b + 1) * T) for b in seqs]

    cs = [_mmx_l(ltri_b, da[rows[b]], pieces=2) for b in seqs]
    scores = [[_mm(cm[rows[b], g * N:(g + 1) * N], bm[rows[b], g * N:(g + 1) * N], _NT)
               for g in range(2)] for b in seqs]
    y_diag = []
    for b in seqs:
        acc = None
        for h in range(N_HEADS):
            da_col = jnp.broadcast_to(da[rows[b], h * HEAD_DIM:h * HEAD_DIM + 1], (T, T))
            seg = _mmx_l(ltri_b, da_col * umat, pieces=2)
            dec = jnp.where(ltri, jnp.exp(jnp.where(ltri, seg, 0.0)), 0.0)
            term = _mm(scores[b][h // 2] * dec, xdt[rows[b]] * masks[h])
            acc = term if acc is None else acc + term
        y_diag.append(acc)

    ys = []
    for b in seqs:
        st = st_ref[b]
        cmb = cm[rows[b]]
        y_off = jnp.concatenate([_mm(cmb[:, 0:N], st[:, 0:N]), _mm(cmb[:, N:2 * N], st[:, N:2 * N])],
                                axis=1) * jnp.exp(cs[b])
        cs_last = cs[b][T - 1:T, :]
        xd = xdt[rows[b]] * jnp.exp(cs_last - cs[b])
        bmb = bm[rows[b]]
        st_ref[b] = st * jnp.exp(cs_last) + jnp.concatenate(
            [_mm(bmb[:, 0:N], xd[:, 0:N], _TN), _mm(bmb[:, N:2 * N], xd[:, N:2 * N], _TN)], axis=1)
        ys.append(y_diag[b] + y_off)

    y = (jnp.concatenate(ys, axis=0) + xs * d_dense) * _silu(z)
    halves = []
    for g in range(2):
        yg = y[:, g * N:(g + 1) * N]
        halves.append(yg * lax.rsqrt(jnp.mean(yg * yg, axis=-1, keepdims=True) + RMS_EPS))
    y_ref[...] = (jnp.concatenate(halves, axis=1) * norm_w).reshape(nb, T, G)


def _ssd(sd, cw, cb, dtb, vec, ex, batch, seq):
    nc = seq // SSD_T
    m = batch * seq
    const = lambda shape: pl.BlockSpec(shape, lambda b, c: (0, 0))
    sd3 = sd.reshape(batch, seq, SD_W)
    out = pl.pallas_call(
        _ssd_kernel,
        grid=(batch // SSD_NB, nc),
        in_specs=[pl.BlockSpec((SSD_NB, SSD_T, SD_W), lambda b, c: (b, c, 0)),
                  pl.BlockSpec((SSD_NB, 8, SD_W),
                               lambda b, c: (b, jnp.maximum(c * (SSD_T // 8) - 1, 0), 0)),
                  const((4, SSD_XBC)), const((1, SSD_XBC)), const((1, 128)), const((8, D_GROUP)),
                  const((128, D_GROUP))],
        out_specs=pl.BlockSpec((SSD_NB, SSD_T, D_GROUP), lambda b, c: (b, c, 0)),
        out_shape=jax.ShapeDtypeStruct((batch, seq, D_GROUP), F32),
        scratch_shapes=[pltpu.VMEM((SSD_NB, SSD_STATE, D_GROUP), F32)],
        compiler_params=pltpu.CompilerParams(dimension_semantics=("parallel", "arbitrary"),
                                             vmem_limit_bytes=VMEM_LIMIT),
    )(sd3, sd3, cw, cb, dtb, vec, ex)
    return out.reshape(m, D_GROUP)


def _hgrn_kernel(f_ref, vec_ref, y_ref, st_ref):
    c = pl.program_id(1)
    T = HG_T
    C = HG_C
    G = D_GROUP
    nblk = T // C

    @pl.when(c == 0)
    def _():
        st_ref[...] = jnp.zeros_like(st_ref)

    lb = vec_ref[0:1, :]
    norm_w = vec_ref[1:2, :]
    forget = lb + (1.0 - lb) * _sigmoid_tail(f_ref[:, G:2 * G])
    q = _silu(f_ref[:, 0:G])
    k = 1.0 - forget
    v = f_ref[:, 2 * G:3 * G]
    grp = 4 * C
    blk_tri = ((_iota((grp, grp), 0) // C == _iota((grp, grp), 1) // C)
               & (_iota((grp, grp), 1) <= _iota((grp, grp), 0))).astype(BF16)
    log_f = jnp.log(forget)
    b = jnp.concatenate([_mmx_l(blk_tri, log_f[r0:r0 + grp], pieces=2) for r0 in range(0, T, grp)], axis=0)
    b2 = b * math.log2(math.e)
    c2 = b2 - jnp.log2(k)

    head_sel = [(_iota((1, G), 1) // HEAD_DIM) == h for h in range(N_HEADS)]
    bo = _same_head().astype(BF16)
    t_idx = _iota((C, 1), 0)
    blocks = lambda z: [z[i * C:(i + 1) * C] for i in range(nblk)]
    bq, qq, kq, vv, b2q, c2q = blocks(b), blocks(q), blocks(k), blocks(v), blocks(b2), blocks(c2)
    b_last = [bq[i][C - 1:C, :] for i in range(nblk)]

    upd = [_dot(_stack_heads(vv[i], head_sel), _stack_heads(kq[i] * jnp.exp(b_last[i] - bq[i]), head_sel),
                _TN) for i in range(nblk)]
    q_in = [qq[i] * jnp.exp(bq[i]) for i in range(nblk)]
    half = C // 2
    o_intra = []
    for i in range(nblk):
        parts = []
        for s in range(C):
            lo = 0 if s < half else half
            e = jnp.exp2(b2q[i][lo:C] - c2q[i][s:s + 1, :])
            parts.append(jnp.where(t_idx[lo:C] >= s, qq[i][lo:C] * e, 0.0))
        z = _mm(jnp.concatenate(parts, axis=0), bo)
        o_lo = z[0:half] * vv[i][0:1, :]
        o_hi = z[half:C] * vv[i][0:1, :]
        for s in range(1, half):
            o_lo = o_lo + z[s * C:s * C + half] * vv[i][s:s + 1, :]
            o_hi = o_hi + z[s * C + half:(s + 1) * C] * vv[i][s:s + 1, :]
        for s in range(half, C):
            r0 = half * C + (s - half) * half
            o_hi = o_hi + z[r0:r0 + half] * vv[i][s:s + 1, :]
        o_intra.append(jnp.concatenate([o_lo, o_hi], axis=0))

    st = st_ref[...]
    outs = []
    for i in range(nblk):
        outs.append(_mm(q_in[i], st, _NT) + o_intra[i])
        st = st * jnp.exp(b_last[i]) + upd[i]
    st_ref[...] = st
    o = jnp.concatenate(outs, axis=0)
    ms = _mmx_r(o * o, bo, pieces=2) * (1.0 / HEAD_DIM)
    y_ref[...] = o * lax.rsqrt(ms + RMS_EPS) * norm_w * _silu(f_ref[:, 3 * G:4 * G])


def _hgrn(hg, vec, batch, seq):
    nc = seq // HG_T
    m = batch * seq
    return pl.pallas_call(
        _hgrn_kernel,
        grid=(batch, nc),
        in_specs=[pl.BlockSpec((HG_T, HG_W), lambda b, c: (b * nc + c, 0)),
                  pl.BlockSpec((8, D_GROUP), lambda b, c: (0, 0))],
        out_specs=pl.BlockSpec((HG_T, D_GROUP), lambda b, c: (b * nc + c, 0)),
        out_shape=jax.ShapeDtypeStruct((m, D_GROUP), F32),
        scratch_shapes=[pltpu.VMEM((D_GROUP, D_GROUP), F32)],
        compiler_params=pltpu.CompilerParams(dimension_semantics=("parallel", "arbitrary"),
                                             vmem_limit_bytes=VMEM_LIMIT),
    )(hg, vec)


def _post_kernel(x_ref, ya_ref, yb_ref, yc_ref, yd_ref, wo_ref, wu_ref, wd_ref, ln_ref, o_ref):
    x = x_ref[...]
    mix = jnp.zeros(x.shape, F32)
    for i, ref in enumerate((ya_ref, yb_ref, yc_ref, yd_ref)):
        mix = mix + jnp.dot(ref[...].astype(BF16), wo_ref[i * D_GROUP:(i + 1) * D_GROUP, :],
                            preferred_element_type=F32)
    x1 = _layer_norm(ALPHA * x + mix, ln_ref[0:1, :], ln_ref[1:2, :])
    xb = x1.astype(BF16)
    acc = ALPHA * x1
    for c0 in range(0, D_FF, FF_CHUNK):
        h = jnp.maximum(jnp.dot(xb, wu_ref[:, c0:c0 + FF_CHUNK], preferred_element_type=F32), 0.0)
        acc = acc + jnp.dot((h * h).astype(BF16), wd_ref[c0:c0 + FF_CHUNK, :],
                            preferred_element_type=F32)
    o_ref[...] = _layer_norm(acc, ln_ref[2:3, :], ln_ref[3:4, :])


def _post(x2d, ys, wo, wu, wd, ln):
    m = x2d.shape[0]
    row = lambda w: pl.BlockSpec((POST_TM, w), lambda i: (i, 0))
    res = lambda shape: pl.BlockSpec(shape, lambda i: (0, 0), pipeline_mode=pl.Buffered(1))
    return pl.pallas_call(
        _post_kernel,
        grid=(m // POST_TM,),
        in_specs=[row(D_MODEL)] + [row(D_GROUP)] * 4
        + [res((D_MODEL, D_MODEL)), res((D_MODEL, D_FF)), res((D_FF, D_MODEL)), res((8, D_MODEL))],
        out_specs=row(D_MODEL),
        out_shape=jax.ShapeDtypeStruct((m, D_MODEL), F32),
        compiler_params=pltpu.CompilerParams(dimension_semantics=("parallel",),
                                             vmem_limit_bytes=VMEM_LIMIT),
    )(x2d, *ys, wo, wu, wd, ln)


def _pad_cols(a, width):
    return jnp.pad(a, ((0, 0), (0, width - a.shape[1])))


def _pad_rows(a, top, total):
    return jnp.pad(a, ((top, total - top - a.shape[0]), (0, 0)))


def _rep_heads(v):
    return jnp.repeat(v.astype(F32), HEAD_DIM)[None, :]


def _rows8(rows, width):
    out = jnp.concatenate([r.reshape(1, width).astype(F32) for r in rows], axis=0)
    return jnp.pad(out, ((0, 8 - out.shape[0]), (0, 0)))


def kernel(x, lower_bounds, w_in, w_in_vres, mu_shift, mu_vres, rwkv_w0, rwkv_w2, rwkv_a0, rwkv_a2, rwkv_g2, rwkv_k_k, rwkv_k_a, rwkv_r_k, rwkv_lnx_w, rwkv_lnx_b, rwkv_v0, rwkv_v2, ssd_conv_w, ssd_conv_b, ssd_dt_bias, ssd_A_log, ssd_D, ssd_norm_w, hgrn_norm_w, w_out, ln1_w, ln1_b, w_up, w_down, ln2_w, ln2_b):
    batch, seq, _ = x.shape
    lb = jax.nn.softmax(lower_bounds.astype(F32), axis=0)
    lb = jnp.cumsum(lb, axis=0) - lb[0]
    expand = (jnp.arange(128)[:, None] == (jnp.arange(D_GROUP)[None, :] // HEAD_DIM)).astype(BF16)

    h = x.reshape(batch * seq, D_MODEL)
    v_first = None
    for l in range(DEPTH):
        mu = mu_shift[l][None, :]
        w_vres = None
        if l > 0:
            w_vres = _pad_cols(w_in_vres[l - 1], 128).astype(BF16)
            mu = jnp.concatenate([mu, mu_vres[l - 1][None, :]], axis=1)
        rw, at, sd, hg = _project(h, w_in, l, w_vres)

        vec_rows = [rwkv_w0[l], rwkv_a0[l], rwkv_k_k[l], rwkv_k_a[l], rwkv_lnx_w[l], rwkv_lnx_b[l],
                    rwkv_r_k[l].reshape(-1)]
        if l > 0:
            vec_rows.append(rwkv_v0[l - 1])
        y_a, v_first = _rwkv(
            rw, v_first, _pad_cols(mu, RW_W), _rows8(vec_rows, D_GROUP),
            _pad_rows(rwkv_w2[l], 0, 128), _pad_rows(rwkv_a2[l], 32, 128), _pad_rows(rwkv_g2[l], 64, 128),
            _pad_rows(rwkv_v2[l - 1], 0, 128) if l > 0 else None, batch, seq)
        y_b = _attention(at, batch, seq)
        ssd_vec = _rows8([_rep_heads(-jnp.exp(ssd_A_log[l].astype(F32))), _rep_heads(ssd_D[l]),
                          ssd_norm_w[l]], D_GROUP)
        y_c = _ssd(sd, ssd_conv_w[l].astype(F32), ssd_conv_b[l][None, :],
                   _pad_cols(ssd_dt_bias[l][None, :], 128), ssd_vec, expand, batch, seq)
        y_d = _hgrn(hg, _rows8([lb[l], hgrn_norm_w[l]], D_GROUP), batch, seq)
        ln = _rows8([ln1_w[l], ln1_b[l], ln2_w[l], ln2_b[l]], D_MODEL)
        h = _post(h, (y_a, y_b, y_c, y_d), w_out[l].astype(BF16), w_up[l].astype(BF16),
                  w_down[l].astype(BF16), ln)
    return h.reshape(batch, seq, D_MODEL)
```

```python
import functools
import math

import jax
import jax.numpy as jnp
from jax import lax
from jax.experimental import pallas as pl
from jax.experimental.pallas import tpu as pltpu

F32 = jnp.float32
BF16 = jnp.bfloat16

D_MODEL = 1024
DEPTH = 2
D_GROUP = 256
HEAD_DIM = 64
N_HEADS = 4
D_FF = 4 * D_MODEL
ALPHA = (2.0 * DEPTH) ** 0.25
LN_EPS = 1e-5
RMS_EPS = 1e-5
RWKV_GN_EPS = HEAD_DIM * 1e-5
RWKV_COLS = 896
ATTN_COLS = 768
SSD_STATE = 128
SSD_XBC = 768
SSD_COLS = 1028
HGRN_COLS = 1024
DILATIONS = (1, 4, 16)
ATTN_BLK = 128
ATTN_GROUP = 4
ALIBI_SLOPES = tuple(2.0 ** (-8.0 * (h + 1) / N_HEADS) for h in range(N_HEADS))

RW_W, AT_W, SD_W, HG_W = 1024, 768, 1152, 1024
PROJ_W = RW_W + AT_W + SD_W + HG_W

RW_T = 64
RW_NCH = 2
RW_NB = 8
RW_ROWS = RW_T * RW_NCH
SSD_T = 128
SSD_NB = 4
HG_C = 16
HG_T = 512
PROJ_TM = 512
POST_TM = 1024
FF_CHUNK = 512

VMEM_LIMIT = 56 * 1024 * 1024


def _dot(a, b, dims):
    return lax.dot_general(a, b, (dims, ((), ())), preferred_element_type=F32)


_NN = ((1,), (0,))
_NT = ((1,), (1,))
_TN = ((0,), (0,))


def _mm(a, b, dims=_NN):
    return _dot(a.astype(BF16), b.astype(BF16), dims)


def _split(x, pieces):
    out = []
    for i in range(pieces):
        hi = x.astype(BF16)
        out.append(hi)
        if i + 1 < pieces:
            x = x - hi.astype(F32)
    return out


def _mmx_l(m01, x, dims=_NN, pieces=3):
    return sum(_dot(m01, xp, dims) for xp in _split(x, pieces))


def _mmx_r(x, m01, dims=_NN, pieces=3):
    return sum(_dot(xp, m01, dims) for xp in _split(x, pieces))


def _mm3(a, b, dims=_NN):
    ah = a.astype(BF16)
    al = (a - ah.astype(F32)).astype(BF16)
    bh = b.astype(BF16)
    bl = (b - bh.astype(F32)).astype(BF16)
    return _dot(ah, bh, dims) + _dot(ah, bl, dims) + _dot(al, bh, dims)


def _iota(shape, dim):
    return lax.broadcasted_iota(jnp.int32, shape, dim)


def _head_masks(width=D_GROUP):
    lane = _iota((1, width), 1) // HEAD_DIM
    return [(lane == h).astype(F32) for h in range(N_HEADS)]


def _same_head(n=D_GROUP):
    return (_iota((n, n), 0) // HEAD_DIM) == (_iota((n, n), 1) // HEAD_DIM)


def _sigmoid(x):
    return 0.5 * jnp.tanh(0.5 * x) + 0.5


def _sigmoid_tail(x):
    return 1.0 / (1.0 + jnp.exp(-x))


def _silu(x):
    return x * _sigmoid(x)


def _softplus(x):
    return jnp.maximum(x, 0.0) + jnp.log(1.0 + jnp.exp(-jnp.abs(x)))


def _layer_norm(x, w, b):
    mu = jnp.mean(x, axis=-1, keepdims=True)
    d = x - mu
    var = jnp.mean(d * d, axis=-1, keepdims=True)
    return d * lax.rsqrt(var + LN_EPS) * w + b


def _shift_rows(x, prev_row):
    row = _iota((x.shape[0], 1), 0)
    return jnp.where(row == 0, prev_row, pltpu.roll(x, 1, axis=0))


def _proj_kernel(has_vres, *refs):
    if has_vres:
        x_ref, w_ref, wv_ref, rw_ref, at_ref, sd_ref, hg_ref = refs
    else:
        x_ref, w_ref, rw_ref, at_ref, sd_ref, hg_ref = refs
    xb = x_ref[...].astype(BF16)
    n_in = w_ref.shape[1]

    def emit(ref, c_out, c_in, width):
        wb = w_ref[:, c_in:c_in + width].astype(BF16)
        ref[:, c_out:c_out + width] = jnp.dot(xb, wb, preferred_element_type=F32)

    src = 0
    for ref, cols, padded in ((rw_ref, RWKV_COLS, RW_W), (at_ref, ATTN_COLS, AT_W),
                              (sd_ref, SSD_COLS, SD_W), (hg_ref, HGRN_COLS, HG_W)):
        for c0 in range(0, padded, 256):
            width = min(256, padded - c0)
            if c0 + width <= cols:
                emit(ref, c0, src + c0, width)
                continue
            real = max(cols - c0, 0)
            real_pad = -(-real // 128) * 128
            if has_vres and ref is rw_ref and real == real_pad and real_pad + wv_ref.shape[1] == width:
                wb = jnp.concatenate([w_ref[:, src + c0:src + c0 + real].astype(BF16), wv_ref[...]], axis=1)
                ref[:, c0:c0 + width] = jnp.dot(xb, wb, preferred_element_type=F32)
                continue
            if real:
                take = min(real_pad, n_in - (src + c0))
                wb = w_ref[:, src + c0:src + c0 + take].astype(BF16)
                lane = _iota((1, take), 1)
                wb = jnp.where(lane < real, wb, jnp.zeros_like(wb))
                ref[:, c0:c0 + take] = jnp.dot(xb, wb, preferred_element_type=F32)
                if take < real_pad:
                    ref[:, c0 + take:c0 + real_pad] = jnp.zeros((xb.shape[0], real_pad - take), F32)
            rest = width - real_pad
            if rest:
                if has_vres and ref is rw_ref:
                    ref[:, c0 + real_pad:c0 + width] = jnp.dot(xb, wv_ref[...], preferred_element_type=F32)
                else:
                    ref[:, c0 + real_pad:c0 + width] = jnp.zeros((xb.shape[0], rest), F32)
        src += cols


def _project(x2d, w_in, layer, w_vres):
    m = x2d.shape[0]
    n_in = w_in.shape[2]
    has_vres = w_vres is not None
    outs = [jax.ShapeDtypeStruct((m, w), F32) for w in (RW_W, AT_W, SD_W, HG_W)]
    in_specs = [pl.BlockSpec((PROJ_TM, D_MODEL), lambda i: (i, 0)),
                pl.BlockSpec((None, D_MODEL, n_in), lambda i: (layer, 0, 0), pipeline_mode=pl.Buffered(1))]
    args = [x2d, w_in]
    if has_vres:
        in_specs.append(pl.BlockSpec((D_MODEL, 128), lambda i: (0, 0)))
        args.append(w_vres)
    return pl.pallas_call(
        functools.partial(_proj_kernel, has_vres),
        grid=(m // PROJ_TM,),
        in_specs=in_specs,
        out_specs=[pl.BlockSpec((PROJ_TM, w), lambda i: (i, 0)) for w in (RW_W, AT_W, SD_W, HG_W)],
        out_shape=outs,
        compiler_params=pltpu.CompilerParams(dimension_semantics=("parallel",),
                                             vmem_limit_bytes=VMEM_LIMIT),
    )(*args)


def _stack_heads(x, head_sel):
    xb = x.astype(BF16)
    zero = jnp.zeros_like(xb)
    return jnp.concatenate([jnp.where(m, xb, zero) for m in head_sel], axis=0)


def _rwkv_chunks(at, bh, kh, rt, v, head_sel):
    n = len(at)
    rng = range(n)
    T = at[0].shape[0]
    G = D_GROUP
    row = _iota((T, G), 0)
    lane_t = _iota((T, G), 1) % T
    strict = lane_t < row
    incl = lane_t <= row
    st = lambda x: _stack_heads(x, head_sel)
    lhs = [jnp.concatenate([at[j], rt[j]], axis=0).astype(BF16) for j in rng]
    ab = [_dot(lhs[j], st(bh[j]), _NT) for j in rng]
    ak = [_dot(lhs[j], st(kh[j]), _NT) for j in rng]
    n_u = [jnp.where(strict, ab[j][0:T], 0.0) for j in rng]
    a_ak = [jnp.where(strict, ak[j][0:T], 0.0) for j in rng]
    m_rb = [jnp.where(incl, ab[j][T:2 * T], 0.0).astype(BF16) for j in rng]
    m_rk = [jnp.where(incl, ak[j][T:2 * T], 0.0).astype(BF16) for j in rng]

    eye = jnp.where(lane_t == row, 1.0, 0.0)
    x_u = [eye + n_u[j] for j in rng]
    pw = [_dot(n_u[j].astype(BF16), st(n_u[j]), _NN) for j in rng]
    steps = T.bit_length() - 2
    for it in range(steps):
        w = [st(pw[j]) for j in rng]
        if it + 1 < steps:
            res = [_dot(jnp.concatenate([pw[j], x_u[j]], axis=0).astype(BF16), w[j], _NN) for j in rng]
            pw = [res[j][0:T] for j in rng]
            x_u = [x_u[j] + res[j][T:2 * T] for j in rng]
        else:
            x_u = [x_u[j] + _dot(x_u[j].astype(BF16), w[j], _NN) for j in rng]

    v_s = [st(v[j]) for j in rng]
    akv = [_dot(a_ak[j].astype(BF16), v_s[j], _NN) for j in rng]
    xb = [x_u[j].astype(BF16) for j in rng]
    p_u = [_dot(xb[j], st(at[j]), _NN) for j in rng]
    q_u = [_dot(xb[j], st(akv[j]), _NN) for j in rng]
    mkv = [_dot(m_rk[j], v_s[j], _NN) for j in rng]
    return p_u, q_u, m_rb, mkv


def _rwkv_kernel(has_vres, *refs):
    if has_vres:
        (f_ref, fprev_ref, vf_ref, mu_ref, vec_ref, w2_ref, a2_ref, g2_ref, v2_ref,
         y_ref, s_ref) = refs
    else:
        (f_ref, fprev_ref, mu_ref, vec_ref, w2_ref, a2_ref, g2_ref,
         y_ref, vout_ref, s_ref) = refs
    c = pl.program_id(1)
    T = RW_T
    G = D_GROUP

    @pl.when(c == 0)
    def _():
        s_ref[...] = jnp.zeros_like(s_ref)

    w0, a0, k_k, k_a = (vec_ref[i:i + 1, :] for i in range(4))
    lnx_w, lnx_b, r_k = (vec_ref[i:i + 1, :] for i in range(4, 7))
    head_sel = [(_iota((1, G), 1) // HEAD_DIM) == h for h in range(N_HEADS)]
    same = _same_head()
    bo = same.astype(BF16)
    ltri = (_iota((T, T), 1) <= _iota((T, T), 0)).astype(BF16)

    def prologue(b0, nb):
        R = RW_ROWS * nb
        nchunks = RW_NCH * nb
        f = f_ref[b0:b0 + nb].reshape(R, RW_W)
        row = _iota((R, 1), 0)
        shifted = pltpu.roll(f, 1, axis=0)
        for b in range(nb):
            prev_row = jnp.where(c == 0, 0.0, fprev_ref[b0 + b, 7:8, :])
            shifted = jnp.where(row == b * RW_ROWS, prev_row, shifted)
        x = f + (shifted - f) * mu_ref[...]
        r = x[:, 0:G]
        k = x[:, G:2 * G]
        v = x[:, 2 * G:3 * G]
        seg = x[:, 3 * G:3 * G + 128]
        p = w0 + _mm3(jnp.tanh(seg), w2_ref[...])
        lw = -math.exp(-0.5) * _sigmoid(p)
        a = _sigmoid(a0 + _mm(seg, a2_ref[...]))
        g = _mm(_sigmoid(seg), g2_ref[...])
        if has_vres:
            seg2 = x[:, 3 * G + 128:3 * G + 256]
            v0 = vec_ref[7:8, :]
            v = v + (vf_ref[b0:b0 + nb].reshape(R, G) - v) * _sigmoid(v0 + _mm(seg2, v2_ref[...]))
        else:
            vout_ref[b0:b0 + nb] = v.reshape(nb, RW_ROWS, G)
        kk = k * k_k
        kk = kk * lax.rsqrt(jnp.maximum(_mm(kk * kk, bo), 1e-24))
        k2 = k * (1.0 + (a - 1.0) * k_a)
        kka = kk * a
        cs = jnp.concatenate([_mmx_l(ltri, lw[j * T:(j + 1) * T], pieces=2) for j in range(nchunks)],
                             axis=0)
        c_last = jnp.concatenate(
            [jnp.broadcast_to(cs[(j + 1) * T - 1:(j + 1) * T, :], (T, G)) for j in range(nchunks)], axis=0)
        e_neg = jnp.exp(-cs)
        e_dec = jnp.exp(c_last - cs)
        chunks = lambda z: [z[j * T:(j + 1) * T] for j in range(nchunks)]
        return dict(b0=b0, nb=nb, nchunks=nchunks, r=r, k2=k2, v=v, g=g, c_last=c_last,
                    at=chunks(-kk * jnp.exp(cs - lw)), bh=chunks(kka * e_neg), kh=chunks(k2 * e_neg),
                    rt=chunks(r * jnp.exp(cs)), bdec=chunks(kka * e_dec), kdec=chunks(k2 * e_dec),
                    vc=chunks(v))

    def middle(ctx):
        n = ctx["nchunks"]
        p_u, q_u, m_rb, mkv = _rwkv_chunks(ctx["at"], ctx["bh"], ctx["kh"], ctx["rt"], ctx["vc"], head_sel)
        ctx.update(q_u=q_u, m_rb=m_rb, mkv=mkv)
        ctx["rp_lhs"] = [jnp.concatenate([ctx["rt"][i], p_u[i]], axis=0).astype(BF16) for i in range(n)]
        ctx["dec_t"] = [jnp.concatenate([ctx["bdec"][i], ctx["kdec"][i]], axis=0).T.astype(BF16)
                        for i in range(n)]
        ctx["w_col"] = [jnp.exp(jnp.broadcast_to(ctx["c_last"][i * T:i * T + 1, :], (128, G))).T[:, 0:1]
                        for i in range(n)]

    def chain(ctx):
        b0, nb, n = ctx["b0"], ctx["nb"], ctx["nchunks"]
        s = [s_ref[b0 + b] for b in range(nb)]
        rs = [None] * n
        sas = [None] * n
        seqs = range(nb)
        for j in range(RW_NCH):
            ids = [b * RW_NCH + j for b in seqs]
            rp = [_dot(ctx["rp_lhs"][i], s[b].astype(BF16), _NN) for b, i in zip(seqs, ids)]
            for b, i in zip(seqs, ids):
                rs[i] = rp[b][0:T]
                sas[i] = rp[b][T:2 * T] + ctx["q_u"][i]
            upd = [_dot(ctx["dec_t"][i], jnp.concatenate([sas[i], ctx["vc"][i]], axis=0).astype(BF16), _NN)
                   for i in ids]
            s = [s[b] * ctx["w_col"][i] + jnp.where(same, upd[b], 0.0) for b, i in zip(seqs, ids)]
        outs = [rs[i] + _dot(ctx["m_rb"][i], _stack_heads(sas[i], head_sel), _NN) + ctx["mkv"][i]
                for i in range(n)]
        for b in seqs:
            s_ref[b0 + b] = s[b]
        ctx["o"] = jnp.concatenate(outs, axis=0)

    def epilogue(ctx):
        o = ctx["o"]
        mean = _mmx_r(o, bo, pieces=2) * (1.0 / HEAD_DIM)
        d = o - mean
        var = _mm(d * d, bo) * (1.0 / HEAD_DIM)
        yn = d * lax.rsqrt(var + RWKV_GN_EPS) * lnx_w + lnx_b
        bonus = _mm(ctx["r"] * ctx["k2"] * r_k, bo) * ctx["v"]
        y_ref[ctx["b0"]:ctx["b0"] + ctx["nb"]] = ((yn + bonus) * ctx["g"]).reshape(ctx["nb"], RW_ROWS, G)

    ctx = prologue(0, RW_NB)
    middle(ctx)
    chain(ctx)
    epilogue(ctx)


def _rwkv(rw, v_first, mu, vec, w2p, a2p, g2p, v2p, batch, seq):
    has_vres = v_first is not None
    nc = seq // RW_ROWS
    m = batch * seq
    row_spec = lambda w: pl.BlockSpec((RW_NB, RW_ROWS, w), lambda b, c: (b, c, 0))
    const = lambda shape: pl.BlockSpec(shape, lambda b, c: (0, 0))
    prev_spec = pl.BlockSpec(
        (RW_NB, 8, RW_W), lambda b, c: (b, jnp.maximum(c * (RW_ROWS // 8) - 1, 0), 0))
    rw3 = rw.reshape(batch, seq, RW_W)
    in_specs = [row_spec(RW_W), prev_spec]
    args = [rw3, rw3]
    if has_vres:
        in_specs.append(row_spec(D_GROUP))
        args.append(v_first.reshape(batch, seq, D_GROUP))
    in_specs += [const((1, RW_W)), const((8, D_GROUP)), const((128, D_GROUP)),
                 const((128, D_GROUP)), const((128, D_GROUP))]
    args += [mu, vec, w2p, a2p, g2p]
    if has_vres:
        in_specs.append(const((128, D_GROUP)))
        args.append(v2p)
    out_shape = [jax.ShapeDtypeStruct((batch, seq, D_GROUP), F32)]
    out_specs = [row_spec(D_GROUP)]
    if not has_vres:
        out_shape.append(jax.ShapeDtypeStruct((batch, seq, D_GROUP), F32))
        out_specs.append(row_spec(D_GROUP))
    outs = pl.pallas_call(
        functools.partial(_rwkv_kernel, has_vres),
        grid=(batch // RW_NB, nc),
        in_specs=in_specs,
        out_specs=out_specs,
        out_shape=out_shape,
        scratch_shapes=[pltpu.VMEM((RW_NB, D_GROUP, D_GROUP), F32)],
        compiler_params=pltpu.CompilerParams(dimension_semantics=("parallel", "arbitrary"),
                                             vmem_limit_bytes=VMEM_LIMIT),
    )(*args)
    y = outs[0].reshape(m, D_GROUP)
    if has_vres:
        return y, v_first
    return y, outs[1].reshape(m, D_GROUP)


def _attn_kernel(qkv_ref, o_ref, perm_ref, acc_ref, l_ref, m_ref, bias_ref):
    G = D_GROUP
    blk = ATTN_BLK
    seq = qkv_ref.shape[0]
    nres = DILATIONS[-1]
    sub = seq // nres
    head_sel = [(_iota((1, G), 1) // HEAD_DIM) == h for h in range(N_HEADS)]
    scale = HEAD_DIM ** -0.5 * math.log2(math.e)

    def jmap(a, d):
        run = blk * d // nres
        return (nres // d) * (a % run) + a // run

    to_perm = (_iota((blk, blk), 1) == jmap(_iota((blk, blk), 0), 1)).astype(BF16)
    from_perm = (_iota((blk, blk), 0) == jmap(_iota((blk, blk), 1), 1)).astype(BF16)
    run1 = blk // nres
    col_scale = jnp.where(_iota((1, AT_W), 1) < G, scale, 1.0)

    def permute(n, carry):
        r0 = pl.multiple_of(n * blk, blk)
        pb = jnp.dot(to_perm, (qkv_ref[pl.ds(r0, blk), :] * col_scale).astype(BF16),
                     preferred_element_type=F32)
        for e in range(nres):
            dst = pl.multiple_of(e * sub + n * run1, run1)
            perm_ref[pl.ds(dst, run1), :] = pb[e * run1:(e + 1) * run1, :]
        return carry

    lax.fori_loop(0, seq // blk, permute, 0, unroll=4)

    branches = tuple(reversed(DILATIONS))
    for bi, d in enumerate(branches):
        jq = jmap(_iota((blk, 2 * blk), 0), d)
        kb = _iota((blk, 2 * blk), 1)
        dist = blk + jq - (jmap(kb % blk, d) + (kb // blk) * blk)
        in_band = (dist >= 0) & (dist <= blk)
        for h in range(N_HEADS):
            bias_ref[bi * N_HEADS + h] = jnp.where(
                in_band, dist.astype(F32) * (-ALIBI_SLOPES[h] * d * math.log2(math.e)), -jnp.inf)

    for bi, d in enumerate(branches):
        runs = nres // d
        run = blk // runs
        nb = seq // (d * blk)
        first = bi == 0
        last = bi == len(branches) - 1

        def blocks(items, with_prev, bi=bi, d=d, runs=runs, run=run, first=first, last=last):
            its = range(len(items))
            heads = range(N_HEADS)

            def starts(rho, nn):
                return [pl.multiple_of((rho + d * e) * sub + nn * run, run) for e in range(runs)]

            def gather(st, c0):
                return jnp.concatenate([perm_ref[pl.ds(s0, run), c0:c0 + G] for s0 in st],
                                       axis=0).astype(BF16)

            cur = [starts(rho, n) for rho, n in items]
            qb = [gather(cur[i], 0) for i in its]
            if with_prev:
                prv = [starts(rho, n - 1) for rho, n in items]
                kcat = [jnp.concatenate([gather(prv[i], G), gather(cur[i], G)], axis=0) for i in its]
                vcat = [jnp.concatenate([gather(prv[i], 2 * G), gather(cur[i], 2 * G)], axis=0)
                        for i in its]
            else:
                kcat = [gather(cur[i], G) for i in its]
                vcat = [gather(cur[i], 2 * G) for i in its]
            zq = jnp.zeros_like(qb[0])
            kc0 = 0 if with_prev else blk
            pairs = [(0, 1), (2, 3)]
            s_p = [[_dot(jnp.concatenate([jnp.where(head_sel[h], qb[i], zq) for h in pr], axis=0),
                         kcat[i], _NT) for pr in pairs] for i in its]
            s = [[s_p[i][h // 2][(h % 2) * blk:(h % 2 + 1) * blk]
                  + bias_ref[bi * N_HEADS + h, :, kc0:2 * blk] for h in heads] for i in its]
            mh = [[jnp.max(s[i][h], axis=-1, keepdims=True) for h in heads] for i in its]
            ph = [[jnp.exp2(s[i][h] - mh[i][h]) for h in heads] for i in its]
            lh = [[jnp.sum(ph[i][h], axis=-1, keepdims=True) for h in heads] for i in its]
            pv_p = [[_dot(jnp.concatenate([ph[i][h].astype(BF16) for h in pr], axis=0), vcat[i], _NN)
                     for pr in pairs] for i in its]
            def by_head(parts):
                out = jnp.broadcast_to(parts[N_HEADS - 1], (blk, G))
                for h in reversed(range(N_HEADS - 1)):
                    out = jnp.where(head_sel[h], parts[h], out)
                return out

            acc_b = [by_head([pv_p[i][h // 2][(h % 2) * blk:(h % 2 + 1) * blk] for h in heads])
                     for i in its]
            l_b = [by_head(lh[i]) for i in its]
            m_b = [by_head(mh[i]) for i in its]
            for i in its:
                outs = []
                for e, s0 in enumerate(cur[i]):
                    idx = pl.ds(s0, run)
                    sl = slice(e * run, (e + 1) * run)
                    if first:
                        acc_ref[idx, :] = acc_b[i][sl]
                        l_ref[idx, :] = l_b[i][sl]
                        m_ref[idx, :] = m_b[i][sl]
                    else:
                        m_old = m_ref[idx, :]
                        m_new = jnp.maximum(m_old, m_b[i][sl])
                        w_old = jnp.exp2(m_old - m_new)
                        w_new = jnp.exp2(m_b[i][sl] - m_new)
                        acc_n = acc_ref[idx, :] * w_old + acc_b[i][sl] * w_new
                        l_n = l_ref[idx, :] * w_old + l_b[i][sl] * w_new
                        if last:
                            outs.append(acc_n / l_n)
                        else:
                            acc_ref[idx, :] = acc_n
                            l_ref[idx, :] = l_n
                            m_ref[idx, :] = m_new
                if last:
                    r0 = pl.multiple_of(items[i][1] * blk, blk)
                    o_ref[pl.ds(r0, blk), :] = _mmx_l(from_perm, jnp.concatenate(outs, axis=0))

        def run_blocks(count, index, with_prev, blocks=blocks):
            grp = ATTN_GROUP - 1 if with_prev else ATTN_GROUP

            def group(it, carry):
                blocks([index(grp * it + u) for u in range(grp)], with_prev)
                return carry

            if count >= grp:
                lax.fori_loop(0, count // grp, group, 0)
            rest = count % grp
            if rest:
                blocks([index(count - rest + u) for u in range(rest)], with_prev)

        run_blocks(d, lambda i: (i, 0), False)
        if nb > 1:
            run_blocks(d * (nb - 1), lambda i, nb=nb: (i // (nb - 1), i % (nb - 1) + 1), True)


def _attention(at, batch, seq):
    m = batch * seq
    return pl.pallas_call(
        _attn_kernel,
        grid=(batch,),
        in_specs=[pl.BlockSpec((seq, AT_W), lambda b: (b, 0))],
        out_specs=pl.BlockSpec((seq, D_GROUP), lambda b: (b, 0)),
        out_shape=jax.ShapeDtypeStruct((m, D_GROUP), F32),
        scratch_shapes=[pltpu.VMEM((seq, AT_W), F32)] + [pltpu.VMEM((seq, D_GROUP), F32)] * 3
        + [pltpu.VMEM((len(DILATIONS) * N_HEADS, ATTN_BLK, 2 * ATTN_BLK), F32)],
        compiler_params=pltpu.CompilerParams(dimension_semantics=("parallel",),
                                             vmem_limit_bytes=VMEM_LIMIT),
    )(at)


def _ssd_kernel(f_ref, fprev_ref, cw_ref, cb_ref, dtb_ref, vec_ref, ex_ref, y_ref, st_ref):
    c = pl.program_id(1)
    T = SSD_T
    G = D_GROUP
    N = SSD_STATE
    nb = SSD_NB
    R = nb * T
    seqs = range(nb)

    @pl.when(c == 0)
    def _():
        st_ref[...] = jnp.zeros_like(st_ref)

    f = f_ref[...].reshape(R, SD_W)
    xbc_raw = f[:, G:G + SSD_XBC]
    prev8 = [jnp.where(c == 0, 0.0, fprev_ref[b, :, G:G + SSD_XBC]) for b in seqs]
    row8 = _iota((8, 1), 0)
    conv = xbc_raw * cw_ref[3:4, :] + cb_ref[...]
    for j in range(1, 4):
        rolled = pltpu.roll(xbc_raw, j, axis=0)
        pieces = []
        for b in seqs:
            pieces.append(jnp.where(row8 < j, pltpu.roll(prev8[b], j, axis=0), rolled[b * T:b * T + 8]))
            pieces.append(rolled[b * T + 8:(b + 1) * T])
        conv = conv + jnp.concatenate(pieces, axis=0) * cw_ref[3 - j:4 - j, :]
    xbc = _silu(conv)
    xs = xbc[:, 0:G]
    bm = xbc[:, G:G + 2 * N]
    cm = xbc[:, G + 2 * N:G + 4 * N]
    z = f[:, 0:G]

    a_dense, d_dense, norm_w = (vec_ref[i:i + 1, :] for i in range(3))
    dt = _softplus(f[:, G + SSD_XBC:G + SSD_XBC + 128] + dtb_ref[...])
    dt_dense = _mmx_r(dt, ex_ref[...], pieces=2)
    da = dt_dense * a_dense
    xdt = xs * dt_dense
    ltri = (_iota((T, T), 1) <= _iota((T, T), 0))
    ltri_b = ltri.astype(BF16)
    umat = (_iota((T, T), 0) > _iota((T, T), 1)).astype(F32)
    masks = _head_masks()
    rows = [slice(b * T, (b + 1) * T) for b in seqs]

    cs = [_mmx_l(ltri_b, da[rows[b]], pieces=2) for b in seqs]
    scores = [[_mm(cm[rows[b], g * N:(g + 1) * N], bm[rows[b], g * N:(g + 1) * N], _NT)
               for g in range(2)] for b in seqs]
    y_diag = []
    for b in seqs:
        acc = None
        for h in range(N_HEADS):
            da_col = jnp.broadcast_to(da[rows[b], h * HEAD_DIM:h * HEAD_DIM + 1], (T, T))
            seg = _mmx_l(ltri_b, da_col * umat, pieces=2)
            dec = jnp.where(ltri, jnp.exp(jnp.where(ltri, seg, 0.0)), 0.0)
            term = _mm(scores[b][h // 2] * dec, xdt[rows[b]] * masks[h])
            acc = term if acc is None else acc + term
        y_diag.append(acc)

    ys = []
    for b in seqs:
        st = st_ref[b]
        cmb = cm[rows[b]]
        y_off = jnp.concatenate([_mm(cmb[:, 0:N], st[:, 0:N]), _mm(cmb[:, N:2 * N], st[:, N:2 * N])],
                                axis=1) * jnp.exp(cs[b])
        cs_last = cs[b][T - 1:T, :]
        xd = xdt[rows[b]] * jnp.exp(cs_last - cs[b])
        bmb = bm[rows[b]]
        st_ref[b] = st * jnp.exp(cs_last) + jnp.concatenate(
            [_mm(bmb[:, 0:N], xd[:, 0:N], _TN), _mm(bmb[:, N:2 * N], xd[:, N:2 * N], _TN)], axis=1)
        ys.append(y_diag[b] + y_off)

    y = (jnp.concatenate(ys, axis=0) + xs * d_dense) * _silu(z)
    halves = []
    for g in range(2):
        yg = y[:, g * N:(g + 1) * N]
        halves.append(yg * lax.rsqrt(jnp.mean(yg * yg, axis=-1, keepdims=True) + RMS_EPS))
    y_ref[...] = (jnp.concatenate(halves, axis=1) * norm_w).reshape(nb, T, G)


def _ssd(sd, cw, cb, dtb, vec, ex, batch, seq):
    nc = seq // SSD_T
    m = batch * seq
    const = lambda shape: pl.BlockSpec(shape, lambda b, c: (0, 0))
    sd3 = sd.reshape(batch, seq, SD_W)
    out = pl.pallas_call(
        _ssd_kernel,
        grid=(batch // SSD_NB, nc),
        in_specs=[pl.BlockSpec((SSD_NB, SSD_T, SD_W), lambda b, c: (b, c, 0)),
                  pl.BlockSpec((SSD_NB, 8, SD_W),
                               lambda b, c: (b, jnp.maximum(c * (SSD_T // 8) - 1, 0), 0)),
                  const((4, SSD_XBC)), const((1, SSD_XBC)), const((1, 128)), const((8, D_GROUP)),
                  const((128, D_GROUP))],
        out_specs=pl.BlockSpec((SSD_NB, SSD_T, D_GROUP), lambda b, c: (b, c, 0)),
        out_shape=jax.ShapeDtypeStruct((batch, seq, D_GROUP), F32),
        scratch_shapes=[pltpu.VMEM((SSD_NB, SSD_STATE, D_GROUP), F32)],
        compiler_params=pltpu.CompilerParams(dimension_semantics=("parallel", "arbitrary"),
                                             vmem_limit_bytes=VMEM_LIMIT),
    )(sd3, sd3, cw, cb, dtb, vec, ex)
    return out.reshape(m, D_GROUP)


def _hgrn_kernel(f_ref, vec_ref, y_ref, st_ref):
    c = pl.program_id(1)
    T = HG_T
    C = HG_C
    G = D_GROUP
    nblk = T // C

    @pl.when(c == 0)
    def _():
        st_ref[...] = jnp.zeros_like(st_ref)

    lb = vec_ref[0:1, :]
    norm_w = vec_ref[1:2, :]
    forget = lb + (1.0 - lb) * _sigmoid_tail(f_ref[:, G:2 * G])
    q = _silu(f_ref[:, 0:G])
    k = 1.0 - forget
    v = f_ref[:, 2 * G:3 * G]
    grp = 4 * C
    blk_tri = ((_iota((grp, grp), 0) // C == _iota((grp, grp), 1) // C)
               & (_iota((grp, grp), 1) <= _iota((grp, grp), 0))).astype(BF16)
    log_f = jnp.log(forget)
    b = jnp.concatenate([_mmx_l(blk_tri, log_f[r0:r0 + grp], pieces=2) for r0 in range(0, T, grp)], axis=0)
    b2 = b * math.log2(math.e)
    c2 = b2 - jnp.log2(k)

    head_sel = [(_iota((1, G), 1) // HEAD_DIM) == h for h in range(N_HEADS)]
    bo = _same_head().astype(BF16)
    t_idx = _iota((C, 1), 0)
    blocks = lambda z: [z[i * C:(i + 1) * C] for i in range(nblk)]
    bq, qq, kq, vv, b2q, c2q = blocks(b), blocks(q), blocks(k), blocks(v), blocks(b2), blocks(c2)
    b_last = [bq[i][C - 1:C, :] for i in range(nblk)]

    upd = [_dot(_stack_heads(vv[i], head_sel), _stack_heads(kq[i] * jnp.exp(b_last[i] - bq[i]), head_sel),
                _TN) for i in range(nblk)]
    q_in = [qq[i] * jnp.exp(bq[i]) for i in range(nblk)]
    half = C // 2
    o_intra = []
    for i in range(nblk):
        parts = []
        for s in range(C):
            lo = 0 if s < half else half
            e = jnp.exp2(b2q[i][lo:C] - c2q[i][s:s + 1, :])
            parts.append(jnp.where(t_idx[lo:C] >= s, qq[i][lo:C] * e, 0.0))
        z = _mm(jnp.concatenate(parts, axis=0), bo)
        o_lo = z[0:half] * vv[i][0:1, :]
        o_hi = z[half:C] * vv[i][0:1, :]
        for s in range(1, half):
            o_lo = o_lo + z[s * C:s * C + half] * vv[i][s:s + 1, :]
            o_hi = o_hi + z[s * C + half:(s + 1) * C] * vv[i][s:s + 1, :]
        for s in range(half, C):
            r0 = half * C + (s - half) * half
            o_hi = o_hi + z[r0:r0 + half] * vv[i][s:s + 1, :]
        o_intra.append(jnp.concatenate([o_lo, o_hi], axis=0))

    st = st_ref[...]
    outs = []
    for i in range(nblk):
        outs.append(_mm(q_in[i], st, _NT) + o_intra[i])
        st = st * jnp.exp(b_last[i]) + upd[i]
    st_ref[...] = st
    o = jnp.concatenate(outs, axis=0)
    ms = _mmx_r(o * o, bo, pieces=2) * (1.0 / HEAD_DIM)
    y_ref[...] = o * lax.rsqrt(ms + RMS_EPS) * norm_w * _silu(f_ref[:, 3 * G:4 * G])


def _hgrn(hg, vec, batch, seq):
    nc = seq // HG_T
    m = batch * seq
    return pl.pallas_call(
        _hgrn_kernel,
        grid=(batch, nc),
        in_specs=[pl.BlockSpec((HG_T, HG_W), lambda b, c: (b * nc + c, 0)),
                  pl.BlockSpec((8, D_GROUP), lambda b, c: (0, 0))],
        out_specs=pl.BlockSpec((HG_T, D_GROUP), lambda b, c: (b * nc + c, 0)),
        out_shape=jax.ShapeDtypeStruct((m, D_GROUP), F32),
        scratch_shapes=[pltpu.VMEM((D_GROUP, D_GROUP), F32)],
        compiler_params=pltpu.CompilerParams(dimension_semantics=("parallel", "arbitrary"),
                                             vmem_limit_bytes=VMEM_LIMIT),
    )(hg, vec)


def _post_kernel(x_ref, ya_ref, yb_ref, yc_ref, yd_ref, wo_ref, wu_ref, wd_ref, ln_ref, o_ref):
    half = POST_TM // 2
    rows = [slice(0, half), slice(half, POST_TM)]

    def mixed(r):
        acc = ALPHA * x_ref[r, :]
        for i, ref in enumerate((ya_ref, yb_ref, yc_ref, yd_ref)):
            acc = acc + jnp.dot(ref[r, :].astype(BF16), wo_ref[i * D_GROUP:(i + 1) * D_GROUP, :],
                                preferred_element_type=F32)
        return acc

    def ffn_chunk(xb, acc, c0):
        h = jnp.maximum(jnp.dot(xb, wu_ref[:, c0:c0 + FF_CHUNK], preferred_element_type=F32), 0.0)
        return acc + jnp.dot((h * h).astype(BF16), wd_ref[c0:c0 + FF_CHUNK, :],
                             preferred_element_type=F32)

    chunks = list(range(0, D_FF, FF_CHUNK))
    pre_a = mixed(rows[0])
    x1a = _layer_norm(pre_a, ln_ref[0:1, :], ln_ref[1:2, :])
    pre_b = mixed(rows[1])
    xba = x1a.astype(BF16)
    acc_a = ALPHA * x1a
    for c0 in chunks[:2]:
        acc_a = ffn_chunk(xba, acc_a, c0)
    x1b = _layer_norm(pre_b, ln_ref[0:1, :], ln_ref[1:2, :])
    for c0 in chunks[2:]:
        acc_a = ffn_chunk(xba, acc_a, c0)
    xbb = x1b.astype(BF16)
    acc_b = ALPHA * x1b
    for c0 in chunks[:2]:
        acc_b = ffn_chunk(xbb, acc_b, c0)
    o_ref[rows[0], :] = _layer_norm(acc_a, ln_ref[2:3, :], ln_ref[3:4, :])
    for c0 in chunks[2:]:
        acc_b = ffn_chunk(xbb, acc_b, c0)
    o_ref[rows[1], :] = _layer_norm(acc_b, ln_ref[2:3, :], ln_ref[3:4, :])


def _post(x2d, ys, wo, wu, wd, ln):
    m = x2d.shape[0]
    row = lambda w: pl.BlockSpec((POST_TM, w), lambda i: (i, 0))
    res = lambda shape: pl.BlockSpec(shape, lambda i: (0, 0), pipeline_mode=pl.Buffered(1))
    return pl.pallas_call(
        _post_kernel,
        grid=(m // POST_TM,),
        in_specs=[row(D_MODEL)] + [row(D_GROUP)] * 4
        + [res((D_MODEL, D_MODEL)), res((D_MODEL, D_FF)), res((D_FF, D_MODEL)), res((8, D_MODEL))],
        out_specs=row(D_MODEL),
        out_shape=jax.ShapeDtypeStruct((m, D_MODEL), F32),
        compiler_params=pltpu.CompilerParams(dimension_semantics=("parallel",),
                                             vmem_limit_bytes=VMEM_LIMIT),
    )(x2d, *ys, wo, wu, wd, ln)


def _pad_cols(a, width):
    return jnp.pad(a, ((0, 0), (0, width - a.shape[1])))


def _pad_rows(a, top, total):
    return jnp.pad(a, ((top, total - top - a.shape[0]), (0, 0)))


def _rep_heads(v):
    return jnp.repeat(v.astype(F32), HEAD_DIM)[None, :]


def _rows8(rows, width):
    out = jnp.concatenate([r.reshape(1, width).astype(F32) for r in rows], axis=0)
    return jnp.pad(out, ((0, 8 - out.shape[0]), (0, 0)))


def kernel(x, lower_bounds, w_in, w_in_vres, mu_shift, mu_vres, rwkv_w0, rwkv_w2, rwkv_a0, rwkv_a2, rwkv_g2, rwkv_k_k, rwkv_k_a, rwkv_r_k, rwkv_lnx_w, rwkv_lnx_b, rwkv_v0, rwkv_v2, ssd_conv_w, ssd_conv_b, ssd_dt_bias, ssd_A_log, ssd_D, ssd_norm_w, hgrn_norm_w, w_out, ln1_w, ln1_b, w_up, w_down, ln2_w, ln2_b):
    batch, seq, _ = x.shape
    lb = jax.nn.softmax(lower_bounds.astype(F32), axis=0)
    lb = jnp.cumsum(lb, axis=0) - lb[0]
    expand = (jnp.arange(128)[:, None] == (jnp.arange(D_GROUP)[None, :] // HEAD_DIM)).astype(BF16)

    h = x.reshape(batch * seq, D_MODEL)
    v_first = None
    for l in range(DEPTH):
        mu = mu_shift[l][None, :]
        w_vres = None
        if l > 0:
            w_vres = _pad_cols(w_in_vres[l - 1], 128).astype(BF16)
            mu = jnp.concatenate([mu, mu_vres[l - 1][None, :]], axis=1)
        rw, at, sd, hg = _project(h, w_in, l, w_vres)

        vec_rows = [rwkv_w0[l], rwkv_a0[l], rwkv_k_k[l], rwkv_k_a[l], rwkv_lnx_w[l], rwkv_lnx_b[l],
                    rwkv_r_k[l].reshape(-1)]
        if l > 0:
            vec_rows.append(rwkv_v0[l - 1])
        y_a, v_first = _rwkv(
            rw, v_first, _pad_cols(mu, RW_W), _rows8(vec_rows, D_GROUP),
            _pad_rows(rwkv_w2[l], 0, 128), _pad_rows(rwkv_a2[l], 32, 128), _pad_rows(rwkv_g2[l], 64, 128),
            _pad_rows(rwkv_v2[l - 1], 0, 128) if l > 0 else None, batch, seq)
        y_b = _attention(at, batch, seq)
        ssd_vec = _rows8([_rep_heads(-jnp.exp(ssd_A_log[l].astype(F32))), _rep_heads(ssd_D[l]),
                          ssd_norm_w[l]], D_GROUP)
        y_c = _ssd(sd, ssd_conv_w[l].astype(F32), ssd_conv_b[l][None, :],
                   _pad_cols(ssd_dt_bias[l][None, :], 128), ssd_vec, expand, batch, seq)
        y_d = _hgrn(hg, _rows8([lb[l], hgrn_norm_w[l]], D_GROUP), batch, seq)
        ln = _rows8([ln1_w[l], ln1_b[l], ln2_w[l], ln2_b[l]], D_MODEL)
        h = _post(h, (y_a, y_b, y_c, y_d), w_out[l].astype(BF16), w_up[l].astype(BF16),
                  w_down[l].astype(BF16), ln)
    return h.reshape(batch, seq, D_MODEL)
```

```python
import functools
import math

import jax
import jax.numpy as jnp
from jax import lax
from jax.experimental import pallas as pl
from jax.experimental.pallas import tpu as pltpu

F32 = jnp.float32
BF16 = jnp.bfloat16

D_MODEL = 1024
DEPTH = 2
D_GROUP = 256
HEAD_DIM = 64
N_HEADS = 4
D_FF = 4 * D_MODEL
ALPHA = (2.0 * DEPTH) ** 0.25
LN_EPS = 1e-5
RMS_EPS = 1e-5
RWKV_GN_EPS = HEAD_DIM * 1e-5
RWKV_COLS = 896
ATTN_COLS = 768
SSD_STATE = 128
SSD_XBC = 768
SSD_COLS = 1028
HGRN_COLS = 1024
DILATIONS = (1, 4, 16)
ATTN_BLK = 128
ATTN_GROUP = 4
ALIBI_SLOPES = tuple(2.0 ** (-8.0 * (h + 1) / N_HEADS) for h in range(N_HEADS))

RW_W, AT_W, SD_W, HG_W = 1024, 768, 1152, 1024

RW_T = 64
RW_NCH = 2
RW_NB = 8
RW_ROWS = RW_T * RW_NCH
SSD_T = 128
SSD_NB = 8
HG_C = 16
HG_T = 512
PROJ_TM = 512
POST_TM = 1024
FF_CHUNK = 512

VMEM_LIMIT = 56 * 1024 * 1024


def _dot(a, b, dims):
    return lax.dot_general(a, b, (dims, ((), ())), preferred_element_type=F32)


_NN = ((1,), (0,))
_NT = ((1,), (1,))
_TN = ((0,), (0,))


def _mm(a, b, dims=_NN):
    return _dot(a.astype(BF16), b.astype(BF16), dims)


def _split(x, pieces):
    out = []
    for i in range(pieces):
        hi = x.astype(BF16)
        out.append(hi)
        if i + 1 < pieces:
            x = x - hi.astype(F32)
    return out


def _mmx_l(m01, x, dims=_NN, pieces=3):
    return sum(_dot(m01, xp, dims) for xp in _split(x, pieces))


def _mmx_r(x, m01, dims=_NN, pieces=3):
    return sum(_dot(xp, m01, dims) for xp in _split(x, pieces))


def _mm3(a, b, dims=_NN):
    ah = a.astype(BF16)
    al = (a - ah.astype(F32)).astype(BF16)
    bh = b.astype(BF16)
    bl = (b - bh.astype(F32)).astype(BF16)
    return _dot(ah, bh, dims) + _dot(ah, bl, dims) + _dot(al, bh, dims)


def _iota(shape, dim):
    return lax.broadcasted_iota(jnp.int32, shape, dim)


def _head_masks(width=D_GROUP):
    lane = _iota((1, width), 1) // HEAD_DIM
    return [(lane == h).astype(F32) for h in range(N_HEADS)]


def _same_head(n=D_GROUP):
    return (_iota((n, n), 0) // HEAD_DIM) == (_iota((n, n), 1) // HEAD_DIM)


def _sigmoid(x):
    return 0.5 * jnp.tanh(0.5 * x) + 0.5


def _sigmoid_tail(x):
    return 1.0 / (1.0 + jnp.exp(-x))


def _silu(x):
    return x * _sigmoid(x)


def _softplus(x):
    return jnp.maximum(x, 0.0) + jnp.log(1.0 + jnp.exp(-jnp.abs(x)))


def _layer_norm(x, w, b):
    mu = jnp.mean(x, axis=-1, keepdims=True)
    d = x - mu
    var = jnp.mean(d * d, axis=-1, keepdims=True)
    return d * lax.rsqrt(var + LN_EPS) * w + b


def _shift_rows(x, prev_row):
    row = _iota((x.shape[0], 1), 0)
    return jnp.where(row == 0, prev_row, pltpu.roll(x, 1, axis=0))


def _proj_kernel(has_vres, *refs):
    if has_vres:
        x_ref, w_ref, wv_ref, rw_ref, at_ref, sd_ref, hg_ref = refs
    else:
        x_ref, w_ref, rw_ref, at_ref, sd_ref, hg_ref = refs
    xb = x_ref[...].astype(BF16)
    n_in = w_ref.shape[1]

    def emit(ref, c_out, c_in, width):
        wb = w_ref[:, c_in:c_in + width].astype(BF16)
        ref[:, c_out:c_out + width] = jnp.dot(xb, wb, preferred_element_type=F32)

    src = 0
    for ref, cols, padded in ((rw_ref, RWKV_COLS, RW_W), (at_ref, ATTN_COLS, AT_W),
                              (sd_ref, SSD_COLS, SD_W), (hg_ref, HGRN_COLS, HG_W)):
        for c0 in range(0, padded, 256):
            width = min(256, padded - c0)
            if c0 + width <= cols:
                emit(ref, c0, src + c0, width)
                continue
            real = max(cols - c0, 0)
            real_pad = -(-real // 128) * 128
            if has_vres and ref is rw_ref and real == real_pad and real_pad + wv_ref.shape[1] == width:
                wb = jnp.concatenate([w_ref[:, src + c0:src + c0 + real].astype(BF16), wv_ref[...]], axis=1)
                ref[:, c0:c0 + width] = jnp.dot(xb, wb, preferred_element_type=F32)
                continue
            if real:
                take = min(real_pad, n_in - (src + c0))
                wb = w_ref[:, src + c0:src + c0 + take].astype(BF16)
                lane = _iota((1, take), 1)
                wb = jnp.where(lane < real, wb, jnp.zeros_like(wb))
                ref[:, c0:c0 + take] = jnp.dot(xb, wb, preferred_element_type=F32)
                if take < real_pad:
                    ref[:, c0 + take:c0 + real_pad] = jnp.zeros((xb.shape[0], real_pad - take), F32)
            rest = width - real_pad
            if rest:
                if has_vres and ref is rw_ref:
                    ref[:, c0 + real_pad:c0 + width] = jnp.dot(xb, wv_ref[...], preferred_element_type=F32)
                else:
                    ref[:, c0 + real_pad:c0 + width] = jnp.zeros((xb.shape[0], rest), F32)
        src += cols


def _project(x2d, w_in, layer, w_vres):
    m = x2d.shape[0]
    n_in = w_in.shape[2]
    has_vres = w_vres is not None
    outs = [jax.ShapeDtypeStruct((m, w), F32) for w in (RW_W, AT_W, SD_W, HG_W)]
    in_specs = [pl.BlockSpec((PROJ_TM, D_MODEL), lambda i: (i, 0)),
                pl.BlockSpec((None, D_MODEL, n_in), lambda i: (layer, 0, 0), pipeline_mode=pl.Buffered(1))]
    args = [x2d, w_in]
    if has_vres:
        in_specs.append(pl.BlockSpec((D_MODEL, 128), lambda i: (0, 0)))
        args.append(w_vres)
    return pl.pallas_call(
        functools.partial(_proj_kernel, has_vres),
        grid=(m // PROJ_TM,),
        in_specs=in_specs,
        out_specs=[pl.BlockSpec((PROJ_TM, w), lambda i: (i, 0)) for w in (RW_W, AT_W, SD_W, HG_W)],
        out_shape=outs,
        compiler_params=pltpu.CompilerParams(dimension_semantics=("parallel",),
                                             vmem_limit_bytes=VMEM_LIMIT),
    )(*args)


def _stack_heads(x, head_sel):
    xb = x.astype(BF16)
    zero = jnp.zeros_like(xb)
    return jnp.concatenate([jnp.where(m, xb, zero) for m in head_sel], axis=0)


def _rwkv_chunks(at, bh, kh, rt, v, head_sel):
    n = len(at)
    rng = range(n)
    T = at[0].shape[0]
    G = D_GROUP
    row = _iota((T, G), 0)
    lane_t = _iota((T, G), 1) % T
    strict = lane_t < row
    incl = lane_t <= row
    st = lambda x: _stack_heads(x, head_sel)
    lhs = [jnp.concatenate([at[j], rt[j]], axis=0).astype(BF16) for j in rng]
    ab = [_dot(lhs[j], st(bh[j]), _NT) for j in rng]
    ak = [_dot(lhs[j], st(kh[j]), _NT) for j in rng]
    n_u = [jnp.where(strict, ab[j][0:T], 0.0) for j in rng]
    a_ak = [jnp.where(strict, ak[j][0:T], 0.0) for j in rng]
    m_rb = [jnp.where(incl, ab[j][T:2 * T], 0.0).astype(BF16) for j in rng]
    m_rk = [jnp.where(incl, ak[j][T:2 * T], 0.0).astype(BF16) for j in rng]

    eye = jnp.where(lane_t == row, 1.0, 0.0)
    x_u = [eye + n_u[j] for j in rng]
    pw = [_dot(n_u[j].astype(BF16), st(n_u[j]), _NN) for j in rng]
    steps = T.bit_length() - 2
    for it in range(steps):
        w = [st(pw[j]) for j in rng]
        if it + 1 < steps:
            res = [_dot(jnp.concatenate([pw[j], x_u[j]], axis=0).astype(BF16), w[j], _NN) for j in rng]
            pw = [res[j][0:T] for j in rng]
            x_u = [x_u[j] + res[j][T:2 * T] for j in rng]
        else:
            x_u = [x_u[j] + _dot(x_u[j].astype(BF16), w[j], _NN) for j in rng]

    v_s = [st(v[j]) for j in rng]
    akv = [_dot(a_ak[j].astype(BF16), v_s[j], _NN) for j in rng]
    xb = [x_u[j].astype(BF16) for j in rng]
    p_u = [_dot(xb[j], st(at[j]), _NN) for j in rng]
    q_u = [_dot(xb[j], st(akv[j]), _NN) for j in rng]
    mkv = [_dot(m_rk[j], v_s[j], _NN) for j in rng]
    return p_u, q_u, m_rb, mkv


def _rwkv_kernel(has_vres, *refs):
    if has_vres:
        (f_ref, fprev_ref, vf_ref, mu_ref, vec_ref, w2_ref, a2_ref, g2_ref, v2_ref,
         y_ref, s_ref) = refs
    else:
        (f_ref, fprev_ref, mu_ref, vec_ref, w2_ref, a2_ref, g2_ref,
         y_ref, vout_ref, s_ref) = refs
    c = pl.program_id(1)
    T = RW_T
    G = D_GROUP

    @pl.when(c == 0)
    def _():
        s_ref[...] = jnp.zeros_like(s_ref)

    w0, a0, k_k, k_a = (vec_ref[i:i + 1, :] for i in range(4))
    lnx_w, lnx_b, r_k = (vec_ref[i:i + 1, :] for i in range(4, 7))
    head_sel = [(_iota((1, G), 1) // HEAD_DIM) == h for h in range(N_HEADS)]
    same = _same_head()
    bo = same.astype(BF16)
    ltri = (_iota((T, T), 1) <= _iota((T, T), 0)).astype(BF16)

    def prologue(b0, nb):
        R = RW_ROWS * nb
        nchunks = RW_NCH * nb
        f = f_ref[b0:b0 + nb].reshape(R, RW_W)
        row = _iota((R, 1), 0)
        shifted = pltpu.roll(f, 1, axis=0)
        for b in range(nb):
            prev_row = jnp.where(c == 0, 0.0, fprev_ref[b0 + b, 7:8, :])
            shifted = jnp.where(row == b * RW_ROWS, prev_row, shifted)
        x = f + (shifted - f) * mu_ref[...]
        r = x[:, 0:G]
        k = x[:, G:2 * G]
        v = x[:, 2 * G:3 * G]
        seg = x[:, 3 * G:3 * G + 128]
        p = w0 + _mm3(jnp.tanh(seg), w2_ref[...])
        lw = (-math.exp(-0.5) * math.log2(math.e)) * _sigmoid(p)
        a = _sigmoid(a0 + _mm(seg, a2_ref[...]))
        g = _mm(_sigmoid(seg), g2_ref[...])
        if has_vres:
            seg2 = x[:, 3 * G + 128:3 * G + 256]
            v0 = vec_ref[7:8, :]
            v = v + (vf_ref[b0:b0 + nb].reshape(R, G) - v) * _sigmoid(v0 + _mm(seg2, v2_ref[...]))
        else:
            vout_ref[b0:b0 + nb] = v.reshape(nb, RW_ROWS, G)
        kk = k * k_k
        kk = kk * lax.rsqrt(jnp.maximum(_mm(kk * kk, bo), 1e-24))
        k2 = k * (1.0 + (a - 1.0) * k_a)
        kka = kk * a
        cs = jnp.concatenate([_mmx_l(ltri, lw[j * T:(j + 1) * T], pieces=2) for j in range(nchunks)],
                             axis=0)
        c_last = jnp.concatenate(
            [jnp.broadcast_to(cs[(j + 1) * T - 1:(j + 1) * T, :], (T, G)) for j in range(nchunks)], axis=0)
        e_neg = jnp.exp2(-cs)
        e_dec = jnp.exp2(c_last - cs)
        chunks = lambda z: [z[j * T:(j + 1) * T] for j in range(nchunks)]
        return dict(b0=b0, nb=nb, nchunks=nchunks, r=r, k2=k2, v=v, g=g, c_last=c_last,
                    at=chunks(-kk * jnp.exp2(cs - lw)), bh=chunks(kka * e_neg), kh=chunks(k2 * e_neg),
                    rt=chunks(r * jnp.exp2(cs)), bdec=chunks(kka * e_dec), kdec=chunks(k2 * e_dec),
                    vc=chunks(v))

    def middle(ctx):
        n = ctx["nchunks"]
        p_u, q_u, m_rb, mkv = _rwkv_chunks(ctx["at"], ctx["bh"], ctx["kh"], ctx["rt"], ctx["vc"], head_sel)
        ctx.update(q_u=q_u, m_rb=m_rb, mkv=mkv)
        ctx["rp_lhs"] = [jnp.concatenate([ctx["rt"][i], p_u[i]], axis=0).astype(BF16) for i in range(n)]
        ctx["dec_t"] = [jnp.concatenate([ctx["bdec"][i], ctx["kdec"][i]], axis=0).T.astype(BF16)
                        for i in range(n)]
        ctx["w_col"] = [jnp.exp2(jnp.broadcast_to(ctx["c_last"][i * T:i * T + 1, :], (128, G))).T[:, 0:1]
                        for i in range(n)]

    def chain(ctx):
        b0, nb, n = ctx["b0"], ctx["nb"], ctx["nchunks"]
        s = [s_ref[b0 + b] for b in range(nb)]
        rs = [None] * n
        sas = [None] * n
        seqs = range(nb)
        for j in range(RW_NCH):
            ids = [b * RW_NCH + j for b in seqs]
            rp = [_dot(ctx["rp_lhs"][i], s[b].astype(BF16), _NN) for b, i in zip(seqs, ids)]
            for b, i in zip(seqs, ids):
                rs[i] = rp[b][0:T]
                sas[i] = rp[b][T:2 * T] + ctx["q_u"][i]
            upd = [_dot(ctx["dec_t"][i], jnp.concatenate([sas[i], ctx["vc"][i]], axis=0).astype(BF16), _NN)
                   for i in ids]
            s = [s[b] * ctx["w_col"][i] + jnp.where(same, upd[b], 0.0) for b, i in zip(seqs, ids)]
        outs = [rs[i] + _dot(ctx["m_rb"][i], _stack_heads(sas[i], head_sel), _NN) + ctx["mkv"][i]
                for i in range(n)]
        for b in seqs:
            s_ref[b0 + b] = s[b]
        ctx["o"] = jnp.concatenate(outs, axis=0)

    def epilogue(ctx):
        o = ctx["o"]
        mean = _mmx_r(o, bo, pieces=2) * (1.0 / HEAD_DIM)
        d = o - mean
        var = _mm(d * d, bo) * (1.0 / HEAD_DIM)
        yn = d * lax.rsqrt(var + RWKV_GN_EPS) * lnx_w + lnx_b
        bonus = _mm(ctx["r"] * ctx["k2"] * r_k, bo) * ctx["v"]
        y_ref[ctx["b0"]:ctx["b0"] + ctx["nb"]] = ((yn + bonus) * ctx["g"]).reshape(ctx["nb"], RW_ROWS, G)

    ctx = prologue(0, RW_NB)
    middle(ctx)
    chain(ctx)
    epilogue(ctx)


def _rwkv(rw, v_first, mu, vec, w2p, a2p, g2p, v2p, batch, seq):
    has_vres = v_first is not None
    nc = seq // RW_ROWS
    m = batch * seq
    row_spec = lambda w: pl.BlockSpec((RW_NB, RW_ROWS, w), lambda b, c: (b, c, 0))
    const = lambda shape: pl.BlockSpec(shape, lambda b, c: (0, 0))
    prev_spec = pl.BlockSpec(
        (RW_NB, 8, RW_W), lambda b, c: (b, jnp.maximum(c * (RW_ROWS // 8) - 1, 0), 0))
    rw3 = rw.reshape(batch, seq, RW_W)
    in_specs = [row_spec(RW_W), prev_spec]
    args = [rw3, rw3]
    if has_vres:
        in_specs.append(row_spec(D_GROUP))
        args.append(v_first.reshape(batch, seq, D_GROUP))
    in_specs += [const((1, RW_W)), const((8, D_GROUP)), const((128, D_GROUP)),
                 const((128, D_GROUP)), const((128, D_GROUP))]
    args += [mu, vec, w2p, a2p, g2p]
    if has_vres:
        in_specs.append(const((128, D_GROUP)))
        args.append(v2p)
    out_shape = [jax.ShapeDtypeStruct((batch, seq, D_GROUP), F32)]
    out_specs = [row_spec(D_GROUP)]
    if not has_vres:
        out_shape.append(jax.ShapeDtypeStruct((batch, seq, D_GROUP), F32))
        out_specs.append(row_spec(D_GROUP))
    outs = pl.pallas_call(
        functools.partial(_rwkv_kernel, has_vres),
        grid=(batch // RW_NB, nc),
        in_specs=in_specs,
        out_specs=out_specs,
        out_shape=out_shape,
        scratch_shapes=[pltpu.VMEM((RW_NB, D_GROUP, D_GROUP), F32)],
        compiler_params=pltpu.CompilerParams(dimension_semantics=("parallel", "arbitrary"),
                                             vmem_limit_bytes=VMEM_LIMIT),
    )(*args)
    y = outs[0].reshape(m, D_GROUP)
    if has_vres:
        return y, v_first
    return y, outs[1].reshape(m, D_GROUP)


def _attn_kernel(qkv_ref, o_ref, perm_ref, acc_ref, l_ref, m_ref, bias_ref):
    G = D_GROUP
    blk = ATTN_BLK
    seq = qkv_ref.shape[0]
    nres = DILATIONS[-1]
    sub = seq // nres
    head_sel = [(_iota((1, G), 1) // HEAD_DIM) == h for h in range(N_HEADS)]
    scale = HEAD_DIM ** -0.5 * math.log2(math.e)

    def jmap(a, d):
        run = blk * d // nres
        return (nres // d) * (a % run) + a // run

    to_perm = (_iota((blk, blk), 1) == jmap(_iota((blk, blk), 0), 1)).astype(BF16)
    from_perm = (_iota((blk, blk), 0) == jmap(_iota((blk, blk), 1), 1)).astype(BF16)
    run1 = blk // nres
    col_scale = jnp.where(_iota((1, AT_W), 1) < G, scale, 1.0)

    def permute(n, carry):
        r0 = pl.multiple_of(n * blk, blk)
        pb = jnp.dot(to_perm, (qkv_ref[pl.ds(r0, blk), :] * col_scale).astype(BF16),
                     preferred_element_type=F32)
        for e in range(nres):
            dst = pl.multiple_of(e * sub + n * run1, run1)
            perm_ref[pl.ds(dst, run1), :] = pb[e * run1:(e + 1) * run1, :]
        return carry

    lax.fori_loop(0, seq // blk, permute, 0, unroll=4)

    branches = tuple(reversed(DILATIONS))
    for bi, d in enumerate(branches):
        jq = jmap(_iota((blk, 2 * blk), 0), d)
        kb = _iota((blk, 2 * blk), 1)
        dist = blk + jq - (jmap(kb % blk, d) + (kb // blk) * blk)
        in_band = (dist >= 0) & (dist <= blk)
        for h in range(N_HEADS):
            bias_ref[bi * N_HEADS + h] = jnp.where(
                in_band, dist.astype(F32) * (-ALIBI_SLOPES[h] * d * math.log2(math.e)), -jnp.inf)

    for bi, d in enumerate(branches):
        runs = nres // d
        run = blk // runs
        nb = seq // (d * blk)
        first = bi == 0
        last = bi == len(branches) - 1

        def blocks(items, with_prev, bi=bi, d=d, runs=runs, run=run, first=first, last=last):
            its = range(len(items))
            heads = range(N_HEADS)

            def starts(rho, nn):
                return [pl.multiple_of((rho + d * e) * sub + nn * run, run) for e in range(runs)]

            def gather(st, c0):
                return jnp.concatenate([perm_ref[pl.ds(s0, run), c0:c0 + G] for s0 in st],
                                       axis=0).astype(BF16)

            cur = [starts(rho, n) for rho, n in items]
            qb = [gather(cur[i], 0) for i in its]
            if with_prev:
                prv = [starts(rho, n - 1) for rho, n in items]
                kcat = [jnp.concatenate([gather(prv[i], G), gather(cur[i], G)], axis=0) for i in its]
                vcat = [jnp.concatenate([gather(prv[i], 2 * G), gather(cur[i], 2 * G)], axis=0)
                        for i in its]
            else:
                kcat = [gather(cur[i], G) for i in its]
                vcat = [gather(cur[i], 2 * G) for i in its]
            zq = jnp.zeros_like(qb[0])
            kc0 = 0 if with_prev else blk
            pairs = [(0, 1), (2, 3)]
            s_p = [[_dot(jnp.concatenate([jnp.where(head_sel[h], qb[i], zq) for h in pr], axis=0),
                         kcat[i], _NT) for pr in pairs] for i in its]
            s = [[s_p[i][h // 2][(h % 2) * blk:(h % 2 + 1) * blk]
                  + bias_ref[bi * N_HEADS + h, :, kc0:2 * blk] for h in heads] for i in its]
            mh = [[jnp.max(s[i][h], axis=-1, keepdims=True) for h in heads] for i in its]
            ph = [[jnp.exp2(s[i][h] - mh[i][h]) for h in heads] for i in its]
            lh = [[jnp.sum(ph[i][h], axis=-1, keepdims=True) for h in heads] for i in its]
            pv_p = [[_dot(jnp.concatenate([ph[i][h].astype(BF16) for h in pr], axis=0), vcat[i], _NN)
                     for pr in pairs] for i in its]
            def by_head(parts):
                out = jnp.broadcast_to(parts[N_HEADS - 1], (blk, G))
                for h in reversed(range(N_HEADS - 1)):
                    out = jnp.where(head_sel[h], parts[h], out)
                return out

            acc_b = [by_head([pv_p[i][h // 2][(h % 2) * blk:(h % 2 + 1) * blk] for h in heads])
                     for i in its]
            l_b = [by_head(lh[i]) for i in its]
            m_b = [by_head(mh[i]) for i in its]
            for i in its:
                outs = []
                for e, s0 in enumerate(cur[i]):
                    idx = pl.ds(s0, run)
                    sl = slice(e * run, (e + 1) * run)
                    if first:
                        acc_ref[idx, :] = acc_b[i][sl]
                        l_ref[idx, :] = l_b[i][sl]
                        m_ref[idx, :] = m_b[i][sl]
                    else:
                        m_old = m_ref[idx, :]
                        m_new = jnp.maximum(m_old, m_b[i][sl])
                        w_old = jnp.exp2(m_old - m_new)
                        w_new = jnp.exp2(m_b[i][sl] - m_new)
                        acc_n = acc_ref[idx, :] * w_old + acc_b[i][sl] * w_new
                        l_n = l_ref[idx, :] * w_old + l_b[i][sl] * w_new
                        if last:
                            outs.append(acc_n / l_n)
                        else:
                            acc_ref[idx, :] = acc_n
                            l_ref[idx, :] = l_n
                            m_ref[idx, :] = m_new
                if last:
                    r0 = pl.multiple_of(items[i][1] * blk, blk)
                    o_ref[pl.ds(r0, blk), :] = _mmx_l(from_perm, jnp.concatenate(outs, axis=0))

        def run_blocks(count, index, with_prev, blocks=blocks):
            grp = ATTN_GROUP - 1 if with_prev else ATTN_GROUP

            def group(it, carry):
                blocks([index(grp * it + u) for u in range(grp)], with_prev)
                return carry

            if count >= grp:
                lax.fori_loop(0, count // grp, group, 0)
            rest = count % grp
            if rest:
                blocks([index(count - rest + u) for u in range(rest)], with_prev)

        run_blocks(d, lambda i: (i, 0), False)
        if nb > 1:
            run_blocks(d * (nb - 1), lambda i, nb=nb: (i // (nb - 1), i % (nb - 1) + 1), True)


def _attention(at, batch, seq):
    m = batch * seq
    return pl.pallas_call(
        _attn_kernel,
        grid=(batch,),
        in_specs=[pl.BlockSpec((seq, AT_W), lambda b: (b, 0))],
        out_specs=pl.BlockSpec((seq, D_GROUP), lambda b: (b, 0)),
        out_shape=jax.ShapeDtypeStruct((m, D_GROUP), F32),
        scratch_shapes=[pltpu.VMEM((seq, AT_W), F32)] + [pltpu.VMEM((seq, D_GROUP), F32)] * 3
        + [pltpu.VMEM((len(DILATIONS) * N_HEADS, ATTN_BLK, 2 * ATTN_BLK), F32)],
        compiler_params=pltpu.CompilerParams(dimension_semantics=("parallel",),
                                             vmem_limit_bytes=VMEM_LIMIT),
    )(at)


def _ssd_kernel(f_ref, fprev_ref, cw_ref, cb_ref, dtb_ref, vec_ref, ex_ref, y_ref, st_ref):
    c = pl.program_id(1)
    T = SSD_T
    G = D_GROUP
    N = SSD_STATE
    nb = SSD_NB
    R = nb * T
    seqs = range(nb)

    @pl.when(c == 0)
    def _():
        st_ref[...] = jnp.zeros_like(st_ref)

    f = f_ref[...].reshape(R, SD_W)
    xbc_raw = f[:, G:G + SSD_XBC]
    prev8 = [jnp.where(c == 0, 0.0, fprev_ref[b, :, G:G + SSD_XBC]) for b in seqs]
    row8 = _iota((8, 1), 0)
    conv = xbc_raw * cw_ref[3:4, :] + cb_ref[...]
    for j in range(1, 4):
        rolled = pltpu.roll(xbc_raw, j, axis=0)
        pieces = []
        for b in seqs:
            pieces.append(jnp.where(row8 < j, pltpu.roll(prev8[b], j, axis=0), rolled[b * T:b * T + 8]))
            pieces.append(rolled[b * T + 8:(b + 1) * T])
        conv = conv + jnp.concatenate(pieces, axis=0) * cw_ref[3 - j:4 - j, :]
    xbc = _silu(conv)
    xs = xbc[:, 0:G]
    bm = xbc[:, G:G + 2 * N]
    cm = xbc[:, G + 2 * N:G + 4 * N]
    z = f[:, 0:G]

    a_dense, d_dense, norm_w = (vec_ref[i:i + 1, :] for i in range(3))
    dt = _softplus(f[:, G + SSD_XBC:G + SSD_XBC + 128] + dtb_ref[...])
    dt_dense = _mmx_r(dt, ex_ref[...], pieces=2)
    da = dt_dense * a_dense
    xdt = xs * dt_dense
    ltri = (_iota((T, T), 1) <= _iota((T, T), 0))
    ltri_b = ltri.astype(BF16)
    umat = (_iota((T, T), 0) > _iota((T, T), 1)).astype(F32)
    masks = _head_masks()
    rows = [slice(b * T, (b + 1) * T) for b in seqs]

    cs = [_mmx_l(ltri_b, da[rows[b]], pieces=2) for b in seqs]
    scores = [[_mm(cm[rows[b], g * N:(g + 1) * N], bm[rows[b], g * N:(g + 1) * N], _NT)
               for g in range(2)] for b in seqs]
    y_diag = []
    for b in seqs:
        acc = None
        for h in range(N_HEADS):
            da_col = jnp.broadcast_to(da[rows[b], h * HEAD_DIM:h * HEAD_DIM + 1], (T, T))
            seg = _mmx_l(ltri_b, da_col * umat, pieces=2)
            dec = jnp.where(ltri, jnp.exp2(seg), 0.0)
            term = _mm(scores[b][h // 2] * dec, xdt[rows[b]] * masks[h])
            acc = term if acc is None else acc + term
        y_diag.append(acc)

    ys = []
    for b in seqs:
        st = st_ref[b]
        cmb = cm[rows[b]]
        y_off = jnp.concatenate([_mm(cmb[:, 0:N], st[:, 0:N]), _mm(cmb[:, N:2 * N], st[:, N:2 * N])],
                                axis=1) * jnp.exp2(cs[b])
        cs_last = cs[b][T - 1:T, :]
        xd = xdt[rows[b]] * jnp.exp2(cs_last - cs[b])
        bmb = bm[rows[b]]
        st_ref[b] = st * jnp.exp2(cs_last) + jnp.concatenate(
            [_mm(bmb[:, 0:N], xd[:, 0:N], _TN), _mm(bmb[:, N:2 * N], xd[:, N:2 * N], _TN)], axis=1)
        ys.append(y_diag[b] + y_off)

    y = (jnp.concatenate(ys, axis=0) + xs * d_dense) * _silu(z)
    halves = []
    for g in range(2):
        yg = y[:, g * N:(g + 1) * N]
        halves.append(yg * lax.rsqrt(jnp.mean(yg * yg, axis=-1, keepdims=True) + RMS_EPS))
    y_ref[...] = (jnp.concatenate(halves, axis=1) * norm_w).reshape(nb, T, G)


def _ssd(sd, cw, cb, dtb, vec, ex, batch, seq):
    nc = seq // SSD_T
    m = batch * seq
    const = lambda shape: pl.BlockSpec(shape, lambda b, c: (0, 0))
    sd3 = sd.reshape(batch, seq, SD_W)
    out = pl.pallas_call(
        _ssd_kernel,
        grid=(batch // SSD_NB, nc),
        in_specs=[pl.BlockSpec((SSD_NB, SSD_T, SD_W), lambda b, c: (b, c, 0)),
                  pl.BlockSpec((SSD_NB, 8, SD_W),
                               lambda b, c: (b, jnp.maximum(c * (SSD_T // 8) - 1, 0), 0)),
                  const((4, SSD_XBC)), const((1, SSD_XBC)), const((1, 128)), const((8, D_GROUP)),
                  const((128, D_GROUP))],
        out_specs=pl.BlockSpec((SSD_NB, SSD_T, D_GROUP), lambda b, c: (b, c, 0)),
        out_shape=jax.ShapeDtypeStruct((batch, seq, D_GROUP), F32),
        scratch_shapes=[pltpu.VMEM((SSD_NB, SSD_STATE, D_GROUP), F32)],
        compiler_params=pltpu.CompilerParams(dimension_semantics=("parallel", "arbitrary"),
                                             vmem_limit_bytes=VMEM_LIMIT),
    )(sd3, sd3, cw, cb, dtb, vec, ex)
    return out.reshape(m, D_GROUP)


def _hgrn_kernel(f_ref, vec_ref, y_ref, st_ref):
    c = pl.program_id(1)
    T = HG_T
    C = HG_C
    G = D_GROUP
    nblk = T // C

    @pl.when(c == 0)
    def _():
        st_ref[...] = jnp.zeros_like(st_ref)

    lb = vec_ref[0:1, :]
    norm_w = vec_ref[1:2, :]
    forget = lb + (1.0 - lb) * _sigmoid_tail(f_ref[:, G:2 * G])
    q = _silu(f_ref[:, 0:G])
    k = 1.0 - forget
    v = f_ref[:, 2 * G:3 * G]
    grp = 4 * C
    blk_tri = ((_iota((grp, grp), 0) // C == _iota((grp, grp), 1) // C)
               & (_iota((grp, grp), 1) <= _iota((grp, grp), 0))).astype(BF16)
    log_f = jnp.log(forget)
    b = jnp.concatenate([_mmx_l(blk_tri, log_f[r0:r0 + grp], pieces=2) for r0 in range(0, T, grp)], axis=0)
    b2 = b * math.log2(math.e)
    c2 = b2 - jnp.log2(k)

    head_sel = [(_iota((1, G), 1) // HEAD_DIM) == h for h in range(N_HEADS)]
    bo = _same_head().astype(BF16)
    t_idx = _iota((C, 1), 0)
    blocks = lambda z: [z[i * C:(i + 1) * C] for i in range(nblk)]
    bq, qq, kq, vv, b2q, c2q = blocks(b), blocks(q), blocks(k), blocks(v), blocks(b2), blocks(c2)
    b_last = [bq[i][C - 1:C, :] for i in range(nblk)]

    upd = [_dot(_stack_heads(vv[i], head_sel), _stack_heads(kq[i] * jnp.exp(b_last[i] - bq[i]), head_sel),
                _TN) for i in range(nblk)]
    q_in = [qq[i] * jnp.exp(bq[i]) for i in range(nblk)]
    half = C // 2
    o_intra = []
    for i in range(nblk):
        parts = []
        for s in range(C):
            lo = 0 if s < half else half
            e = jnp.exp2(b2q[i][lo:C] - c2q[i][s:s + 1, :])
            parts.append(jnp.where(t_idx[lo:C] >= s, qq[i][lo:C] * e, 0.0))
        z = _mm(jnp.concatenate(parts, axis=0), bo)
        o_lo = z[0:half] * vv[i][0:1, :]
        o_hi = z[half:C] * vv[i][0:1, :]
        for s in range(1, half):
            o_lo = o_lo + z[s * C:s * C + half] * vv[i][s:s + 1, :]
            o_hi = o_hi + z[s * C + half:(s + 1) * C] * vv[i][s:s + 1, :]
        for s in range(half, C):
            r0 = half * C + (s - half) * half
            o_hi = o_hi + z[r0:r0 + half] * vv[i][s:s + 1, :]
        o_intra.append(jnp.concatenate([o_lo, o_hi], axis=0))

    st = st_ref[...]
    outs = []
    for i in range(nblk):
        outs.append(_mm(q_in[i], st, _NT) + o_intra[i])
        st = st * jnp.exp(b_last[i]) + upd[i]
    st_ref[...] = st
    o = jnp.concatenate(outs, axis=0)
    ms = _mmx_r(o * o, bo, pieces=2) * (1.0 / HEAD_DIM)
    y_ref[...] = o * lax.rsqrt(ms + RMS_EPS) * norm_w * _silu(f_ref[:, 3 * G:4 * G])


def _hgrn(hg, vec, batch, seq):
    nc = seq // HG_T
    m = batch * seq
    return pl.pallas_call(
        _hgrn_kernel,
        grid=(batch, nc),
        in_specs=[pl.BlockSpec((HG_T, HG_W), lambda b, c: (b * nc + c, 0)),
                  pl.BlockSpec((8, D_GROUP), lambda b, c: (0, 0))],
        out_specs=pl.BlockSpec((HG_T, D_GROUP), lambda b, c: (b * nc + c, 0)),
        out_shape=jax.ShapeDtypeStruct((m, D_GROUP), F32),
        scratch_shapes=[pltpu.VMEM((D_GROUP, D_GROUP), F32)],
        compiler_params=pltpu.CompilerParams(dimension_semantics=("parallel", "arbitrary"),
                                             vmem_limit_bytes=VMEM_LIMIT),
    )(hg, vec)


def _post_kernel(x_ref, ya_ref, yb_ref, yc_ref, yd_ref, wo_ref, wu_ref, wd_ref, ln_ref, o_ref):
    half = POST_TM // 2
    rows = [slice(0, half), slice(half, POST_TM)]

    def mixed(r):
        acc = ALPHA * x_ref[r, :]
        for i, ref in enumerate((ya_ref, yb_ref, yc_ref, yd_ref)):
            acc = acc + jnp.dot(ref[r, :].astype(BF16), wo_ref[i * D_GROUP:(i + 1) * D_GROUP, :],
                                preferred_element_type=F32)
        return acc

    def ffn_chunk(xb, acc, c0):
        h = jnp.maximum(jnp.dot(xb, wu_ref[:, c0:c0 + FF_CHUNK], preferred_element_type=F32), 0.0)
        return acc + jnp.dot((h * h).astype(BF16), wd_ref[c0:c0 + FF_CHUNK, :],
                             preferred_element_type=F32)

    chunks = list(range(0, D_FF, FF_CHUNK))
    pre_a = mixed(rows[0])
    x1a = _layer_norm(pre_a, ln_ref[0:1, :], ln_ref[1:2, :])
    pre_b = mixed(rows[1])
    xba = x1a.astype(BF16)
    acc_a = ALPHA * x1a
    for c0 in chunks[:2]:
        acc_a = ffn_chunk(xba, acc_a, c0)
    x1b = _layer_norm(pre_b, ln_ref[0:1, :], ln_ref[1:2, :])
    for c0 in chunks[2:]:
        acc_a = ffn_chunk(xba, acc_a, c0)
    xbb = x1b.astype(BF16)
    acc_b = ALPHA * x1b
    for c0 in chunks[:2]:
        acc_b = ffn_chunk(xbb, acc_b, c0)
    o_ref[rows[0], :] = _layer_norm(acc_a, ln_ref[2:3, :], ln_ref[3:4, :])
    for c0 in chunks[2:]:
        acc_b = ffn_chunk(xbb, acc_b, c0)
    o_ref[rows[1], :] = _layer_norm(acc_b, ln_ref[2:3, :], ln_ref[3:4, :])


def _post(x2d, ys, wo, wu, wd, ln):
    m = x2d.shape[0]
    row = lambda w: pl.BlockSpec((POST_TM, w), lambda i: (i, 0))
    res = lambda shape: pl.BlockSpec(shape, lambda i: (0, 0), pipeline_mode=pl.Buffered(1))
    return pl.pallas_call(
        _post_kernel,
        grid=(m // POST_TM,),
        in_specs=[row(D_MODEL)] + [row(D_GROUP)] * 4
        + [res((D_MODEL, D_MODEL)), res((D_MODEL, D_FF)), res((D_FF, D_MODEL)), res((8, D_MODEL))],
        out_specs=row(D_MODEL),
        out_shape=jax.ShapeDtypeStruct((m, D_MODEL), F32),
        compiler_params=pltpu.CompilerParams(dimension_semantics=("parallel",),
                                             vmem_limit_bytes=VMEM_LIMIT),
    )(x2d, *ys, wo, wu, wd, ln)


def _pad_cols(a, width):
    return jnp.pad(a, ((0, 0), (0, width - a.shape[1])))


def _pad_rows(a, top, total):
    return jnp.pad(a, ((top, total - top - a.shape[0]), (0, 0)))


def _rep_heads(v):
    return jnp.repeat(v.astype(F32), HEAD_DIM)[None, :]


def _rows8(rows, width):
    out = jnp.concatenate([r.reshape(1, width).astype(F32) for r in rows], axis=0)
    return jnp.pad(out, ((0, 8 - out.shape[0]), (0, 0)))


def kernel(x, lower_bounds, w_in, w_in_vres, mu_shift, mu_vres, rwkv_w0, rwkv_w2, rwkv_a0, rwkv_a2, rwkv_g2, rwkv_k_k, rwkv_k_a, rwkv_r_k, rwkv_lnx_w, rwkv_lnx_b, rwkv_v0, rwkv_v2, ssd_conv_w, ssd_conv_b, ssd_dt_bias, ssd_A_log, ssd_D, ssd_norm_w, hgrn_norm_w, w_out, ln1_w, ln1_b, w_up, w_down, ln2_w, ln2_b):
    batch, seq, _ = x.shape
    lb = jax.nn.softmax(lower_bounds.astype(F32), axis=0)
    lb = jnp.cumsum(lb, axis=0) - lb[0]
    expand = (jnp.arange(128)[:, None] == (jnp.arange(D_GROUP)[None, :] // HEAD_DIM)).astype(BF16)

    h = x.reshape(batch * seq, D_MODEL)
    v_first = None
    for l in range(DEPTH):
        mu = mu_shift[l][None, :]
        w_vres = None
        if l > 0:
            w_vres = _pad_cols(w_in_vres[l - 1], 128).astype(BF16)
            mu = jnp.concatenate([mu, mu_vres[l - 1][None, :]], axis=1)
        rw, at, sd, hg = _project(h, w_in, l, w_vres)

        vec_rows = [rwkv_w0[l], rwkv_a0[l], rwkv_k_k[l], rwkv_k_a[l], rwkv_lnx_w[l], rwkv_lnx_b[l],
                    rwkv_r_k[l].reshape(-1)]
        if l > 0:
            vec_rows.append(rwkv_v0[l - 1])
        y_a, v_first = _rwkv(
            rw, v_first, _pad_cols(mu, RW_W), _rows8(vec_rows, D_GROUP),
            _pad_rows(rwkv_w2[l], 0, 128), _pad_rows(rwkv_a2[l], 32, 128), _pad_rows(rwkv_g2[l], 64, 128),
            _pad_rows(rwkv_v2[l - 1], 0, 128) if l > 0 else None, batch, seq)
        y_b = _attention(at, batch, seq)
        ssd_vec = _rows8([_rep_heads(-jnp.exp(ssd_A_log[l].astype(F32)) * math.log2(math.e)), _rep_heads(ssd_D[l]),
                          ssd_norm_w[l]], D_GROUP)
        y_c = _ssd(sd, ssd_conv_w[l].astype(F32), ssd_conv_b[l][None, :],
                   _pad_cols(ssd_dt_bias[l][None, :], 128), ssd_vec, expand, batch, seq)
        y_d = _hgrn(hg, _rows8([lb[l], hgrn_norm_w[l]], D_GROUP), batch, seq)
        ln = _rows8([ln1_w[l], ln1_b[l], ln2_w[l], ln2_b[l]], D_MODEL)
        h = _post(h, (y_a, y_b, y_c, y_d), w_out[l].astype(BF16), w_up[l].astype(BF16),
                  w_down[l].astype(BF16), ln)
    return h.reshape(batch, seq, D_MODEL)
```

```python
import functools
import math

import jax
import jax.numpy as jnp
from jax import lax
from jax.experimental import pallas as pl
from jax.experimental.pallas import tpu as pltpu

F32 = jnp.float32
BF16 = jnp.bfloat16

D_MODEL = 1024
DEPTH = 2
D_GROUP = 256
HEAD_DIM = 64
N_HEADS = 4
D_FF = 4 * D_MODEL
ALPHA = (2.0 * DEPTH) ** 0.25
LN_EPS = 1e-5
RMS_EPS = 1e-5
RWKV_GN_EPS = HEAD_DIM * 1e-5
RWKV_COLS = 896
ATTN_COLS = 768
SSD_STATE = 128
SSD_XBC = 768
SSD_COLS = 1028
HGRN_COLS = 1024
DILATIONS = (1, 4, 16)
ATTN_BLK = 128
ATTN_GROUP = 4
ALIBI_SLOPES = tuple(2.0 ** (-8.0 * (h + 1) / N_HEADS) for h in range(N_HEADS))

RW_W, AT_W, SD_W, HG_W = 1024, 768, 1152, 1024

RW_T = 64
RW_NCH = 2
RW_NB = 8
RW_ROWS = RW_T * RW_NCH
SSD_T = 128
SSD_NB = 8
HG_C = 16
HG_T = 512
PROJ_TM = 512
POST_TM = 1024
FF_CHUNK = 512

V7X_VMEM_BYTES = 64 * 1024 * 1024
VMEM_LIMIT = V7X_VMEM_BYTES - 8 * 1024 * 1024


def _dot(a, b, dims):
    return lax.dot_general(a, b, (dims, ((), ())), preferred_element_type=F32)


_NN = ((1,), (0,))
_NT = ((1,), (1,))
_TN = ((0,), (0,))


def _mm(a, b, dims=_NN):
    return _dot(a.astype(BF16), b.astype(BF16), dims)


def _split(x, pieces):
    out = []
    for i in range(pieces):
        hi = x.astype(BF16)
        out.append(hi)
        if i + 1 < pieces:
            x = x - hi.astype(F32)
    return out


def _mmx_l(m01, x, dims=_NN, pieces=3):
    return sum(_dot(m01, xp, dims) for xp in _split(x, pieces))


def _mmx_r(x, m01, dims=_NN, pieces=3):
    return sum(_dot(xp, m01, dims) for xp in _split(x, pieces))


def _mm3(a, b, dims=_NN):
    ah = a.astype(BF16)
    al = (a - ah.astype(F32)).astype(BF16)
    bh = b.astype(BF16)
    bl = (b - bh.astype(F32)).astype(BF16)
    return _dot(ah, bh, dims) + _dot(ah, bl, dims) + _dot(al, bh, dims)


def _iota(shape, dim):
    return lax.broadcasted_iota(jnp.int32, shape, dim)


def _head_masks(width=D_GROUP):
    lane = _iota((1, width), 1) // HEAD_DIM
    return [(lane == h).astype(F32) for h in range(N_HEADS)]


def _same_head(n=D_GROUP):
    return (_iota((n, n), 0) // HEAD_DIM) == (_iota((n, n), 1) // HEAD_DIM)


def _sigmoid(x):
    return 0.5 * jnp.tanh(0.5 * x) + 0.5


def _sigmoid_tail(x):
    return 1.0 / (1.0 + jnp.exp(-x))


def _silu(x):
    return x * _sigmoid(x)


def _softplus(x):
    return jnp.maximum(x, 0.0) + jnp.log(1.0 + jnp.exp(-jnp.abs(x)))


def _layer_norm(x, w, b):
    mu = jnp.mean(x, axis=-1, keepdims=True)
    d = x - mu
    var = jnp.mean(d * d, axis=-1, keepdims=True)
    return d * lax.rsqrt(var + LN_EPS) * w + b


def _proj_kernel(has_vres, *refs):
    if has_vres:
        x_ref, w_ref, wv_ref, rw_ref, at_ref, sd_ref, hg_ref = refs
    else:
        x_ref, w_ref, rw_ref, at_ref, sd_ref, hg_ref = refs
    xb = x_ref[...].astype(BF16)
    n_in = w_ref.shape[1]

    def emit(ref, c_out, c_in, width):
        wb = w_ref[:, c_in:c_in + width].astype(BF16)
        ref[:, c_out:c_out + width] = jnp.dot(xb, wb, preferred_element_type=F32)

    src = 0
    for ref, cols, padded in ((rw_ref, RWKV_COLS, RW_W), (at_ref, ATTN_COLS, AT_W),
                              (sd_ref, SSD_COLS, SD_W), (hg_ref, HGRN_COLS, HG_W)):
        for c0 in range(0, padded, 256):
            width = min(256, padded - c0)
            if c0 + width <= cols:
                emit(ref, c0, src + c0, width)
                continue
            real = max(cols - c0, 0)
            real_pad = -(-real // 128) * 128
            if has_vres and ref is rw_ref and real == real_pad and real_pad + wv_ref.shape[1] == width:
                wb = jnp.concatenate([w_ref[:, src + c0:src + c0 + real].astype(BF16), wv_ref[...]], axis=1)
                ref[:, c0:c0 + width] = jnp.dot(xb, wb, preferred_element_type=F32)
                continue
            if real:
                take = min(real_pad, n_in - (src + c0))
                wb = w_ref[:, src + c0:src + c0 + take].astype(BF16)
                lane = _iota((1, take), 1)
                wb = jnp.where(lane < real, wb, jnp.zeros_like(wb))
                ref[:, c0:c0 + take] = jnp.dot(xb, wb, preferred_element_type=F32)
                if take < real_pad:
                    ref[:, c0 + take:c0 + real_pad] = jnp.zeros((xb.shape[0], real_pad - take), F32)
            rest = width - real_pad
            if rest:
                if has_vres and ref is rw_ref:
                    ref[:, c0 + real_pad:c0 + width] = jnp.dot(xb, wv_ref[...], preferred_element_type=F32)
                else:
                    ref[:, c0 + real_pad:c0 + width] = jnp.zeros((xb.shape[0], rest), F32)
        src += cols


def _project(x2d, w_in, layer, w_vres):
    m = x2d.shape[0]
    n_in = w_in.shape[2]
    has_vres = w_vres is not None
    outs = [jax.ShapeDtypeStruct((m, w), F32) for w in (RW_W, AT_W, SD_W, HG_W)]
    in_specs = [pl.BlockSpec((PROJ_TM, D_MODEL), lambda i: (i, 0)),
                pl.BlockSpec((None, D_MODEL, n_in), lambda i: (layer, 0, 0), pipeline_mode=pl.Buffered(1))]
    args = [x2d, w_in]
    if has_vres:
        in_specs.append(pl.BlockSpec((D_MODEL, 128), lambda i: (0, 0)))
        args.append(w_vres)
    return pl.pallas_call(
        functools.partial(_proj_kernel, has_vres),
        grid=(m // PROJ_TM,),
        in_specs=in_specs,
        out_specs=[pl.BlockSpec((PROJ_TM, w), lambda i: (i, 0)) for w in (RW_W, AT_W, SD_W, HG_W)],
        out_shape=outs,
        compiler_params=pltpu.CompilerParams(dimension_semantics=("parallel",),
                                             vmem_limit_bytes=VMEM_LIMIT),
    )(*args)


def _stack_heads(x, head_sel):
    xb = x.astype(BF16)
    zero = jnp.zeros_like(xb)
    return jnp.concatenate([jnp.where(m, xb, zero) for m in head_sel], axis=0)


def _rwkv_chunks(at, bh, kh, rt, v, head_sel):
    n = len(at)
    rng = range(n)
    T = at[0].shape[0]
    G = D_GROUP
    row = _iota((T, G), 0)
    lane_t = _iota((T, G), 1) % T
    strict = lane_t < row
    incl = lane_t <= row
    st = lambda x: _stack_heads(x, head_sel)
    lhs = [jnp.concatenate([at[j], rt[j]], axis=0).astype(BF16) for j in rng]
    ab = [_dot(lhs[j], st(bh[j]), _NT) for j in rng]
    ak = [_dot(lhs[j], st(kh[j]), _NT) for j in rng]
    n_u = [jnp.where(strict, ab[j][0:T], 0.0) for j in rng]
    a_ak = [jnp.where(strict, ak[j][0:T], 0.0) for j in rng]
    m_rb = [jnp.where(incl, ab[j][T:2 * T], 0.0).astype(BF16) for j in rng]
    m_rk = [jnp.where(incl, ak[j][T:2 * T], 0.0).astype(BF16) for j in rng]

    eye = jnp.where(lane_t == row, 1.0, 0.0)
    x_u = [eye + n_u[j] for j in rng]
    pw = [_dot(n_u[j].astype(BF16), st(n_u[j]), _NN) for j in rng]
    steps = T.bit_length() - 2
    for it in range(steps):
        w = [st(pw[j]) for j in rng]
        if it + 1 < steps:
            res = [_dot(jnp.concatenate([pw[j], x_u[j]], axis=0).astype(BF16), w[j], _NN) for j in rng]
            pw = [res[j][0:T] for j in rng]
            x_u = [x_u[j] + res[j][T:2 * T] for j in rng]
        else:
            x_u = [x_u[j] + _dot(x_u[j].astype(BF16), w[j], _NN) for j in rng]

    v_s = [st(v[j]) for j in rng]
    akv = [_dot(a_ak[j].astype(BF16), v_s[j], _NN) for j in rng]
    xb = [x_u[j].astype(BF16) for j in rng]
    p_u = [_dot(xb[j], st(at[j]), _NN) for j in rng]
    q_u = [_dot(xb[j], st(akv[j]), _NN) for j in rng]
    mkv = [_dot(m_rk[j], v_s[j], _NN) for j in rng]
    return p_u, q_u, m_rb, mkv


def _rwkv_kernel(has_vres, *refs):
    if has_vres:
        (f_ref, fprev_ref, vf_ref, mu_ref, vec_ref, w2_ref, a2_ref, g2_ref, v2_ref,
         y_ref, s_ref) = refs
    else:
        (f_ref, fprev_ref, mu_ref, vec_ref, w2_ref, a2_ref, g2_ref,
         y_ref, vout_ref, s_ref) = refs
    c = pl.program_id(1)
    T = RW_T
    G = D_GROUP

    @pl.when(c == 0)
    def _():
        s_ref[...] = jnp.zeros_like(s_ref)

    w0, a0, k_k, k_a = (vec_ref[i:i + 1, :] for i in range(4))
    lnx_w, lnx_b, r_k = (vec_ref[i:i + 1, :] for i in range(4, 7))
    head_sel = [(_iota((1, G), 1) // HEAD_DIM) == h for h in range(N_HEADS)]
    same = _same_head()
    bo = same.astype(BF16)
    ltri = (_iota((T, T), 1) <= _iota((T, T), 0)).astype(BF16)

    def prologue(b0, nb):
        R = RW_ROWS * nb
        nchunks = RW_NCH * nb
        f = f_ref[b0:b0 + nb].reshape(R, RW_W)
        row = _iota((R, 1), 0)
        shifted = pltpu.roll(f, 1, axis=0)
        for b in range(nb):
            prev_row = jnp.where(c == 0, 0.0, fprev_ref[b0 + b, 7:8, :])
            shifted = jnp.where(row == b * RW_ROWS, prev_row, shifted)
        x = f + (shifted - f) * mu_ref[...]
        r = x[:, 0:G]
        k = x[:, G:2 * G]
        v = x[:, 2 * G:3 * G]
        seg = x[:, 3 * G:3 * G + 128]
        p = w0 + _mm3(jnp.tanh(seg), w2_ref[...])
        lw = (-math.exp(-0.5) * math.log2(math.e)) * _sigmoid(p)
        a = _sigmoid(a0 + _mm(seg, a2_ref[...]))
        g = _mm(_sigmoid(seg), g2_ref[...])
        if has_vres:
            seg2 = x[:, 3 * G + 128:3 * G + 256]
            v0 = vec_ref[7:8, :]
            v = v + (vf_ref[b0:b0 + nb].reshape(R, G) - v) * _sigmoid(v0 + _mm(seg2, v2_ref[...]))
        else:
            vout_ref[b0:b0 + nb] = v.reshape(nb, RW_ROWS, G)
        kk = k * k_k
        kk = kk * lax.rsqrt(jnp.maximum(_mm(kk * kk, bo), 1e-24))
        k2 = k * (1.0 + (a - 1.0) * k_a)
        kka = kk * a
        cs = jnp.concatenate([_mmx_l(ltri, lw[j * T:(j + 1) * T], pieces=2) for j in range(nchunks)],
                             axis=0)
        c_last = jnp.concatenate(
            [jnp.broadcast_to(cs[(j + 1) * T - 1:(j + 1) * T, :], (T, G)) for j in range(nchunks)], axis=0)
        e_neg = jnp.exp2(-cs)
        e_dec = jnp.exp2(c_last - cs)
        chunks = lambda z: [z[j * T:(j + 1) * T] for j in range(nchunks)]
        return dict(b0=b0, nb=nb, nchunks=nchunks, r=r, k2=k2, v=v, g=g, c_last=c_last,
                    at=chunks(-kk * jnp.exp2(cs - lw)), bh=chunks(kka * e_neg), kh=chunks(k2 * e_neg),
                    rt=chunks(r * jnp.exp2(cs)), bdec=chunks(kka * e_dec), kdec=chunks(k2 * e_dec),
                    vc=chunks(v))

    def middle(ctx):
        n = ctx["nchunks"]
        p_u, q_u, m_rb, mkv = _rwkv_chunks(ctx["at"], ctx["bh"], ctx["kh"], ctx["rt"], ctx["vc"], head_sel)
        ctx.update(q_u=q_u, m_rb=m_rb, mkv=mkv)
        ctx["rp_lhs"] = [jnp.concatenate([ctx["rt"][i], p_u[i]], axis=0).astype(BF16) for i in range(n)]
        ctx["dec_t"] = [jnp.concatenate([ctx["bdec"][i], ctx["kdec"][i]], axis=0).T.astype(BF16)
                        for i in range(n)]
        ctx["w_col"] = [jnp.exp2(jnp.broadcast_to(ctx["c_last"][i * T:i * T + 1, :], (128, G))).T[:, 0:1]
                        for i in range(n)]

    def chain(ctx):
        b0, nb, n = ctx["b0"], ctx["nb"], ctx["nchunks"]
        s = [s_ref[b0 + b] for b in range(nb)]
        rs = [None] * n
        sas = [None] * n
        seqs = range(nb)
        for j in range(RW_NCH):
            ids = [b * RW_NCH + j for b in seqs]
            rp = [_dot(ctx["rp_lhs"][i], s[b].astype(BF16), _NN) for b, i in zip(seqs, ids)]
            for b, i in zip(seqs, ids):
                rs[i] = rp[b][0:T]
                sas[i] = rp[b][T:2 * T] + ctx["q_u"][i]
            upd = [_dot(ctx["dec_t"][i], jnp.concatenate([sas[i], ctx["vc"][i]], axis=0).astype(BF16), _NN)
                   for i in ids]
            s = [s[b] * ctx["w_col"][i] + jnp.where(same, upd[b], 0.0) for b, i in zip(seqs, ids)]
        outs = [rs[i] + _dot(ctx["m_rb"][i], _stack_heads(sas[i], head_sel), _NN) + ctx["mkv"][i]
                for i in range(n)]
        for b in seqs:
            s_ref[b0 + b] = s[b]
        ctx["o"] = jnp.concatenate(outs, axis=0)

    def epilogue(ctx):
        o = ctx["o"]
        mean = _mmx_r(o, bo, pieces=2) * (1.0 / HEAD_DIM)
        d = o - mean
        var = _mm(d * d, bo) * (1.0 / HEAD_DIM)
        yn = d * lax.rsqrt(var + RWKV_GN_EPS) * lnx_w + lnx_b
        bonus = _mm(ctx["r"] * ctx["k2"] * r_k, bo) * ctx["v"]
        y_ref[ctx["b0"]:ctx["b0"] + ctx["nb"]] = ((yn + bonus) * ctx["g"]).reshape(ctx["nb"], RW_ROWS, G)

    ctx = prologue(0, RW_NB)
    middle(ctx)
    chain(ctx)
    epilogue(ctx)


def _rwkv(rw, v_first, mu, vec, w2p, a2p, g2p, v2p, batch, seq):
    has_vres = v_first is not None
    nc = seq // RW_ROWS
    m = batch * seq
    row_spec = lambda w: pl.BlockSpec((RW_NB, RW_ROWS, w), lambda b, c: (b, c, 0))
    const = lambda shape: pl.BlockSpec(shape, lambda b, c: (0, 0))
    prev_spec = pl.BlockSpec(
        (RW_NB, 8, RW_W), lambda b, c: (b, jnp.maximum(c * (RW_ROWS // 8) - 1, 0), 0))
    rw3 = rw.reshape(batch, seq, RW_W)
    in_specs = [row_spec(RW_W), prev_spec]
    args = [rw3, rw3]
    if has_vres:
        in_specs.append(row_spec(D_GROUP))
        args.append(v_first.reshape(batch, seq, D_GROUP))
    in_specs += [const((1, RW_W)), const((8, D_GROUP)), const((128, D_GROUP)),
                 const((128, D_GROUP)), const((128, D_GROUP))]
    args += [mu, vec, w2p, a2p, g2p]
    if has_vres:
        in_specs.append(const((128, D_GROUP)))
        args.append(v2p)
    out_shape = [jax.ShapeDtypeStruct((batch, seq, D_GROUP), F32)]
    out_specs = [row_spec(D_GROUP)]
    if not has_vres:
        out_shape.append(jax.ShapeDtypeStruct((batch, seq, D_GROUP), F32))
        out_specs.append(row_spec(D_GROUP))
    outs = pl.pallas_call(
        functools.partial(_rwkv_kernel, has_vres),
        grid=(batch // RW_NB, nc),
        in_specs=in_specs,
        out_specs=out_specs,
        out_shape=out_shape,
        scratch_shapes=[pltpu.VMEM((RW_NB, D_GROUP, D_GROUP), F32)],
        compiler_params=pltpu.CompilerParams(dimension_semantics=("parallel", "arbitrary"),
                                             vmem_limit_bytes=VMEM_LIMIT),
    )(*args)
    y = outs[0].reshape(m, D_GROUP)
    if has_vres:
        return y, v_first
    return y, outs[1].reshape(m, D_GROUP)


def _attn_kernel(qkv_ref, o_ref, perm_ref, acc_ref, l_ref, m_ref, bias_ref):
    G = D_GROUP
    blk = ATTN_BLK
    seq = qkv_ref.shape[0]
    nres = DILATIONS[-1]
    sub = seq // nres
    head_sel = [(_iota((1, G), 1) // HEAD_DIM) == h for h in range(N_HEADS)]
    scale = HEAD_DIM ** -0.5 * math.log2(math.e)

    def jmap(a, d):
        run = blk * d // nres
        return (nres // d) * (a % run) + a // run

    to_perm = (_iota((blk, blk), 1) == jmap(_iota((blk, blk), 0), 1)).astype(BF16)
    from_perm = (_iota((blk, blk), 0) == jmap(_iota((blk, blk), 1), 1)).astype(BF16)
    run1 = blk // nres
    col_scale = jnp.where(_iota((1, AT_W), 1) < G, scale, 1.0)

    def permute(n, carry):
        r0 = pl.multiple_of(n * blk, blk)
        pb = jnp.dot(to_perm, (qkv_ref[pl.ds(r0, blk), :] * col_scale).astype(BF16),
                     preferred_element_type=F32)
        for e in range(nres):
            dst = pl.multiple_of(e * sub + n * run1, run1)
            perm_ref[pl.ds(dst, run1), :] = pb[e * run1:(e + 1) * run1, :]
        return carry

    lax.fori_loop(0, seq // blk, permute, 0, unroll=4)

    branches = tuple(reversed(DILATIONS))
    for bi, d in enumerate(branches):
        jq = jmap(_iota((blk, 2 * blk), 0), d)
        kb = _iota((blk, 2 * blk), 1)
        dist = blk + jq - (jmap(kb % blk, d) + (kb // blk) * blk)
        in_band = (dist >= 0) & (dist <= blk)
        for h in range(N_HEADS):
            bias_ref[bi * N_HEADS + h] = jnp.where(
                in_band, dist.astype(F32) * (-ALIBI_SLOPES[h] * d * math.log2(math.e)), -jnp.inf)

    for bi, d in enumerate(branches):
        runs = nres // d
        run = blk // runs
        nb = seq // (d * blk)
        first = bi == 0
        last = bi == len(branches) - 1

        def blocks(items, with_prev, bi=bi, d=d, runs=runs, run=run, first=first, last=last):
            its = range(len(items))
            heads = range(N_HEADS)

            def starts(rho, nn):
                return [pl.multiple_of((rho + d * e) * sub + nn * run, run) for e in range(runs)]

            def gather(st, c0):
                return jnp.concatenate([perm_ref[pl.ds(s0, run), c0:c0 + G] for s0 in st],
                                       axis=0).astype(BF16)

            cur = [starts(rho, n) for rho, n in items]
            qb = [gather(cur[i], 0) for i in its]
            if with_prev:
                prv = [starts(rho, n - 1) for rho, n in items]
                kcat = [jnp.concatenate([gather(prv[i], G), gather(cur[i], G)], axis=0) for i in its]
                vcat = [jnp.concatenate([gather(prv[i], 2 * G), gather(cur[i], 2 * G)], axis=0)
                        for i in its]
            else:
                kcat = [gather(cur[i], G) for i in its]
                vcat = [gather(cur[i], 2 * G) for i in its]
            zq = jnp.zeros_like(qb[0])
            kc0 = 0 if with_prev else blk
            pairs = [(0, 1), (2, 3)]
            s_p = [[_dot(jnp.concatenate([jnp.where(head_sel[h], qb[i], zq) for h in pr], axis=0),
                         kcat[i], _NT) for pr in pairs] for i in its]
            s = [[s_p[i][h // 2][(h % 2) * blk:(h % 2 + 1) * blk]
                  + bias_ref[bi * N_HEADS + h, :, kc0:2 * blk] for h in heads] for i in its]
            mh = [[jnp.max(s[i][h], axis=-1, keepdims=True) for h in heads] for i in its]
            ph = [[jnp.exp2(s[i][h] - mh[i][h]) for h in heads] for i in its]
            lh = [[jnp.sum(ph[i][h], axis=-1, keepdims=True) for h in heads] for i in its]
            pv_p = [[_dot(jnp.concatenate([ph[i][h].astype(BF16) for h in pr], axis=0), vcat[i], _NN)
                     for pr in pairs] for i in its]
            def by_head(parts):
                out = jnp.broadcast_to(parts[N_HEADS - 1], (blk, G))
                for h in reversed(range(N_HEADS - 1)):
                    out = jnp.where(head_sel[h], parts[h], out)
                return out

            acc_b = [by_head([pv_p[i][h // 2][(h % 2) * blk:(h % 2 + 1) * blk] for h in heads])
                     for i in its]
            l_b = [by_head(lh[i]) for i in its]
            m_b = [by_head(mh[i]) for i in its]
            for i in its:
                outs = []
                for e, s0 in enumerate(cur[i]):
                    idx = pl.ds(s0, run)
                    sl = slice(e * run, (e + 1) * run)
                    if first:
                        acc_ref[idx, :] = acc_b[i][sl]
                        l_ref[idx, :] = l_b[i][sl]
                        m_ref[idx, :] = m_b[i][sl]
                    else:
                        m_old = m_ref[idx, :]
                        m_new = jnp.maximum(m_old, m_b[i][sl])
                        w_old = jnp.exp2(m_old - m_new)
                        w_new = jnp.exp2(m_b[i][sl] - m_new)
                        acc_n = acc_ref[idx, :] * w_old + acc_b[i][sl] * w_new
                        l_n = l_ref[idx, :] * w_old + l_b[i][sl] * w_new
                        if last:
                            outs.append(acc_n / l_n)
                        else:
                            acc_ref[idx, :] = acc_n
                            l_ref[idx, :] = l_n
                            m_ref[idx, :] = m_new
                if last:
                    r0 = pl.multiple_of(items[i][1] * blk, blk)
                    o_ref[pl.ds(r0, blk), :] = _mmx_l(from_perm, jnp.concatenate(outs, axis=0))

        def run_blocks(count, index, with_prev, blocks=blocks):
            grp = ATTN_GROUP - 1 if with_prev else ATTN_GROUP

            def group(it, carry):
                blocks([index(grp * it + u) for u in range(grp)], with_prev)
                return carry

            if count >= grp:
                lax.fori_loop(0, count // grp, group, 0)
            rest = count % grp
            if rest:
                blocks([index(count - rest + u) for u in range(rest)], with_prev)

        run_blocks(d, lambda i: (i, 0), False)
        if nb > 1:
            run_blocks(d * (nb - 1), lambda i, nb=nb: (i // (nb - 1), i % (nb - 1) + 1), True)


def _attention(at, batch, seq):
    m = batch * seq
    return pl.pallas_call(
        _attn_kernel,
        grid=(batch,),
        in_specs=[pl.BlockSpec((seq, AT_W), lambda b: (b, 0))],
        out_specs=pl.BlockSpec((seq, D_GROUP), lambda b: (b, 0)),
        out_shape=jax.ShapeDtypeStruct((m, D_GROUP), F32),
        scratch_shapes=[pltpu.VMEM((seq, AT_W), F32)] + [pltpu.VMEM((seq, D_GROUP), F32)] * 3
        + [pltpu.VMEM((len(DILATIONS) * N_HEADS, ATTN_BLK, 2 * ATTN_BLK), F32)],
        compiler_params=pltpu.CompilerParams(dimension_semantics=("parallel",),
                                             vmem_limit_bytes=VMEM_LIMIT),
    )(at)


def _ssd_kernel(f_ref, fprev_ref, cw_ref, cb_ref, dtb_ref, vec_ref, ex_ref, y_ref, st_ref):
    c = pl.program_id(1)
    T = SSD_T
    G = D_GROUP
    N = SSD_STATE
    nb = SSD_NB
    R = nb * T
    seqs = range(nb)

    @pl.when(c == 0)
    def _():
        st_ref[...] = jnp.zeros_like(st_ref)

    f = f_ref[...].reshape(R, SD_W)
    xbc_raw = f[:, G:G + SSD_XBC]
    prev8 = [jnp.where(c == 0, 0.0, fprev_ref[b, :, G:G + SSD_XBC]) for b in seqs]
    row8 = _iota((8, 1), 0)
    conv = xbc_raw * cw_ref[3:4, :] + cb_ref[...]
    for j in range(1, 4):
        rolled = pltpu.roll(xbc_raw, j, axis=0)
        pieces = []
        for b in seqs:
            pieces.append(jnp.where(row8 < j, pltpu.roll(prev8[b], j, axis=0), rolled[b * T:b * T + 8]))
            pieces.append(rolled[b * T + 8:(b + 1) * T])
        conv = conv + jnp.concatenate(pieces, axis=0) * cw_ref[3 - j:4 - j, :]
    xbc = _silu(conv)
    xs = xbc[:, 0:G]
    bm = xbc[:, G:G + 2 * N]
    cm = xbc[:, G + 2 * N:G + 4 * N]
    z = f[:, 0:G]

    a_dense, d_dense, norm_w = (vec_ref[i:i + 1, :] for i in range(3))
    dt = _softplus(f[:, G + SSD_XBC:G + SSD_XBC + 128] + dtb_ref[...])
    dt_dense = _mmx_r(dt, ex_ref[...], pieces=2)
    da = dt_dense * a_dense
    xdt = xs * dt_dense
    ltri = (_iota((T, T), 1) <= _iota((T, T), 0))
    ltri_b = ltri.astype(BF16)
    umat = (_iota((T, T), 0) > _iota((T, T), 1)).astype(F32)
    masks = _head_masks()
    rows = [slice(b * T, (b + 1) * T) for b in seqs]

    cs = [_mmx_l(ltri_b, da[rows[b]], pieces=2) for b in seqs]
    scores = [[_mm(cm[rows[b], g * N:(g + 1) * N], bm[rows[b], g * N:(g + 1) * N], _NT)
               for g in range(2)] for b in seqs]
    y_diag = []
    for b in seqs:
        acc = None
        for h in range(N_HEADS):
            da_col = jnp.broadcast_to(da[rows[b], h * HEAD_DIM:h * HEAD_DIM + 1], (T, T))
            seg = _mmx_l(ltri_b, da_col * umat, pieces=2)
            dec = jnp.where(ltri, jnp.exp2(seg), 0.0)
            term = _mm(scores[b][h // 2] * dec, xdt[rows[b]] * masks[h])
            acc = term if acc is None else acc + term
        y_diag.append(acc)

    ys = []
    for b in seqs:
        st = st_ref[b]
        cmb = cm[rows[b]]
        y_off = jnp.concatenate([_mm(cmb[:, 0:N], st[:, 0:N]), _mm(cmb[:, N:2 * N], st[:, N:2 * N])],
                                axis=1) * jnp.exp2(cs[b])
        cs_last = cs[b][T - 1:T, :]
        xd = xdt[rows[b]] * jnp.exp2(cs_last - cs[b])
        bmb = bm[rows[b]]
        st_ref[b] = st * jnp.exp2(cs_last) + jnp.concatenate(
            [_mm(bmb[:, 0:N], xd[:, 0:N], _TN), _mm(bmb[:, N:2 * N], xd[:, N:2 * N], _TN)], axis=1)
        ys.append(y_diag[b] + y_off)

    y = (jnp.concatenate(ys, axis=0) + xs * d_dense) * _silu(z)
    halves = []
    for g in range(2):
        yg = y[:, g * N:(g + 1) * N]
        halves.append(yg * lax.rsqrt(jnp.mean(yg * yg, axis=-1, keepdims=True) + RMS_EPS))
    y_ref[...] = (jnp.concatenate(halves, axis=1) * norm_w).reshape(nb, T, G)


def _ssd(sd, cw, cb, dtb, vec, ex, batch, seq):
    nc = seq // SSD_T
    m = batch * seq
    const = lambda shape: pl.BlockSpec(shape, lambda b, c: (0, 0))
    sd3 = sd.reshape(batch, seq, SD_W)
    out = pl.pallas_call(
        _ssd_kernel,
        grid=(batch // SSD_NB, nc),
        in_specs=[pl.BlockSpec((SSD_NB, SSD_T, SD_W), lambda b, c: (b, c, 0)),
                  pl.BlockSpec((SSD_NB, 8, SD_W),
                               lambda b, c: (b, jnp.maximum(c * (SSD_T // 8) - 1, 0), 0)),
                  const((4, SSD_XBC)), const((1, SSD_XBC)), const((1, 128)), const((8, D_GROUP)),
                  const((128, D_GROUP))],
        out_specs=pl.BlockSpec((SSD_NB, SSD_T, D_GROUP), lambda b, c: (b, c, 0)),
        out_shape=jax.ShapeDtypeStruct((batch, seq, D_GROUP), F32),
        scratch_shapes=[pltpu.VMEM((SSD_NB, SSD_STATE, D_GROUP), F32)],
        compiler_params=pltpu.CompilerParams(dimension_semantics=("parallel", "arbitrary"),
                                             vmem_limit_bytes=VMEM_LIMIT),
    )(sd3, sd3, cw, cb, dtb, vec, ex)
    return out.reshape(m, D_GROUP)


def _hgrn_kernel(f_ref, vec_ref, y_ref, st_ref):
    c = pl.program_id(1)
    T = HG_T
    C = HG_C
    G = D_GROUP
    nblk = T // C

    @pl.when(c == 0)
    def _():
        st_ref[...] = jnp.zeros_like(st_ref)

    lb = vec_ref[0:1, :]
    norm_w = vec_ref[1:2, :]
    forget = lb + (1.0 - lb) * _sigmoid_tail(f_ref[:, G:2 * G])
    q = _silu(f_ref[:, 0:G])
    k = 1.0 - forget
    v = f_ref[:, 2 * G:3 * G]
    grp = 4 * C
    blk_tri = ((_iota((grp, grp), 0) // C == _iota((grp, grp), 1) // C)
               & (_iota((grp, grp), 1) <= _iota((grp, grp), 0))).astype(BF16)
    log_f = jnp.log(forget)
    b = jnp.concatenate([_mmx_l(blk_tri, log_f[r0:r0 + grp], pieces=2) for r0 in range(0, T, grp)], axis=0)
    b2 = b * math.log2(math.e)
    c2 = b2 - jnp.log2(k)

    head_sel = [(_iota((1, G), 1) // HEAD_DIM) == h for h in range(N_HEADS)]
    bo = _same_head().astype(BF16)
    t_idx = _iota((C, 1), 0)
    blocks = lambda z: [z[i * C:(i + 1) * C] for i in range(nblk)]
    bq, qq, kq, vv, b2q, c2q = blocks(b), blocks(q), blocks(k), blocks(v), blocks(b2), blocks(c2)
    b_last = [bq[i][C - 1:C, :] for i in range(nblk)]

    upd = [_dot(_stack_heads(vv[i], head_sel), _stack_heads(kq[i] * jnp.exp(b_last[i] - bq[i]), head_sel),
                _TN) for i in range(nblk)]
    q_in = [qq[i] * jnp.exp(bq[i]) for i in range(nblk)]
    half = C // 2
    o_intra = []
    for i in range(nblk):
        parts = []
        for s in range(C):
            lo = 0 if s < half else half
            e = jnp.exp2(b2q[i][lo:C] - c2q[i][s:s + 1, :])
            parts.append(jnp.where(t_idx[lo:C] >= s, qq[i][lo:C] * e, 0.0))
        z = _mm(jnp.concatenate(parts, axis=0), bo)
        o_lo = z[0:half] * vv[i][0:1, :]
        o_hi = z[half:C] * vv[i][0:1, :]
        for s in range(1, half):
            o_lo = o_lo + z[s * C:s * C + half] * vv[i][s:s + 1, :]
            o_hi = o_hi + z[s * C + half:(s + 1) * C] * vv[i][s:s + 1, :]
        for s in range(half, C):
            r0 = half * C + (s - half) * half
            o_hi = o_hi + z[r0:r0 + half] * vv[i][s:s + 1, :]
        o_intra.append(jnp.concatenate([o_lo, o_hi], axis=0))

    st = st_ref[...]
    outs = []
    for i in range(nblk):
        outs.append(_mm(q_in[i], st, _NT) + o_intra[i])
        st = st * jnp.exp(b_last[i]) + upd[i]
    st_ref[...] = st
    o = jnp.concatenate(outs, axis=0)
    ms = _mmx_r(o * o, bo, pieces=2) * (1.0 / HEAD_DIM)
    y_ref[...] = o * lax.rsqrt(ms + RMS_EPS) * norm_w * _silu(f_ref[:, 3 * G:4 * G])


def _hgrn(hg, vec, batch, seq):
    nc = seq // HG_T
    m = batch * seq
    return pl.pallas_call(
        _hgrn_kernel,
        grid=(batch, nc),
        in_specs=[pl.BlockSpec((HG_T, HG_W), lambda b, c: (b * nc + c, 0)),
                  pl.BlockSpec((8, D_GROUP), lambda b, c: (0, 0))],
        out_specs=pl.BlockSpec((HG_T, D_GROUP), lambda b, c: (b * nc + c, 0)),
        out_shape=jax.ShapeDtypeStruct((m, D_GROUP), F32),
        scratch_shapes=[pltpu.VMEM((D_GROUP, D_GROUP), F32)],
        compiler_params=pltpu.CompilerParams(dimension_semantics=("parallel", "arbitrary"),
                                             vmem_limit_bytes=VMEM_LIMIT),
    )(hg, vec)


def _post_kernel(x_ref, ya_ref, yb_ref, yc_ref, yd_ref, wo_ref, wu_ref, wd_ref, ln_ref, o_ref):
    half = POST_TM // 2
    rows = [slice(0, half), slice(half, POST_TM)]

    def mixed(r):
        acc = ALPHA * x_ref[r, :]
        for i, ref in enumerate((ya_ref, yb_ref, yc_ref, yd_ref)):
            acc = acc + jnp.dot(ref[r, :].astype(BF16), wo_ref[i * D_GROUP:(i + 1) * D_GROUP, :],
                                preferred_element_type=F32)
        return acc

    def ffn_chunk(xb, acc, c0):
        h = jnp.maximum(jnp.dot(xb, wu_ref[:, c0:c0 + FF_CHUNK], preferred_element_type=F32), 0.0)
        return acc + jnp.dot((h * h).astype(BF16), wd_ref[c0:c0 + FF_CHUNK, :],
                             preferred_element_type=F32)

    chunks = list(range(0, D_FF, FF_CHUNK))
    pre_a = mixed(rows[0])
    x1a = _layer_norm(pre_a, ln_ref[0:1, :], ln_ref[1:2, :])
    pre_b = mixed(rows[1])
    xba = x1a.astype(BF16)
    acc_a = ALPHA * x1a
    for c0 in chunks[:2]:
        acc_a = ffn_chunk(xba, acc_a, c0)
    x1b = _layer_norm(pre_b, ln_ref[0:1, :], ln_ref[1:2, :])
    for c0 in chunks[2:]:
        acc_a = ffn_chunk(xba, acc_a, c0)
    xbb = x1b.astype(BF16)
    acc_b = ALPHA * x1b
    for c0 in chunks[:2]:
        acc_b = ffn_chunk(xbb, acc_b, c0)
    o_ref[rows[0], :] = _layer_norm(acc_a, ln_ref[2:3, :], ln_ref[3:4, :])
    for c0 in chunks[2:]:
        acc_b = ffn_chunk(xbb, acc_b, c0)
    o_ref[rows[1], :] = _layer_norm(acc_b, ln_ref[2:3, :], ln_ref[3:4, :])


def _post(x2d, ys, wo, wu, wd, ln):
    m = x2d.shape[0]
    row = lambda w: pl.BlockSpec((POST_TM, w), lambda i: (i, 0))
    res = lambda shape: pl.BlockSpec(shape, lambda i: (0, 0), pipeline_mode=pl.Buffered(1))
    return pl.pallas_call(
        _post_kernel,
        grid=(m // POST_TM,),
        in_specs=[row(D_MODEL)] + [row(D_GROUP)] * 4
        + [res((D_MODEL, D_MODEL)), res((D_MODEL, D_FF)), res((D_FF, D_MODEL)), res((8, D_MODEL))],
        out_specs=row(D_MODEL),
        out_shape=jax.ShapeDtypeStruct((m, D_MODEL), F32),
        compiler_params=pltpu.CompilerParams(dimension_semantics=("parallel",),
                                             vmem_limit_bytes=VMEM_LIMIT),
    )(x2d, *ys, wo, wu, wd, ln)


def _pad_cols(a, width):
    return jnp.pad(a, ((0, 0), (0, width - a.shape[1])))


def _pad_rows(a, top, total):
    return jnp.pad(a, ((top, total - top - a.shape[0]), (0, 0)))


def _rep_heads(v):
    return jnp.repeat(v.astype(F32), HEAD_DIM)[None, :]


def _rows8(rows, width):
    out = jnp.concatenate([r.reshape(1, width).astype(F32) for r in rows], axis=0)
    return jnp.pad(out, ((0, 8 - out.shape[0]), (0, 0)))


def kernel(x, lower_bounds, w_in, w_in_vres, mu_shift, mu_vres, rwkv_w0, rwkv_w2, rwkv_a0, rwkv_a2, rwkv_g2, rwkv_k_k, rwkv_k_a, rwkv_r_k, rwkv_lnx_w, rwkv_lnx_b, rwkv_v0, rwkv_v2, ssd_conv_w, ssd_conv_b, ssd_dt_bias, ssd_A_log, ssd_D, ssd_norm_w, hgrn_norm_w, w_out, ln1_w, ln1_b, w_up, w_down, ln2_w, ln2_b):
    batch, seq, _ = x.shape
    lb = jax.nn.softmax(lower_bounds.astype(F32), axis=0)
    lb = jnp.cumsum(lb, axis=0) - lb[0]
    expand = (jnp.arange(128)[:, None] == (jnp.arange(D_GROUP)[None, :] // HEAD_DIM)).astype(BF16)

    h = x.reshape(batch * seq, D_MODEL)
    v_first = None
    for l in range(DEPTH):
        mu = mu_shift[l][None, :]
        w_vres = None
        if l > 0:
            w_vres = _pad_cols(w_in_vres[l - 1], 128).astype(BF16)
            mu = jnp.concatenate([mu, mu_vres[l - 1][None, :]], axis=1)
        rw, at, sd, hg = _project(h, w_in, l, w_vres)

        vec_rows = [rwkv_w0[l], rwkv_a0[l], rwkv_k_k[l], rwkv_k_a[l], rwkv_lnx_w[l], rwkv_lnx_b[l],
                    rwkv_r_k[l].reshape(-1)]
        if l > 0:
            vec_rows.append(rwkv_v0[l - 1])
        y_a, v_first = _rwkv(
            rw, v_first, _pad_cols(mu, RW_W), _rows8(vec_rows, D_GROUP),
            _pad_rows(rwkv_w2[l], 0, 128), _pad_rows(rwkv_a2[l], 32, 128), _pad_rows(rwkv_g2[l], 64, 128),
            _pad_rows(rwkv_v2[l - 1], 0, 128) if l > 0 else None, batch, seq)
        y_b = _attention(at, batch, seq)
        ssd_vec = _rows8([_rep_heads(-jnp.exp(ssd_A_log[l].astype(F32)) * math.log2(math.e)), _rep_heads(ssd_D[l]),
                          ssd_norm_w[l]], D_GROUP)
        y_c = _ssd(sd, ssd_conv_w[l].astype(F32), ssd_conv_b[l][None, :],
                   _pad_cols(ssd_dt_bias[l][None, :], 128), ssd_vec, expand, batch, seq)
        y_d = _hgrn(hg, _rows8([lb[l], hgrn_norm_w[l]], D_GROUP), batch, seq)
        ln = _rows8([ln1_w[l], ln1_b[l], ln2_w[l], ln2_b[l]], D_MODEL)
        h = _post(h, (y_a, y_b, y_c, y_d), w_out[l].astype(BF16), w_up[l].astype(BF16),
                  w_down[l].astype(BF16), ln)
    return h.reshape(batch, seq, D_MODEL)
```

```python
import functools
import math

import jax
import jax.numpy as jnp
from jax import lax
from jax.experimental import pallas as pl
from jax.experimental.pallas import tpu as pltpu

F32 = jnp.float32
BF16 = jnp.bfloat16

D_MODEL = 1024
DEPTH = 2
D_GROUP = 256
HEAD_DIM = 64
N_HEADS = 4
D_FF = 4 * D_MODEL
ALPHA = (2.0 * DEPTH) ** 0.25
LN_EPS = 1e-5
RMS_EPS = 1e-5
RWKV_GN_EPS = HEAD_DIM * 1e-5
RWKV_COLS = 896
ATTN_COLS = 768
SSD_STATE = 128
SSD_XBC = 768
SSD_COLS = 1028
HGRN_COLS = 1024
DILATIONS = (1, 4, 16)
ATTN_BLK = 128
ATTN_GROUP = 4
ALIBI_SLOPES = tuple(2.0 ** (-8.0 * (h + 1) / N_HEADS) for h in range(N_HEADS))

RW_W, AT_W, SD_W, HG_W = 1024, 768, 1152, 1024

RW_T = 64
RW_NCH = 2
RW_NB = 8
RW_ROWS = RW_T * RW_NCH
SSD_T = 128
SSD_NB = 8
HG_C = 16
HG_T = 512
PROJ_TM = 512
POST_TM = 1024
FF_CHUNK = 512

V7X_VMEM_BYTES = 64 * 1024 * 1024
VMEM_LIMIT = V7X_VMEM_BYTES - 8 * 1024 * 1024


def _dot(a, b, dims):
    return lax.dot_general(a, b, (dims, ((), ())), preferred_element_type=F32)


_NN = ((1,), (0,))
_NT = ((1,), (1,))
_TN = ((0,), (0,))


def _mm(a, b, dims=_NN):
    return _dot(a.astype(BF16), b.astype(BF16), dims)


def _split(x, pieces):
    out = []
    for i in range(pieces):
        hi = x.astype(BF16)
        out.append(hi)
        if i + 1 < pieces:
            x = x - hi.astype(F32)
    return out


def _mmx_l(m01, x, dims=_NN, pieces=3):
    return sum(_dot(m01, xp, dims) for xp in _split(x, pieces))


def _mmx_r(x, m01, dims=_NN, pieces=3):
    return sum(_dot(xp, m01, dims) for xp in _split(x, pieces))


def _mm3(a, b, dims=_NN):
    ah = a.astype(BF16)
    al = (a - ah.astype(F32)).astype(BF16)
    bh = b.astype(BF16)
    bl = (b - bh.astype(F32)).astype(BF16)
    return _dot(ah, bh, dims) + _dot(ah, bl, dims) + _dot(al, bh, dims)


def _iota(shape, dim):
    return lax.broadcasted_iota(jnp.int32, shape, dim)


def _head_masks(width=D_GROUP):
    lane = _iota((1, width), 1) // HEAD_DIM
    return [(lane == h).astype(F32) for h in range(N_HEADS)]


def _same_head(n=D_GROUP):
    return (_iota((n, n), 0) // HEAD_DIM) == (_iota((n, n), 1) // HEAD_DIM)


def _sigmoid(x):
    return 0.5 * jnp.tanh(0.5 * x) + 0.5


def _sigmoid_tail(x):
    return 1.0 / (1.0 + jnp.exp(-x))


def _silu(x):
    return x * _sigmoid(x)


def _softplus(x):
    return jnp.maximum(x, 0.0) + jnp.log(1.0 + jnp.exp(-jnp.abs(x)))


def _layer_norm(x, w, b):
    mu = jnp.mean(x, axis=-1, keepdims=True)
    d = x - mu
    var = jnp.mean(d * d, axis=-1, keepdims=True)
    return d * lax.rsqrt(var + LN_EPS) * w + b


def _proj_kernel(has_vres, *refs):
    if has_vres:
        x_ref, w_ref, wv_ref, rw_ref, at_ref, sd_ref, hg_ref = refs
    else:
        x_ref, w_ref, rw_ref, at_ref, sd_ref, hg_ref = refs
    xb = x_ref[...].astype(BF16)
    n_in = w_ref.shape[1]

    def emit(ref, c_out, c_in, width):
        wb = w_ref[:, c_in:c_in + width].astype(BF16)
        ref[:, c_out:c_out + width] = jnp.dot(xb, wb, preferred_element_type=F32)

    src = 0
    for ref, cols, padded in ((rw_ref, RWKV_COLS, RW_W), (at_ref, ATTN_COLS, AT_W),
                              (sd_ref, SSD_COLS, SD_W), (hg_ref, HGRN_COLS, HG_W)):
        for c0 in range(0, padded, 256):
            width = min(256, padded - c0)
            if c0 + width <= cols:
                emit(ref, c0, src + c0, width)
                continue
            real = max(cols - c0, 0)
            real_pad = -(-real // 128) * 128
            if has_vres and ref is rw_ref and real == real_pad and real_pad + wv_ref.shape[1] == width:
                wb = jnp.concatenate([w_ref[:, src + c0:src + c0 + real].astype(BF16), wv_ref[...]], axis=1)
                ref[:, c0:c0 + width] = jnp.dot(xb, wb, preferred_element_type=F32)
                continue
            if real:
                take = min(real_pad, n_in - (src + c0))
                wb = w_ref[:, src + c0:src + c0 + take].astype(BF16)
                lane = _iota((1, take), 1)
                wb = jnp.where(lane < real, wb, jnp.zeros_like(wb))
                ref[:, c0:c0 + take] = jnp.dot(xb, wb, preferred_element_type=F32)
                if take < real_pad:
                    ref[:, c0 + take:c0 + real_pad] = jnp.zeros((xb.shape[0], real_pad - take), F32)
            rest = width - real_pad
            if rest:
                if has_vres and ref is rw_ref:
                    ref[:, c0 + real_pad:c0 + width] = jnp.dot(xb, wv_ref[...], preferred_element_type=F32)
                else:
                    ref[:, c0 + real_pad:c0 + width] = jnp.zeros((xb.shape[0], rest), F32)
        src += cols


def _project(x2d, w_in, layer, w_vres):
    m = x2d.shape[0]
    n_in = w_in.shape[2]
    has_vres = w_vres is not None
    outs = [jax.ShapeDtypeStruct((m, w), F32) for w in (RW_W, AT_W, SD_W, HG_W)]
    in_specs = [pl.BlockSpec((PROJ_TM, D_MODEL), lambda i: (i, 0)),
                pl.BlockSpec((None, D_MODEL, n_in), lambda i: (layer, 0, 0), pipeline_mode=pl.Buffered(1))]
    args = [x2d, w_in]
    if has_vres:
        in_specs.append(pl.BlockSpec((D_MODEL, 128), lambda i: (0, 0)))
        args.append(w_vres)
    return pl.pallas_call(
        functools.partial(_proj_kernel, has_vres),
        grid=(m // PROJ_TM,),
        in_specs=in_specs,
        out_specs=[pl.BlockSpec((PROJ_TM, w), lambda i: (i, 0)) for w in (RW_W, AT_W, SD_W, HG_W)],
        out_shape=outs,
        compiler_params=pltpu.CompilerParams(dimension_semantics=("parallel",),
                                             vmem_limit_bytes=VMEM_LIMIT),
    )(*args)


def _stack_heads(x, head_sel):
    xb = x.astype(BF16)
    zero = jnp.zeros_like(xb)
    return jnp.concatenate([jnp.where(m, xb, zero) for m in head_sel], axis=0)


def _rwkv_chunks(at, bh, kh, rt, v, head_sel):
    n = len(at)
    rng = range(n)
    T = at[0].shape[0]
    G = D_GROUP
    row = _iota((T, G), 0)
    lane_t = _iota((T, G), 1) % T
    strict = lane_t < row
    incl = lane_t <= row
    st = lambda x: _stack_heads(x, head_sel)
    lhs = [jnp.concatenate([at[j], rt[j]], axis=0).astype(BF16) for j in rng]
    ab = [_dot(lhs[j], st(bh[j]), _NT) for j in rng]
    ak = [_dot(lhs[j], st(kh[j]), _NT) for j in rng]
    n_u = [jnp.where(strict, ab[j][0:T], 0.0) for j in rng]
    a_ak = [jnp.where(strict, ak[j][0:T], 0.0) for j in rng]
    m_rb = [jnp.where(incl, ab[j][T:2 * T], 0.0).astype(BF16) for j in rng]
    m_rk = [jnp.where(incl, ak[j][T:2 * T], 0.0).astype(BF16) for j in rng]

    eye = jnp.where(lane_t == row, 1.0, 0.0)
    x_u = [eye + n_u[j] for j in rng]
    pw = [_dot(n_u[j].astype(BF16), st(n_u[j]), _NN) for j in rng]
    steps = T.bit_length() - 2
    for it in range(steps):
        w = [st(pw[j]) for j in rng]
        if it + 1 < steps:
            res = [_dot(jnp.concatenate([pw[j], x_u[j]], axis=0).astype(BF16), w[j], _NN) for j in rng]
            pw = [res[j][0:T] for j in rng]
            x_u = [x_u[j] + res[j][T:2 * T] for j in rng]
        else:
            x_u = [x_u[j] + _dot(x_u[j].astype(BF16), w[j], _NN) for j in rng]

    v_s = [st(v[j]) for j in rng]
    akv = [_dot(a_ak[j].astype(BF16), v_s[j], _NN) for j in rng]
    xb = [x_u[j].astype(BF16) for j in rng]
    p_u = [_dot(xb[j], st(at[j]), _NN) for j in rng]
    q_u = [_dot(xb[j], st(akv[j]), _NN) for j in rng]
    mkv = [_dot(m_rk[j], v_s[j], _NN) for j in rng]
    return p_u, q_u, m_rb, mkv


def _rwkv_kernel(has_vres, *refs):
    if has_vres:
        (f_ref, fprev_ref, vf_ref, mu_ref, vec_ref, w2_ref, a2_ref, g2_ref, v2_ref,
         y_ref, s_ref) = refs
    else:
        (f_ref, fprev_ref, mu_ref, vec_ref, w2_ref, a2_ref, g2_ref,
         y_ref, vout_ref, s_ref) = refs
    c = pl.program_id(1)
    T = RW_T
    G = D_GROUP

    @pl.when(c == 0)
    def _():
        s_ref[...] = jnp.zeros_like(s_ref)

    w0, a0, k_k, k_a = (vec_ref[i:i + 1, :] for i in range(4))
    lnx_w, lnx_b, r_k = (vec_ref[i:i + 1, :] for i in range(4, 7))
    head_sel = [(_iota((1, G), 1) // HEAD_DIM) == h for h in range(N_HEADS)]
    same = _same_head()
    bo = same.astype(BF16)
    ltri = (_iota((T, T), 1) <= _iota((T, T), 0)).astype(BF16)

    def prologue(b0, nb):
        R = RW_ROWS * nb
        nchunks = RW_NCH * nb
        f = f_ref[b0:b0 + nb].reshape(R, RW_W)
        row = _iota((R, 1), 0)
        shifted = pltpu.roll(f, 1, axis=0)
        for b in range(nb):
            prev_row = jnp.where(c == 0, 0.0, fprev_ref[b0 + b, 7:8, :])
            shifted = jnp.where(row == b * RW_ROWS, prev_row, shifted)
        x = f + (shifted - f) * mu_ref[...]
        r = x[:, 0:G]
        k = x[:, G:2 * G]
        v = x[:, 2 * G:3 * G]
        seg = x[:, 3 * G:3 * G + 128]
        p = w0 + _mm3(jnp.tanh(seg), w2_ref[...])
        lw = (-math.exp(-0.5) * math.log2(math.e)) * _sigmoid(p)
        a = _sigmoid(a0 + _mm(seg, a2_ref[...]))
        g = _mm(_sigmoid(seg), g2_ref[...])
        if has_vres:
            seg2 = x[:, 3 * G + 128:3 * G + 256]
            v0 = vec_ref[7:8, :]
            v = v + (vf_ref[b0:b0 + nb].reshape(R, G) - v) * _sigmoid(v0 + _mm(seg2, v2_ref[...]))
        else:
            vout_ref[b0:b0 + nb] = v.reshape(nb, RW_ROWS, G)
        kk = k * k_k
        kk = kk * lax.rsqrt(jnp.maximum(_mm(kk * kk, bo), 1e-24))
        k2 = k * (1.0 + (a - 1.0) * k_a)
        kka = kk * a
        cs = jnp.concatenate([_mmx_l(ltri, lw[j * T:(j + 1) * T], pieces=2) for j in range(nchunks)],
                             axis=0)
        c_last = jnp.concatenate(
            [jnp.broadcast_to(cs[(j + 1) * T - 1:(j + 1) * T, :], (T, G)) for j in range(nchunks)], axis=0)
        e_neg = jnp.exp2(-cs)
        e_dec = jnp.exp2(c_last - cs)
        chunks = lambda z: [z[j * T:(j + 1) * T] for j in range(nchunks)]
        return dict(b0=b0, nb=nb, nchunks=nchunks, r=r, k2=k2, v=v, g=g, c_last=c_last,
                    at=chunks(-kk * jnp.exp2(cs - lw)), bh=chunks(kka * e_neg), kh=chunks(k2 * e_neg),
                    rt=chunks(r * jnp.exp2(cs)), bdec=chunks(kka * e_dec), kdec=chunks(k2 * e_dec),
                    vc=chunks(v))

    def middle(ctx):
        n = ctx["nchunks"]
        p_u, q_u, m_rb, mkv = _rwkv_chunks(ctx["at"], ctx["bh"], ctx["kh"], ctx["rt"], ctx["vc"], head_sel)
        ctx.update(q_u=q_u, m_rb=m_rb, mkv=mkv)
        ctx["rp_lhs"] = [jnp.concatenate([ctx["rt"][i], p_u[i]], axis=0).astype(BF16) for i in range(n)]
        ctx["dec_t"] = [jnp.concatenate([ctx["bdec"][i], ctx["kdec"][i]], axis=0).T.astype(BF16)
                        for i in range(n)]
        ctx["w_col"] = [jnp.exp2(jnp.broadcast_to(ctx["c_last"][i * T:i * T + 1, :], (128, G))).T[:, 0:1]
                        for i in range(n)]

    def chain(ctx):
        b0, nb, n = ctx["b0"], ctx["nb"], ctx["nchunks"]
        s = [s_ref[b0 + b] for b in range(nb)]
        rs = [None] * n
        sas = [None] * n
        seqs = range(nb)
        for j in range(RW_NCH):
            ids = [b * RW_NCH + j for b in seqs]
            rp = [_dot(ctx["rp_lhs"][i], s[b].astype(BF16), _NN) for b, i in zip(seqs, ids)]
            for b, i in zip(seqs, ids):
                rs[i] = rp[b][0:T]
                sas[i] = rp[b][T:2 * T] + ctx["q_u"][i]
            upd = [_dot(ctx["dec_t"][i], jnp.concatenate([sas[i], ctx["vc"][i]], axis=0).astype(BF16), _NN)
                   for i in ids]
            s = [s[b] * ctx["w_col"][i] + jnp.where(same, upd[b], 0.0) for b, i in zip(seqs, ids)]
        outs = [rs[i] + _dot(ctx["m_rb"][i], _stack_heads(sas[i], head_sel), _NN) + ctx["mkv"][i]
                for i in range(n)]
        for b in seqs:
            s_ref[b0 + b] = s[b]
        ctx["o"] = jnp.concatenate(outs, axis=0)

    def epilogue(ctx):
        o = ctx["o"]
        mean = _mmx_r(o, bo, pieces=2) * (1.0 / HEAD_DIM)
        d = o - mean
        var = _mm(d * d, bo) * (1.0 / HEAD_DIM)
        yn = d * lax.rsqrt(var + RWKV_GN_EPS) * lnx_w + lnx_b
        bonus = _mm(ctx["r"] * ctx["k2"] * r_k, bo) * ctx["v"]
        y_ref[ctx["b0"]:ctx["b0"] + ctx["nb"]] = ((yn + bonus) * ctx["g"]).reshape(ctx["nb"], RW_ROWS, G)

    ctx = prologue(0, RW_NB)
    middle(ctx)
    chain(ctx)
    epilogue(ctx)


def _rwkv(rw, v_first, mu, vec, w2p, a2p, g2p, v2p, batch, seq):
    has_vres = v_first is not None
    nc = seq // RW_ROWS
    m = batch * seq
    row_spec = lambda w: pl.BlockSpec((RW_NB, RW_ROWS, w), lambda b, c: (b, c, 0))
    const = lambda shape: pl.BlockSpec(shape, lambda b, c: (0, 0))
    prev_spec = pl.BlockSpec(
        (RW_NB, 8, RW_W), lambda b, c: (b, jnp.maximum(c * (RW_ROWS // 8) - 1, 0), 0))
    rw3 = rw.reshape(batch, seq, RW_W)
    in_specs = [row_spec(RW_W), prev_spec]
    args = [rw3, rw3]
    if has_vres:
        in_specs.append(row_spec(D_GROUP))
        args.append(v_first.reshape(batch, seq, D_GROUP))
    in_specs += [const((1, RW_W)), const((8, D_GROUP)), const((128, D_GROUP)),
                 const((128, D_GROUP)), const((128, D_GROUP))]
    args += [mu, vec, w2p, a2p, g2p]
    if has_vres:
        in_specs.append(const((128, D_GROUP)))
        args.append(v2p)
    out_shape = [jax.ShapeDtypeStruct((batch, seq, D_GROUP), F32)]
    out_specs = [row_spec(D_GROUP)]
    if not has_vres:
        out_shape.append(jax.ShapeDtypeStruct((batch, seq, D_GROUP), F32))
        out_specs.append(row_spec(D_GROUP))
    outs = pl.pallas_call(
        functools.partial(_rwkv_kernel, has_vres),
        grid=(batch // RW_NB, nc),
        in_specs=in_specs,
        out_specs=out_specs,
        out_shape=out_shape,
        scratch_shapes=[pltpu.VMEM((RW_NB, D_GROUP, D_GROUP), F32)],
        compiler_params=pltpu.CompilerParams(dimension_semantics=("parallel", "arbitrary"),
                                             vmem_limit_bytes=VMEM_LIMIT),
    )(*args)
    y = outs[0].reshape(m, D_GROUP)
    if has_vres:
        return y, v_first
    return y, outs[1].reshape(m, D_GROUP)


def _attn_kernel(qkv_ref, o_ref, perm_ref, acc_ref, l_ref, m_ref, bias_ref):
    G = D_GROUP
    blk = ATTN_BLK
    seq = qkv_ref.shape[0]
    nres = DILATIONS[-1]
    sub = seq // nres
    head_sel = [(_iota((1, G), 1) // HEAD_DIM) == h for h in range(N_HEADS)]
    scale = HEAD_DIM ** -0.5 * math.log2(math.e)

    def jmap(a, d):
        run = blk * d // nres
        return (nres // d) * (a % run) + a // run

    to_perm = (_iota((blk, blk), 1) == jmap(_iota((blk, blk), 0), 1)).astype(BF16)
    from_perm = (_iota((blk, blk), 0) == jmap(_iota((blk, blk), 1), 1)).astype(BF16)
    run1 = blk // nres
    col_scale = jnp.where(_iota((1, AT_W), 1) < G, scale, 1.0)

    def permute(n, carry):
        r0 = pl.multiple_of(n * blk, blk)
        pb = jnp.dot(to_perm, (qkv_ref[pl.ds(r0, blk), :] * col_scale).astype(BF16),
                     preferred_element_type=F32)
        for e in range(nres):
            dst = pl.multiple_of(e * sub + n * run1, run1)
            perm_ref[pl.ds(dst, run1), :] = pb[e * run1:(e + 1) * run1, :]
        return carry

    lax.fori_loop(0, seq // blk, permute, 0, unroll=4)

    branches = tuple(reversed(DILATIONS))

    @pl.when(pl.program_id(0) == 0)
    def _():
        for bi, d in enumerate(branches):
            jq = jmap(_iota((blk, 2 * blk), 0), d)
            kb = _iota((blk, 2 * blk), 1)
            dist = blk + jq - (jmap(kb % blk, d) + (kb // blk) * blk)
            in_band = (dist >= 0) & (dist <= blk)
            for h in range(N_HEADS):
                bias_ref[bi * N_HEADS + h] = jnp.where(
                    in_band, dist.astype(F32) * (-ALIBI_SLOPES[h] * d * math.log2(math.e)), -jnp.inf)

    for bi, d in enumerate(branches):
        runs = nres // d
        run = blk // runs
        nb = seq // (d * blk)
        first = bi == 0
        last = bi == len(branches) - 1

        def blocks(items, with_prev, bi=bi, d=d, runs=runs, run=run, first=first, last=last):
            its = range(len(items))
            heads = range(N_HEADS)

            def starts(rho, nn):
                return [pl.multiple_of((rho + d * e) * sub + nn * run, run) for e in range(runs)]

            def gather(st, c0):
                return jnp.concatenate([perm_ref[pl.ds(s0, run), c0:c0 + G] for s0 in st],
                                       axis=0).astype(BF16)

            cur = [starts(rho, n) for rho, n in items]
            qb = [gather(cur[i], 0) for i in its]
            if with_prev:
                prv = [starts(rho, n - 1) for rho, n in items]
                kcat = [jnp.concatenate([gather(prv[i], G), gather(cur[i], G)], axis=0) for i in its]
                vcat = [jnp.concatenate([gather(prv[i], 2 * G), gather(cur[i], 2 * G)], axis=0)
                        for i in its]
            else:
                kcat = [gather(cur[i], G) for i in its]
                vcat = [gather(cur[i], 2 * G) for i in its]
            zq = jnp.zeros_like(qb[0])
            kc0 = 0 if with_prev else blk
            pairs = [(0, 1), (2, 3)]
            s_p = [[_dot(jnp.concatenate([jnp.where(head_sel[h], qb[i], zq) for h in pr], axis=0),
                         kcat[i], _NT) for pr in pairs] for i in its]
            s = [[s_p[i][h // 2][(h % 2) * blk:(h % 2 + 1) * blk]
                  + bias_ref[bi * N_HEADS + h, :, kc0:2 * blk] for h in heads] for i in its]
            mh = [[jnp.max(s[i][h], axis=-1, keepdims=True) for h in heads] for i in its]
            ph = [[jnp.exp2(s[i][h] - mh[i][h]) for h in heads] for i in its]
            lh = [[jnp.sum(ph[i][h], axis=-1, keepdims=True) for h in heads] for i in its]
            pv_p = [[_dot(jnp.concatenate([ph[i][h].astype(BF16) for h in pr], axis=0), vcat[i], _NN)
                     for pr in pairs] for i in its]
            def by_head(parts):
                out = jnp.broadcast_to(parts[N_HEADS - 1], (blk, G))
                for h in reversed(range(N_HEADS - 1)):
                    out = jnp.where(head_sel[h], parts[h], out)
                return out

            acc_b = [by_head([pv_p[i][h // 2][(h % 2) * blk:(h % 2 + 1) * blk] for h in heads])
                     for i in its]
            l_b = [by_head(lh[i]) for i in its]
            m_b = [by_head(mh[i]) for i in its]
            for i in its:
                outs = []
                for e, s0 in enumerate(cur[i]):
                    idx = pl.ds(s0, run)
                    sl = slice(e * run, (e + 1) * run)
                    if first:
                        acc_ref[idx, :] = acc_b[i][sl]
                        l_ref[idx, :] = l_b[i][sl]
                        m_ref[idx, :] = m_b[i][sl]
                    else:
                        m_old = m_ref[idx, :]
                        m_new = jnp.maximum(m_old, m_b[i][sl])
                        w_old = jnp.exp2(m_old - m_new)
                        w_new = jnp.exp2(m_b[i][sl] - m_new)
                        acc_n = acc_ref[idx, :] * w_old + acc_b[i][sl] * w_new
                        l_n = l_ref[idx, :] * w_old + l_b[i][sl] * w_new
                        if last:
                            outs.append(acc_n / l_n)
                        else:
                            acc_ref[idx, :] = acc_n
                            l_ref[idx, :] = l_n
                            m_ref[idx, :] = m_new
                if last:
                    r0 = pl.multiple_of(items[i][1] * blk, blk)
                    o_ref[pl.ds(r0, blk), :] = _mmx_l(from_perm, jnp.concatenate(outs, axis=0))

        def run_blocks(count, index, with_prev, blocks=blocks):
            grp = ATTN_GROUP - 1 if with_prev else ATTN_GROUP

            def group(it, carry):
                blocks([index(grp * it + u) for u in range(grp)], with_prev)
                return carry

            if count >= grp:
                lax.fori_loop(0, count // grp, group, 0)
            rest = count % grp
            if rest:
                blocks([index(count - rest + u) for u in range(rest)], with_prev)

        run_blocks(d, lambda i: (i, 0), False)
        if nb > 1:
            run_blocks(d * (nb - 1), lambda i, nb=nb: (i // (nb - 1), i % (nb - 1) + 1), True)


def _attention(at, batch, seq):
    m = batch * seq
    return pl.pallas_call(
        _attn_kernel,
        grid=(batch,),
        in_specs=[pl.BlockSpec((seq, AT_W), lambda b: (b, 0))],
        out_specs=pl.BlockSpec((seq, D_GROUP), lambda b: (b, 0)),
        out_shape=jax.ShapeDtypeStruct((m, D_GROUP), F32),
        scratch_shapes=[pltpu.VMEM((seq, AT_W), F32)] + [pltpu.VMEM((seq, D_GROUP), F32)] * 3
        + [pltpu.VMEM((len(DILATIONS) * N_HEADS, ATTN_BLK, 2 * ATTN_BLK), F32)],
        compiler_params=pltpu.CompilerParams(dimension_semantics=("arbitrary",),
                                             vmem_limit_bytes=VMEM_LIMIT),
    )(at)


def _ssd_kernel(f_ref, fprev_ref, cw_ref, cb_ref, dtb_ref, vec_ref, ex_ref, y_ref, st_ref):
    c = pl.program_id(1)
    T = SSD_T
    G = D_GROUP
    N = SSD_STATE
    nb = SSD_NB
    R = nb * T
    seqs = range(nb)

    @pl.when(c == 0)
    def _():
        st_ref[...] = jnp.zeros_like(st_ref)

    f = f_ref[...].reshape(R, SD_W)
    xbc_raw = f[:, G:G + SSD_XBC]
    prev8 = [jnp.where(c == 0, 0.0, fprev_ref[b, :, G:G + SSD_XBC]) for b in seqs]
    row8 = _iota((8, 1), 0)
    conv = xbc_raw * cw_ref[3:4, :] + cb_ref[...]
    for j in range(1, 4):
        rolled = pltpu.roll(xbc_raw, j, axis=0)
        pieces = []
        for b in seqs:
            pieces.append(jnp.where(row8 < j, pltpu.roll(prev8[b], j, axis=0), rolled[b * T:b * T + 8]))
            pieces.append(rolled[b * T + 8:(b + 1) * T])
        conv = conv + jnp.concatenate(pieces, axis=0) * cw_ref[3 - j:4 - j, :]
    xbc = _silu(conv)
    xs = xbc[:, 0:G]
    bm = xbc[:, G:G + 2 * N]
    cm = xbc[:, G + 2 * N:G + 4 * N]
    z = f[:, 0:G]

    a_dense, d_dense, norm_w = (vec_ref[i:i + 1, :] for i in range(3))
    dt = _softplus(f[:, G + SSD_XBC:G + SSD_XBC + 128] + dtb_ref[...])
    dt_dense = _mmx_r(dt, ex_ref[...], pieces=2)
    da = dt_dense * a_dense
    xdt = xs * dt_dense
    ltri = (_iota((T, T), 1) <= _iota((T, T), 0))
    ltri_b = ltri.astype(BF16)
    umat = (_iota((T, T), 0) > _iota((T, T), 1)).astype(F32)
    masks = _head_masks()
    rows = [slice(b * T, (b + 1) * T) for b in seqs]

    cs = [_mmx_l(ltri_b, da[rows[b]], pieces=2) for b in seqs]
    scores = [[_mm(cm[rows[b], g * N:(g + 1) * N], bm[rows[b], g * N:(g + 1) * N], _NT)
               for g in range(2)] for b in seqs]
    y_diag = []
    for b in seqs:
        acc = None
        for h in range(N_HEADS):
            da_col = jnp.broadcast_to(da[rows[b], h * HEAD_DIM:h * HEAD_DIM + 1], (T, T))
            seg = _mmx_l(ltri_b, da_col * umat, pieces=2)
            dec = jnp.where(ltri, jnp.exp2(seg), 0.0)
            term = _mm(scores[b][h // 2] * dec, xdt[rows[b]] * masks[h])
            acc = term if acc is None else acc + term
        y_diag.append(acc)

    ys = []
    for b in seqs:
        st = st_ref[b]
        cmb = cm[rows[b]]
        y_off = jnp.concatenate([_mm(cmb[:, 0:N], st[:, 0:N]), _mm(cmb[:, N:2 * N], st[:, N:2 * N])],
                                axis=1) * jnp.exp2(cs[b])
        cs_last = cs[b][T - 1:T, :]
        xd = xdt[rows[b]] * jnp.exp2(cs_last - cs[b])
        bmb = bm[rows[b]]
        st_ref[b] = st * jnp.exp2(cs_last) + jnp.concatenate(
            [_mm(bmb[:, 0:N], xd[:, 0:N], _TN), _mm(bmb[:, N:2 * N], xd[:, N:2 * N], _TN)], axis=1)
        ys.append(y_diag[b] + y_off)

    y = (jnp.concatenate(ys, axis=0) + xs * d_dense) * _silu(z)
    halves = []
    for g in range(2):
        yg = y[:, g * N:(g + 1) * N]
        halves.append(yg * lax.rsqrt(jnp.mean(yg * yg, axis=-1, keepdims=True) + RMS_EPS))
    y_ref[...] = (jnp.concatenate(halves, axis=1) * norm_w).reshape(nb, T, G)


def _ssd(sd, cw, cb, dtb, vec, ex, batch, seq):
    nc = seq // SSD_T
    m = batch * seq
    const = lambda shape: pl.BlockSpec(shape, lambda b, c: (0, 0))
    sd3 = sd.reshape(batch, seq, SD_W)
    out = pl.pallas_call(
        _ssd_kernel,
        grid=(batch // SSD_NB, nc),
        in_specs=[pl.BlockSpec((SSD_NB, SSD_T, SD_W), lambda b, c: (b, c, 0)),
                  pl.BlockSpec((SSD_NB, 8, SD_W),
                               lambda b, c: (b, jnp.maximum(c * (SSD_T // 8) - 1, 0), 0)),
                  const((4, SSD_XBC)), const((1, SSD_XBC)), const((1, 128)), const((8, D_GROUP)),
                  const((128, D_GROUP))],
        out_specs=pl.BlockSpec((SSD_NB, SSD_T, D_GROUP), lambda b, c: (b, c, 0)),
        out_shape=jax.ShapeDtypeStruct((batch, seq, D_GROUP), F32),
        scratch_shapes=[pltpu.VMEM((SSD_NB, SSD_STATE, D_GROUP), F32)],
        compiler_params=pltpu.CompilerParams(dimension_semantics=("parallel", "arbitrary"),
                                             vmem_limit_bytes=VMEM_LIMIT),
    )(sd3, sd3, cw, cb, dtb, vec, ex)
    return out.reshape(m, D_GROUP)


def _hgrn_kernel(f_ref, vec_ref, y_ref, st_ref):
    c = pl.program_id(1)
    T = HG_T
    C = HG_C
    G = D_GROUP
    nblk = T // C

    @pl.when(c == 0)
    def _():
        st_ref[...] = jnp.zeros_like(st_ref)

    lb = vec_ref[0:1, :]
    norm_w = vec_ref[1:2, :]
    forget = lb + (1.0 - lb) * _sigmoid_tail(f_ref[:, G:2 * G])
    q = _silu(f_ref[:, 0:G])
    k = 1.0 - forget
    v = f_ref[:, 2 * G:3 * G]
    grp = 4 * C
    blk_tri = ((_iota((grp, grp), 0) // C == _iota((grp, grp), 1) // C)
               & (_iota((grp, grp), 1) <= _iota((grp, grp), 0))).astype(BF16)
    log_f = jnp.log(forget)
    b = jnp.concatenate([_mmx_l(blk_tri, log_f[r0:r0 + grp], pieces=2) for r0 in range(0, T, grp)], axis=0)
    b2 = b * math.log2(math.e)
    c2 = b2 - jnp.log2(k)

    head_sel = [(_iota((1, G), 1) // HEAD_DIM) == h for h in range(N_HEADS)]
    bo = _same_head().astype(BF16)
    t_idx = _iota((C, 1), 0)
    blocks = lambda z: [z[i * C:(i + 1) * C] for i in range(nblk)]
    bq, qq, kq, vv, b2q, c2q = blocks(b), blocks(q), blocks(k), blocks(v), blocks(b2), blocks(c2)
    b_last = [bq[i][C - 1:C, :] for i in range(nblk)]

    upd = [_dot(_stack_heads(vv[i], head_sel), _stack_heads(kq[i] * jnp.exp(b_last[i] - bq[i]), head_sel),
                _TN) for i in range(nblk)]
    q_in = [qq[i] * jnp.exp(bq[i]) for i in range(nblk)]
    half = C // 2
    o_intra = []
    for i in range(nblk):
        parts = []
        for s in range(C):
            lo = 0 if s < half else half
            e = jnp.exp2(b2q[i][lo:C] - c2q[i][s:s + 1, :])
            parts.append(jnp.where(t_idx[lo:C] >= s, qq[i][lo:C] * e, 0.0))
        z = _mm(jnp.concatenate(parts, axis=0), bo)
        o_lo = z[0:half] * vv[i][0:1, :]
        o_hi = z[half:C] * vv[i][0:1, :]
        for s in range(1, half):
            o_lo = o_lo + z[s * C:s * C + half] * vv[i][s:s + 1, :]
            o_hi = o_hi + z[s * C + half:(s + 1) * C] * vv[i][s:s + 1, :]
        for s in range(half, C):
            r0 = half * C + (s - half) * half
            o_hi = o_hi + z[r0:r0 + half] * vv[i][s:s + 1, :]
        o_intra.append(jnp.concatenate([o_lo, o_hi], axis=0))

    st = st_ref[...]
    outs = []
    for i in range(nblk):
        outs.append(_mm(q_in[i], st, _NT) + o_intra[i])
        st = st * jnp.exp(b_last[i]) + upd[i]
    st_ref[...] = st
    o = jnp.concatenate(outs, axis=0)
    ms = _mmx_r(o * o, bo, pieces=2) * (1.0 / HEAD_DIM)
    y_ref[...] = o * lax.rsqrt(ms + RMS_EPS) * norm_w * _silu(f_ref[:, 3 * G:4 * G])


def _hgrn(hg, vec, batch, seq):
    nc = seq // HG_T
    m = batch * seq
    return pl.pallas_call(
        _hgrn_kernel,
        grid=(batch, nc),
        in_specs=[pl.BlockSpec((HG_T, HG_W), lambda b, c: (b * nc + c, 0)),
                  pl.BlockSpec((8, D_GROUP), lambda b, c: (0, 0))],
        out_specs=pl.BlockSpec((HG_T, D_GROUP), lambda b, c: (b * nc + c, 0)),
        out_shape=jax.ShapeDtypeStruct((m, D_GROUP), F32),
        scratch_shapes=[pltpu.VMEM((D_GROUP, D_GROUP), F32)],
        compiler_params=pltpu.CompilerParams(dimension_semantics=("parallel", "arbitrary"),
                                             vmem_limit_bytes=VMEM_LIMIT),
    )(hg, vec)


def _post_kernel(x_ref, ya_ref, yb_ref, yc_ref, yd_ref, wo_ref, wu_ref, wd_ref, ln_ref, o_ref):
    half = POST_TM // 2
    rows = [slice(0, half), slice(half, POST_TM)]

    def mixed(r):
        acc = ALPHA * x_ref[r, :]
        for i, ref in enumerate((ya_ref, yb_ref, yc_ref, yd_ref)):
            acc = acc + jnp.dot(ref[r, :].astype(BF16), wo_ref[i * D_GROUP:(i + 1) * D_GROUP, :],
                                preferred_element_type=F32)
        return acc

    def ffn_chunk(xb, acc, c0):
        h = jnp.maximum(jnp.dot(xb, wu_ref[:, c0:c0 + FF_CHUNK], preferred_element_type=F32), 0.0)
        return acc + jnp.dot((h * h).astype(BF16), wd_ref[c0:c0 + FF_CHUNK, :],
                             preferred_element_type=F32)

    chunks = list(range(0, D_FF, FF_CHUNK))
    pre_a = mixed(rows[0])
    x1a = _layer_norm(pre_a, ln_ref[0:1, :], ln_ref[1:2, :])
    pre_b = mixed(rows[1])
    xba = x1a.astype(BF16)
    acc_a = ALPHA * x1a
    for c0 in chunks[:2]:
        acc_a = ffn_chunk(xba, acc_a, c0)
    x1b = _layer_norm(pre_b, ln_ref[0:1, :], ln_ref[1:2, :])
    for c0 in chunks[2:]:
        acc_a = ffn_chunk(xba, acc_a, c0)
    xbb = x1b.astype(BF16)
    acc_b = ALPHA * x1b
    for c0 in chunks[:2]:
        acc_b = ffn_chunk(xbb, acc_b, c0)
    o_ref[rows[0], :] = _layer_norm(acc_a, ln_ref[2:3, :], ln_ref[3:4, :])
    for c0 in chunks[2:]:
        acc_b = ffn_chunk(xbb, acc_b, c0)
    o_ref[rows[1], :] = _layer_norm(acc_b, ln_ref[2:3, :], ln_ref[3:4, :])


def _post(x2d, ys, wo, wu, wd, ln):
    m = x2d.shape[0]
    row = lambda w: pl.BlockSpec((POST_TM, w), lambda i: (i, 0))
    res = lambda shape: pl.BlockSpec(shape, lambda i: (0, 0), pipeline_mode=pl.Buffered(1))
    return pl.pallas_call(
        _post_kernel,
        grid=(m // POST_TM,),
        in_specs=[row(D_MODEL)] + [row(D_GROUP)] * 4
        + [res((D_MODEL, D_MODEL)), res((D_MODEL, D_FF)), res((D_FF, D_MODEL)), res((8, D_MODEL))],
        out_specs=row(D_MODEL),
        out_shape=jax.ShapeDtypeStruct((m, D_MODEL), F32),
        compiler_params=pltpu.CompilerParams(dimension_semantics=("parallel",),
                                             vmem_limit_bytes=VMEM_LIMIT),
    )(x2d, *ys, wo, wu, wd, ln)


def _pad_cols(a, width):
    return jnp.pad(a, ((0, 0), (0, width - a.shape[1])))


def _pad_rows(a, top, total):
    return jnp.pad(a, ((top, total - top - a.shape[0]), (0, 0)))


def _rep_heads(v):
    return jnp.repeat(v.astype(F32), HEAD_DIM)[None, :]


def _rows8(rows, width):
    out = jnp.concatenate([r.reshape(1, width).astype(F32) for r in rows], axis=0)
    return jnp.pad(out, ((0, 8 - out.shape[0]), (0, 0)))


def kernel(x, lower_bounds, w_in, w_in_vres, mu_shift, mu_vres, rwkv_w0, rwkv_w2, rwkv_a0, rwkv_a2, rwkv_g2, rwkv_k_k, rwkv_k_a, rwkv_r_k, rwkv_lnx_w, rwkv_lnx_b, rwkv_v0, rwkv_v2, ssd_conv_w, ssd_conv_b, ssd_dt_bias, ssd_A_log, ssd_D, ssd_norm_w, hgrn_norm_w, w_out, ln1_w, ln1_b, w_up, w_down, ln2_w, ln2_b):
    batch, seq, _ = x.shape
    lb = jax.nn.softmax(lower_bounds.astype(F32), axis=0)
    lb = jnp.cumsum(lb, axis=0) - lb[0]
    expand = (jnp.arange(128)[:, None] == (jnp.arange(D_GROUP)[None, :] // HEAD_DIM)).astype(BF16)

    h = x.reshape(batch * seq, D_MODEL)
    v_first = None
    for l in range(DEPTH):
        mu = mu_shift[l][None, :]
        w_vres = None
        if l > 0:
            w_vres = _pad_cols(w_in_vres[l - 1], 128).astype(BF16)
            mu = jnp.concatenate([mu, mu_vres[l - 1][None, :]], axis=1)
        rw, at, sd, hg = _project(h, w_in, l, w_vres)

        vec_rows = [rwkv_w0[l], rwkv_a0[l], rwkv_k_k[l], rwkv_k_a[l], rwkv_lnx_w[l], rwkv_lnx_b[l],
                    rwkv_r_k[l].reshape(-1)]
        if l > 0:
            vec_rows.append(rwkv_v0[l - 1])
        y_a, v_first = _rwkv(
            rw, v_first, _pad_cols(mu, RW_W), _rows8(vec_rows, D_GROUP),
            _pad_rows(rwkv_w2[l], 0, 128), _pad_rows(rwkv_a2[l], 32, 128), _pad_rows(rwkv_g2[l], 64, 128),
            _pad_rows(rwkv_v2[l - 1], 0, 128) if l > 0 else None, batch, seq)
        y_b = _attention(at, batch, seq)
        ssd_vec = _rows8([_rep_heads(-jnp.exp(ssd_A_log[l].astype(F32)) * math.log2(math.e)), _rep_heads(ssd_D[l]),
                          ssd_norm_w[l]], D_GROUP)
        y_c = _ssd(sd, ssd_conv_w[l].astype(F32), ssd_conv_b[l][None, :],
                   _pad_cols(ssd_dt_bias[l][None, :], 128), ssd_vec, expand, batch, seq)
        y_d = _hgrn(hg, _rows8([lb[l], hgrn_norm_w[l]], D_GROUP), batch, seq)
        ln = _rows8([ln1_w[l], ln1_b[l], ln2_w[l], ln2_b[l]], D_MODEL)
        h = _post(h, (y_a, y_b, y_c, y_d), w_out[l].astype(BF16), w_up[l].astype(BF16),
                  w_down[l].astype(BF16), ln)
    return h.reshape(batch, seq, D_MODEL)
```

```python
import functools
import math

import jax
import jax.numpy as jnp
from jax import lax
from jax.experimental import pallas as pl
from jax.experimental.pallas import tpu as pltpu

F32 = jnp.float32
BF16 = jnp.bfloat16

D_MODEL = 1024
DEPTH = 2
D_GROUP = 256
HEAD_DIM = 64
N_HEADS = 4
D_FF = 4 * D_MODEL
ALPHA = (2.0 * DEPTH) ** 0.25
LN_EPS = 1e-5
RMS_EPS = 1e-5
RWKV_GN_EPS = HEAD_DIM * 1e-5
RWKV_COLS = 896
ATTN_COLS = 768
SSD_STATE = 128
SSD_XBC = 768
SSD_COLS = 1028
HGRN_COLS = 1024
DILATIONS = (1, 4, 16)
ATTN_BLK = 128
ATTN_GROUP = 4
ATTN_SCALE = HEAD_DIM ** -0.5 * math.log2(math.e)
ALIBI_SLOPES = tuple(2.0 ** (-8.0 * (h + 1) / N_HEADS) for h in range(N_HEADS))

RW_W, AT_W, SD_W, HG_W = 1024, 768, 1152, 1024

RW_T = 64
RW_NCH = 2
RW_NB = 8
RW_ROWS = RW_T * RW_NCH
SSD_T = 128
SSD_NB = 8
HG_C = 16
HG_T = 512
PROJ_TM = 512
POST_TM = 1024
FF_CHUNK = 512

V7X_VMEM_BYTES = 64 * 1024 * 1024
VMEM_LIMIT = V7X_VMEM_BYTES - 8 * 1024 * 1024


def _dot(a, b, dims):
    return lax.dot_general(a, b, (dims, ((), ())), preferred_element_type=F32)


_NN = ((1,), (0,))
_NT = ((1,), (1,))
_TN = ((0,), (0,))


def _mm(a, b, dims=_NN):
    return _dot(a.astype(BF16), b.astype(BF16), dims)


def _split(x, pieces):
    out = []
    for i in range(pieces):
        hi = x.astype(BF16)
        out.append(hi)
        if i + 1 < pieces:
            x = x - hi.astype(F32)
    return out


def _mmx_l(m01, x, dims=_NN, pieces=3):
    return sum(_dot(m01, xp, dims) for xp in _split(x, pieces))


def _mmx_r(x, m01, dims=_NN, pieces=3):
    return sum(_dot(xp, m01, dims) for xp in _split(x, pieces))


def _mm3(a, b, dims=_NN):
    ah = a.astype(BF16)
    al = (a - ah.astype(F32)).astype(BF16)
    bh = b.astype(BF16)
    bl = (b - bh.astype(F32)).astype(BF16)
    return _dot(ah, bh, dims) + _dot(ah, bl, dims) + _dot(al, bh, dims)


def _iota(shape, dim):
    return lax.broadcasted_iota(jnp.int32, shape, dim)


def _head_masks(width=D_GROUP):
    lane = _iota((1, width), 1) // HEAD_DIM
    return [(lane == h).astype(F32) for h in range(N_HEADS)]


def _same_head(n=D_GROUP):
    return (_iota((n, n), 0) // HEAD_DIM) == (_iota((n, n), 1) // HEAD_DIM)


def _sigmoid(x):
    return 0.5 * jnp.tanh(0.5 * x) + 0.5


def _sigmoid_tail(x):
    return 1.0 / (1.0 + jnp.exp(-x))


def _silu(x):
    return x * _sigmoid(x)


def _softplus(x):
    return jnp.maximum(x, 0.0) + jnp.log(1.0 + jnp.exp(-jnp.abs(x)))


def _layer_norm(x, w, b):
    mu = jnp.mean(x, axis=-1, keepdims=True)
    d = x - mu
    var = jnp.mean(d * d, axis=-1, keepdims=True)
    return d * lax.rsqrt(var + LN_EPS) * w + b


def _proj_kernel(has_vres, *refs):
    if has_vres:
        x_ref, w_ref, wv_ref, rw_ref, at_ref, sd_ref, hg_ref = refs
    else:
        x_ref, w_ref, rw_ref, at_ref, sd_ref, hg_ref = refs
    xb = x_ref[...].astype(BF16)
    n_in = w_ref.shape[1]

    def emit(ref, c_out, c_in, width):
        wb = w_ref[:, c_in:c_in + width].astype(BF16)
        val = jnp.dot(xb, wb, preferred_element_type=F32)
        if ref is at_ref and c_out == 0:
            val = val * ATTN_SCALE
        ref[:, c_out:c_out + width] = val.astype(ref.dtype)

    src = 0
    for ref, cols, padded in ((rw_ref, RWKV_COLS, RW_W), (at_ref, ATTN_COLS, AT_W),
                              (sd_ref, SSD_COLS, SD_W), (hg_ref, HGRN_COLS, HG_W)):
        for c0 in range(0, padded, 256):
            width = min(256, padded - c0)
            if c0 + width <= cols:
                emit(ref, c0, src + c0, width)
                continue
            real = max(cols - c0, 0)
            real_pad = -(-real // 128) * 128
            if has_vres and ref is rw_ref and real == real_pad and real_pad + wv_ref.shape[1] == width:
                wb = jnp.concatenate([w_ref[:, src + c0:src + c0 + real].astype(BF16), wv_ref[...]], axis=1)
                ref[:, c0:c0 + width] = jnp.dot(xb, wb, preferred_element_type=F32)
                continue
            if real:
                take = min(real_pad, n_in - (src + c0))
                wb = w_ref[:, src + c0:src + c0 + take].astype(BF16)
                lane = _iota((1, take), 1)
                wb = jnp.where(lane < real, wb, jnp.zeros_like(wb))
                ref[:, c0:c0 + take] = jnp.dot(xb, wb, preferred_element_type=F32)
                if take < real_pad:
                    ref[:, c0 + take:c0 + real_pad] = jnp.zeros((xb.shape[0], real_pad - take), F32)
            rest = width - real_pad
            if rest:
                if has_vres and ref is rw_ref:
                    ref[:, c0 + real_pad:c0 + width] = jnp.dot(xb, wv_ref[...], preferred_element_type=F32)
                else:
                    ref[:, c0 + real_pad:c0 + width] = jnp.zeros((xb.shape[0], rest), F32)
        src += cols


def _project(x2d, w_in, layer, w_vres):
    m = x2d.shape[0]
    n_in = w_in.shape[2]
    has_vres = w_vres is not None
    outs = [jax.ShapeDtypeStruct((m, w), BF16 if w == AT_W else F32) for w in (RW_W, AT_W, SD_W, HG_W)]
    in_specs = [pl.BlockSpec((PROJ_TM, D_MODEL), lambda i: (i, 0)),
                pl.BlockSpec((None, D_MODEL, n_in), lambda i: (layer, 0, 0), pipeline_mode=pl.Buffered(1))]
    args = [x2d, w_in]
    if has_vres:
        in_specs.append(pl.BlockSpec((D_MODEL, 128), lambda i: (0, 0)))
        args.append(w_vres)
    return pl.pallas_call(
        functools.partial(_proj_kernel, has_vres),
        grid=(m // PROJ_TM,),
        in_specs=in_specs,
        out_specs=[pl.BlockSpec((PROJ_TM, w), lambda i: (i, 0)) for w in (RW_W, AT_W, SD_W, HG_W)],
        out_shape=outs,
        compiler_params=pltpu.CompilerParams(dimension_semantics=("parallel",),
                                             vmem_limit_bytes=VMEM_LIMIT),
    )(*args)


def _stack_heads(x, head_sel):
    xb = x.astype(BF16)
    zero = jnp.zeros_like(xb)
    return jnp.concatenate([jnp.where(m, xb, zero) for m in head_sel], axis=0)


def _rwkv_chunks(at, bh, kh, rt, v, head_sel):
    n = len(at)
    rng = range(n)
    T = at[0].shape[0]
    G = D_GROUP
    row = _iota((T, G), 0)
    lane_t = _iota((T, G), 1) % T
    strict = lane_t < row
    incl = lane_t <= row
    st = lambda x: _stack_heads(x, head_sel)
    lhs = [jnp.concatenate([at[j], rt[j]], axis=0).astype(BF16) for j in rng]
    ab = [_dot(lhs[j], st(bh[j]), _NT) for j in rng]
    ak = [_dot(lhs[j], st(kh[j]), _NT) for j in rng]
    n_u = [jnp.where(strict, ab[j][0:T], 0.0) for j in rng]
    a_ak = [jnp.where(strict, ak[j][0:T], 0.0) for j in rng]
    m_rb = [jnp.where(incl, ab[j][T:2 * T], 0.0).astype(BF16) for j in rng]
    m_rk = [jnp.where(incl, ak[j][T:2 * T], 0.0).astype(BF16) for j in rng]

    eye = jnp.where(lane_t == row, 1.0, 0.0)
    x_u = [eye + n_u[j] for j in rng]
    pw = [_dot(n_u[j].astype(BF16), st(n_u[j]), _NN) for j in rng]
    steps = T.bit_length() - 2
    for it in range(steps):
        w = [st(pw[j]) for j in rng]
        if it + 1 < steps:
            res = [_dot(jnp.concatenate([pw[j], x_u[j]], axis=0).astype(BF16), w[j], _NN) for j in rng]
            pw = [res[j][0:T] for j in rng]
            x_u = [x_u[j] + res[j][T:2 * T] for j in rng]
        else:
            x_u = [x_u[j] + _dot(x_u[j].astype(BF16), w[j], _NN) for j in rng]

    v_s = [st(v[j]) for j in rng]
    akv = [_dot(a_ak[j].astype(BF16), v_s[j], _NN) for j in rng]
    xb = [x_u[j].astype(BF16) for j in rng]
    p_u = [_dot(xb[j], st(at[j]), _NN) for j in rng]
    q_u = [_dot(xb[j], st(akv[j]), _NN) for j in rng]
    mkv = [_dot(m_rk[j], v_s[j], _NN) for j in rng]
    return p_u, q_u, m_rb, mkv


def _rwkv_kernel(has_vres, *refs):
    if has_vres:
        (f_ref, fprev_ref, vf_ref, mu_ref, vec_ref, w2_ref, a2_ref, g2_ref, v2_ref,
         y_ref, s_ref) = refs
    else:
        (f_ref, fprev_ref, mu_ref, vec_ref, w2_ref, a2_ref, g2_ref,
         y_ref, vout_ref, s_ref) = refs
    c = pl.program_id(1)
    T = RW_T
    G = D_GROUP

    @pl.when(c == 0)
    def _():
        s_ref[...] = jnp.zeros_like(s_ref)

    w0, a0, k_k, k_a = (vec_ref[i:i + 1, :] for i in range(4))
    lnx_w, lnx_b, r_k = (vec_ref[i:i + 1, :] for i in range(4, 7))
    head_sel = [(_iota((1, G), 1) // HEAD_DIM) == h for h in range(N_HEADS)]
    same = _same_head()
    bo = same.astype(BF16)
    ltri = (_iota((T, T), 1) <= _iota((T, T), 0)).astype(BF16)

    def prologue(b0, nb):
        R = RW_ROWS * nb
        nchunks = RW_NCH * nb
        f = f_ref[b0:b0 + nb].reshape(R, RW_W)
        row = _iota((R, 1), 0)
        shifted = pltpu.roll(f, 1, axis=0)
        for b in range(nb):
            prev_row = jnp.where(c == 0, 0.0, fprev_ref[b0 + b, 7:8, :])
            shifted = jnp.where(row == b * RW_ROWS, prev_row, shifted)
        x = f + (shifted - f) * mu_ref[...]
        r = x[:, 0:G]
        k = x[:, G:2 * G]
        v = x[:, 2 * G:3 * G]
        seg = x[:, 3 * G:3 * G + 128]
        p = w0 + _mm3(jnp.tanh(seg), w2_ref[...])
        lw = (-math.exp(-0.5) * math.log2(math.e)) * _sigmoid(p)
        a = _sigmoid(a0 + _mm(seg, a2_ref[...]))
        g = _mm(_sigmoid(seg), g2_ref[...])
        if has_vres:
            seg2 = x[:, 3 * G + 128:3 * G + 256]
            v0 = vec_ref[7:8, :]
            v = v + (vf_ref[b0:b0 + nb].reshape(R, G) - v) * _sigmoid(v0 + _mm(seg2, v2_ref[...]))
        else:
            vout_ref[b0:b0 + nb] = v.reshape(nb, RW_ROWS, G)
        kk = k * k_k
        kk = kk * lax.rsqrt(jnp.maximum(_mm(kk * kk, bo), 1e-24))
        k2 = k * (1.0 + (a - 1.0) * k_a)
        kka = kk * a
        cs = jnp.concatenate([_mmx_l(ltri, lw[j * T:(j + 1) * T], pieces=2) for j in range(nchunks)],
                             axis=0)
        c_last = jnp.concatenate(
            [jnp.broadcast_to(cs[(j + 1) * T - 1:(j + 1) * T, :], (T, G)) for j in range(nchunks)], axis=0)
        e_neg = jnp.exp2(-cs)
        e_dec = jnp.exp2(c_last - cs)
        chunks = lambda z: [z[j * T:(j + 1) * T] for j in range(nchunks)]
        return dict(b0=b0, nb=nb, nchunks=nchunks, r=r, k2=k2, v=v, g=g, c_last=c_last,
                    at=chunks(-kk * jnp.exp2(cs - lw)), bh=chunks(kka * e_neg), kh=chunks(k2 * e_neg),
                    rt=chunks(r * jnp.exp2(cs)), bdec=chunks(kka * e_dec), kdec=chunks(k2 * e_dec),
                    vc=chunks(v))

    def middle(ctx):
        n = ctx["nchunks"]
        p_u, q_u, m_rb, mkv = _rwkv_chunks(ctx["at"], ctx["bh"], ctx["kh"], ctx["rt"], ctx["vc"], head_sel)
        ctx.update(q_u=q_u, m_rb=m_rb, mkv=mkv)
        ctx["rp_lhs"] = [jnp.concatenate([ctx["rt"][i], p_u[i]], axis=0).astype(BF16) for i in range(n)]
        ctx["dec_t"] = [jnp.concatenate([ctx["bdec"][i], ctx["kdec"][i]], axis=0).T.astype(BF16)
                        for i in range(n)]
        ctx["w_col"] = [jnp.exp2(jnp.broadcast_to(ctx["c_last"][i * T:i * T + 1, :], (128, G))).T[:, 0:1]
                        for i in range(n)]

    def chain(ctx):
        b0, nb, n = ctx["b0"], ctx["nb"], ctx["nchunks"]
        s = [s_ref[b0 + b] for b in range(nb)]
        rs = [None] * n
        sas = [None] * n
        seqs = range(nb)
        for j in range(RW_NCH):
            ids = [b * RW_NCH + j for b in seqs]
            rp = [_dot(ctx["rp_lhs"][i], s[b].astype(BF16), _NN) for b, i in zip(seqs, ids)]
            for b, i in zip(seqs, ids):
                rs[i] = rp[b][0:T]
                sas[i] = rp[b][T:2 * T] + ctx["q_u"][i]
            upd = [_dot(ctx["dec_t"][i], jnp.concatenate([sas[i], ctx["vc"][i]], axis=0).astype(BF16), _NN)
                   for i in ids]
            s = [s[b] * ctx["w_col"][i] + jnp.where(same, upd[b], 0.0) for b, i in zip(seqs, ids)]
        outs = [rs[i] + _dot(ctx["m_rb"][i], _stack_heads(sas[i], head_sel), _NN) + ctx["mkv"][i]
                for i in range(n)]
        for b in seqs:
            s_ref[b0 + b] = s[b]
        ctx["o"] = jnp.concatenate(outs, axis=0)

    def epilogue(ctx):
        o = ctx["o"]
        mean = _mmx_r(o, bo, pieces=2) * (1.0 / HEAD_DIM)
        d = o - mean
        var = _mm(d * d, bo) * (1.0 / HEAD_DIM)
        yn = d * lax.rsqrt(var + RWKV_GN_EPS) * lnx_w + lnx_b
        bonus = _mm(ctx["r"] * ctx["k2"] * r_k, bo) * ctx["v"]
        y_ref[ctx["b0"]:ctx["b0"] + ctx["nb"]] = ((yn + bonus) * ctx["g"]).reshape(ctx["nb"], RW_ROWS, G)

    ctx = prologue(0, RW_NB)
    middle(ctx)
    chain(ctx)
    epilogue(ctx)


def _rwkv(rw, v_first, mu, vec, w2p, a2p, g2p, v2p, batch, seq):
    has_vres = v_first is not None
    nc = seq // RW_ROWS
    m = batch * seq
    row_spec = lambda w: pl.BlockSpec((RW_NB, RW_ROWS, w), lambda b, c: (b, c, 0))
    const = lambda shape: pl.BlockSpec(shape, lambda b, c: (0, 0))
    prev_spec = pl.BlockSpec(
        (RW_NB, 8, RW_W), lambda b, c: (b, jnp.maximum(c * (RW_ROWS // 8) - 1, 0), 0))
    rw3 = rw.reshape(batch, seq, RW_W)
    in_specs = [row_spec(RW_W), prev_spec]
    args = [rw3, rw3]
    if has_vres:
        in_specs.append(row_spec(D_GROUP))
        args.append(v_first.reshape(batch, seq, D_GROUP))
    in_specs += [const((1, RW_W)), const((8, D_GROUP)), const((128, D_GROUP)),
                 const((128, D_GROUP)), const((128, D_GROUP))]
    args += [mu, vec, w2p, a2p, g2p]
    if has_vres:
        in_specs.append(const((128, D_GROUP)))
        args.append(v2p)
    out_shape = [jax.ShapeDtypeStruct((batch, seq, D_GROUP), F32)]
    out_specs = [row_spec(D_GROUP)]
    if not has_vres:
        out_shape.append(jax.ShapeDtypeStruct((batch, seq, D_GROUP), F32))
        out_specs.append(row_spec(D_GROUP))
    outs = pl.pallas_call(
        functools.partial(_rwkv_kernel, has_vres),
        grid=(batch // RW_NB, nc),
        in_specs=in_specs,
        out_specs=out_specs,
        out_shape=out_shape,
        scratch_shapes=[pltpu.VMEM((RW_NB, D_GROUP, D_GROUP), F32)],
        compiler_params=pltpu.CompilerParams(dimension_semantics=("parallel", "arbitrary"),
                                             vmem_limit_bytes=VMEM_LIMIT),
    )(*args)
    y = outs[0].reshape(m, D_GROUP)
    if has_vres:
        return y, v_first
    return y, outs[1].reshape(m, D_GROUP)


def _attn_kernel(qkv_ref, o_ref, perm_ref, acc_ref, l_ref, m_ref, bias_ref):
    G = D_GROUP
    blk = ATTN_BLK
    seq = qkv_ref.shape[0]
    nres = DILATIONS[-1]
    sub = seq // nres
    head_sel = [(_iota((1, G), 1) // HEAD_DIM) == h for h in range(N_HEADS)]
    def jmap(a, d):
        run = blk * d // nres
        return (nres // d) * (a % run) + a // run

    to_perm = (_iota((blk, blk), 1) == jmap(_iota((blk, blk), 0), 1)).astype(BF16)
    from_perm = (_iota((blk, blk), 0) == jmap(_iota((blk, blk), 1), 1)).astype(BF16)
    run1 = blk // nres

    def permute(n, carry):
        r0 = pl.multiple_of(n * blk, blk)
        pb = jnp.dot(to_perm, qkv_ref[pl.ds(r0, blk), :], preferred_element_type=F32)
        for e in range(nres):
            dst = pl.multiple_of(e * sub + n * run1, run1)
            perm_ref[pl.ds(dst, run1), :] = pb[e * run1:(e + 1) * run1, :]
        return carry

    lax.fori_loop(0, seq // blk, permute, 0, unroll=4)

    branches = tuple(reversed(DILATIONS))

    @pl.when(pl.program_id(0) == 0)
    def _():
        for bi, d in enumerate(branches):
            jq = jmap(_iota((blk, 2 * blk), 0), d)
            kb = _iota((blk, 2 * blk), 1)
            dist = blk + jq - (jmap(kb % blk, d) + (kb // blk) * blk)
            in_band = (dist >= 0) & (dist <= blk)
            for h in range(N_HEADS):
                bias_ref[bi * N_HEADS + h] = jnp.where(
                    in_band, dist.astype(F32) * (-ALIBI_SLOPES[h] * d * math.log2(math.e)), -jnp.inf)

    for bi, d in enumerate(branches):
        runs = nres // d
        run = blk // runs
        nb = seq // (d * blk)
        first = bi == 0
        last = bi == len(branches) - 1

        def blocks(items, with_prev, bi=bi, d=d, runs=runs, run=run, first=first, last=last):
            its = range(len(items))
            heads = range(N_HEADS)

            def starts(rho, nn):
                return [pl.multiple_of((rho + d * e) * sub + nn * run, run) for e in range(runs)]

            def gather(st, c0):
                return jnp.concatenate([perm_ref[pl.ds(s0, run), c0:c0 + G] for s0 in st],
                                       axis=0).astype(BF16)

            cur = [starts(rho, n) for rho, n in items]
            qb = [gather(cur[i], 0) for i in its]
            if with_prev:
                prv = [starts(rho, n - 1) for rho, n in items]
                kcat = [jnp.concatenate([gather(prv[i], G), gather(cur[i], G)], axis=0) for i in its]
                vcat = [jnp.concatenate([gather(prv[i], 2 * G), gather(cur[i], 2 * G)], axis=0)
                        for i in its]
            else:
                kcat = [gather(cur[i], G) for i in its]
                vcat = [gather(cur[i], 2 * G) for i in its]
            zq = jnp.zeros_like(qb[0])
            kc0 = 0 if with_prev else blk
            pairs = [(0, 1), (2, 3)]
            s_p = [[_dot(jnp.concatenate([jnp.where(head_sel[h], qb[i], zq) for h in pr], axis=0),
                         kcat[i], _NT) for pr in pairs] for i in its]
            s = [[s_p[i][h // 2][(h % 2) * blk:(h % 2 + 1) * blk]
                  + bias_ref[bi * N_HEADS + h, :, kc0:2 * blk] for h in heads] for i in its]
            mh = [[jnp.max(s[i][h], axis=-1, keepdims=True) for h in heads] for i in its]
            ph = [[jnp.exp2(s[i][h] - mh[i][h]) for h in heads] for i in its]
            lh = [[jnp.sum(ph[i][h], axis=-1, keepdims=True) for h in heads] for i in its]
            pv_p = [[_dot(jnp.concatenate([ph[i][h].astype(BF16) for h in pr], axis=0), vcat[i], _NN)
                     for pr in pairs] for i in its]
            def by_head(parts):
                out = jnp.broadcast_to(parts[N_HEADS - 1], (blk, G))
                for h in reversed(range(N_HEADS - 1)):
                    out = jnp.where(head_sel[h], parts[h], out)
                return out

            acc_b = [by_head([pv_p[i][h // 2][(h % 2) * blk:(h % 2 + 1) * blk] for h in heads])
                     for i in its]
            l_b = [by_head(lh[i]) for i in its]
            m_b = [by_head(mh[i]) for i in its]
            for i in its:
                outs = []
                for e, s0 in enumerate(cur[i]):
                    idx = pl.ds(s0, run)
                    sl = slice(e * run, (e + 1) * run)
                    if first:
                        acc_ref[idx, :] = acc_b[i][sl]
                        l_ref[idx, :] = l_b[i][sl]
                        m_ref[idx, :] = m_b[i][sl]
                    else:
                        m_old = m_ref[idx, :]
                        m_new = jnp.maximum(m_old, m_b[i][sl])
                        w_old = jnp.exp2(m_old - m_new)
                        w_new = jnp.exp2(m_b[i][sl] - m_new)
                        acc_n = acc_ref[idx, :] * w_old + acc_b[i][sl] * w_new
                        l_n = l_ref[idx, :] * w_old + l_b[i][sl] * w_new
                        if last:
                            outs.append(acc_n / l_n)
                        else:
                            acc_ref[idx, :] = acc_n
                            l_ref[idx, :] = l_n
                            m_ref[idx, :] = m_new
                if last:
                    r0 = pl.multiple_of(items[i][1] * blk, blk)
                    o_ref[pl.ds(r0, blk), :] = _mmx_l(from_perm, jnp.concatenate(outs, axis=0))

        def run_blocks(count, index, with_prev, blocks=blocks):
            grp = ATTN_GROUP - 1 if with_prev else ATTN_GROUP

            def group(it, carry):
                blocks([index(grp * it + u) for u in range(grp)], with_prev)
                return carry

            if count >= grp:
                lax.fori_loop(0, count // grp, group, 0)
            rest = count % grp
            if rest:
                blocks([index(count - rest + u) for u in range(rest)], with_prev)

        run_blocks(d, lambda i: (i, 0), False)
        if nb > 1:
            run_blocks(d * (nb - 1), lambda i, nb=nb: (i // (nb - 1), i % (nb - 1) + 1), True)


def _attention(at, batch, seq):
    m = batch * seq
    return pl.pallas_call(
        _attn_kernel,
        grid=(batch,),
        in_specs=[pl.BlockSpec((seq, AT_W), lambda b: (b, 0))],
        out_specs=pl.BlockSpec((seq, D_GROUP), lambda b: (b, 0)),
        out_shape=jax.ShapeDtypeStruct((m, D_GROUP), F32),
        scratch_shapes=[pltpu.VMEM((seq, AT_W), F32)] + [pltpu.VMEM((seq, D_GROUP), F32)] * 3
        + [pltpu.VMEM((len(DILATIONS) * N_HEADS, ATTN_BLK, 2 * ATTN_BLK), F32)],
        compiler_params=pltpu.CompilerParams(dimension_semantics=("arbitrary",),
                                             vmem_limit_bytes=VMEM_LIMIT),
    )(at)


def _ssd_kernel(f_ref, fprev_ref, cw_ref, cb_ref, dtb_ref, vec_ref, ex_ref, y_ref, st_ref):
    c = pl.program_id(1)
    T = SSD_T
    G = D_GROUP
    N = SSD_STATE
    nb = SSD_NB
    R = nb * T
    seqs = range(nb)

    @pl.when(c == 0)
    def _():
        st_ref[...] = jnp.zeros_like(st_ref)

    f = f_ref[...].reshape(R, SD_W)
    xbc_raw = f[:, G:G + SSD_XBC]
    prev8 = [jnp.where(c == 0, 0.0, fprev_ref[b, :, G:G + SSD_XBC]) for b in seqs]
    row8 = _iota((8, 1), 0)
    conv = xbc_raw * cw_ref[3:4, :] + cb_ref[...]
    for j in range(1, 4):
        rolled = pltpu.roll(xbc_raw, j, axis=0)
        pieces = []
        for b in seqs:
            pieces.append(jnp.where(row8 < j, pltpu.roll(prev8[b], j, axis=0), rolled[b * T:b * T + 8]))
            pieces.append(rolled[b * T + 8:(b + 1) * T])
        conv = conv + jnp.concatenate(pieces, axis=0) * cw_ref[3 - j:4 - j, :]
    xbc = _silu(conv)
    xs = xbc[:, 0:G]
    bm = xbc[:, G:G + 2 * N]
    cm = xbc[:, G + 2 * N:G + 4 * N]
    z = f[:, 0:G]

    a_dense, d_dense, norm_w = (vec_ref[i:i + 1, :] for i in range(3))
    dt = _softplus(f[:, G + SSD_XBC:G + SSD_XBC + 128] + dtb_ref[...])
    dt_dense = _mmx_r(dt, ex_ref[...], pieces=2)
    da = dt_dense * a_dense
    xdt = xs * dt_dense
    ltri = (_iota((T, T), 1) <= _iota((T, T), 0))
    ltri_b = ltri.astype(BF16)
    umat = (_iota((T, T), 0) > _iota((T, T), 1)).astype(F32)
    masks = _head_masks()
    rows = [slice(b * T, (b + 1) * T) for b in seqs]

    cs = [_mmx_l(ltri_b, da[rows[b]], pieces=2) for b in seqs]
    scores = [[_mm(cm[rows[b], g * N:(g + 1) * N], bm[rows[b], g * N:(g + 1) * N], _NT)
               for g in range(2)] for b in seqs]
    y_diag = []
    for b in seqs:
        acc = None
        for h in range(N_HEADS):
            da_col = jnp.broadcast_to(da[rows[b], h * HEAD_DIM:h * HEAD_DIM + 1], (T, T))
            seg = _mmx_l(ltri_b, da_col * umat, pieces=2)
            dec = jnp.where(ltri, jnp.exp2(seg), 0.0)
            term = _mm(scores[b][h // 2] * dec, xdt[rows[b]] * masks[h])
            acc = term if acc is None else acc + term
        y_diag.append(acc)

    ys = []
    for b in seqs:
        st = st_ref[b]
        cmb = cm[rows[b]]
        y_off = jnp.concatenate([_mm(cmb[:, 0:N], st[:, 0:N]), _mm(cmb[:, N:2 * N], st[:, N:2 * N])],
                                axis=1) * jnp.exp2(cs[b])
        cs_last = cs[b][T - 1:T, :]
        xd = xdt[rows[b]] * jnp.exp2(cs_last - cs[b])
        bmb = bm[rows[b]]
        st_ref[b] = st * jnp.exp2(cs_last) + jnp.concatenate(
            [_mm(bmb[:, 0:N], xd[:, 0:N], _TN), _mm(bmb[:, N:2 * N], xd[:, N:2 * N], _TN)], axis=1)
        ys.append(y_diag[b] + y_off)

    y = (jnp.concatenate(ys, axis=0) + xs * d_dense) * _silu(z)
    halves = []
    for g in range(2):
        yg = y[:, g * N:(g + 1) * N]
        halves.append(yg * lax.rsqrt(jnp.mean(yg * yg, axis=-1, keepdims=True) + RMS_EPS))
    y_ref[...] = (jnp.concatenate(halves, axis=1) * norm_w).reshape(nb, T, G)


def _ssd(sd, cw, cb, dtb, vec, ex, batch, seq):
    nc = seq // SSD_T
    m = batch * seq
    const = lambda shape: pl.BlockSpec(shape, lambda b, c: (0, 0))
    sd3 = sd.reshape(batch, seq, SD_W)
    out = pl.pallas_call(
        _ssd_kernel,
        grid=(batch // SSD_NB, nc),
        in_specs=[pl.BlockSpec((SSD_NB, SSD_T, SD_W), lambda b, c: (b, c, 0)),
                  pl.BlockSpec((SSD_NB, 8, SD_W),
                               lambda b, c: (b, jnp.maximum(c * (SSD_T // 8) - 1, 0), 0)),
                  const((4, SSD_XBC)), const((1, SSD_XBC)), const((1, 128)), const((8, D_GROUP)),
                  const((128, D_GROUP))],
        out_specs=pl.BlockSpec((SSD_NB, SSD_T, D_GROUP), lambda b, c: (b, c, 0)),
        out_shape=jax.ShapeDtypeStruct((batch, seq, D_GROUP), F32),
        scratch_shapes=[pltpu.VMEM((SSD_NB, SSD_STATE, D_GROUP), F32)],
        compiler_params=pltpu.CompilerParams(dimension_semantics=("parallel", "arbitrary"),
                                             vmem_limit_bytes=VMEM_LIMIT),
    )(sd3, sd3, cw, cb, dtb, vec, ex)
    return out.reshape(m, D_GROUP)


def _hgrn_kernel(f_ref, vec_ref, y_ref, st_ref):
    c = pl.program_id(1)
    T = HG_T
    C = HG_C
    G = D_GROUP
    nblk = T // C

    @pl.when(c == 0)
    def _():
        st_ref[...] = jnp.zeros_like(st_ref)

    lb = vec_ref[0:1, :]
    norm_w = vec_ref[1:2, :]
    forget = lb + (1.0 - lb) * _sigmoid_tail(f_ref[:, G:2 * G])
    q = _silu(f_ref[:, 0:G])
    k = 1.0 - forget
    v = f_ref[:, 2 * G:3 * G]
    grp = 4 * C
    blk_tri = ((_iota((grp, grp), 0) // C == _iota((grp, grp), 1) // C)
               & (_iota((grp, grp), 1) <= _iota((grp, grp), 0))).astype(BF16)
    log_f = jnp.log(forget)
    b = jnp.concatenate([_mmx_l(blk_tri, log_f[r0:r0 + grp], pieces=2) for r0 in range(0, T, grp)], axis=0)
    b2 = b * math.log2(math.e)
    c2 = b2 - jnp.log2(k)

    head_sel = [(_iota((1, G), 1) // HEAD_DIM) == h for h in range(N_HEADS)]
    bo = _same_head().astype(BF16)
    t_idx = _iota((C, 1), 0)
    blocks = lambda z: [z[i * C:(i + 1) * C] for i in range(nblk)]
    bq, qq, kq, vv, b2q, c2q = blocks(b), blocks(q), blocks(k), blocks(v), blocks(b2), blocks(c2)
    b_last = [bq[i][C - 1:C, :] for i in range(nblk)]

    upd = [_dot(_stack_heads(vv[i], head_sel), _stack_heads(kq[i] * jnp.exp(b_last[i] - bq[i]), head_sel),
                _TN) for i in range(nblk)]
    q_in = [qq[i] * jnp.exp(bq[i]) for i in range(nblk)]
    half = C // 2
    o_intra = []
    for i in range(nblk):
        parts = []
        for s in range(C):
            lo = 0 if s < half else half
            e = jnp.exp2(b2q[i][lo:C] - c2q[i][s:s + 1, :])
            parts.append(jnp.where(t_idx[lo:C] >= s, qq[i][lo:C] * e, 0.0))
        z = _mm(jnp.concatenate(parts, axis=0), bo)
        o_lo = z[0:half] * vv[i][0:1, :]
        o_hi = z[half:C] * vv[i][0:1, :]
        for s in range(1, half):
            o_lo = o_lo + z[s * C:s * C + half] * vv[i][s:s + 1, :]
            o_hi = o_hi + z[s * C + half:(s + 1) * C] * vv[i][s:s + 1, :]
        for s in range(half, C):
            r0 = half * C + (s - half) * half
            o_hi = o_hi + z[r0:r0 + half] * vv[i][s:s + 1, :]
        o_intra.append(jnp.concatenate([o_lo, o_hi], axis=0))

    st = st_ref[...]
    outs = []
    for i in range(nblk):
        outs.append(_mm(q_in[i], st, _NT) + o_intra[i])
        st = st * jnp.exp(b_last[i]) + upd[i]
    st_ref[...] = st
    o = jnp.concatenate(outs, axis=0)
    ms = _mmx_r(o * o, bo, pieces=2) * (1.0 / HEAD_DIM)
    y_ref[...] = o * lax.rsqrt(ms + RMS_EPS) * norm_w * _silu(f_ref[:, 3 * G:4 * G])


def _hgrn(hg, vec, batch, seq):
    nc = seq // HG_T
    m = batch * seq
    return pl.pallas_call(
        _hgrn_kernel,
        grid=(batch, nc),
        in_specs=[pl.BlockSpec((HG_T, HG_W), lambda b, c: (b * nc + c, 0)),
                  pl.BlockSpec((8, D_GROUP), lambda b, c: (0, 0))],
        out_specs=pl.BlockSpec((HG_T, D_GROUP), lambda b, c: (b * nc + c, 0)),
        out_shape=jax.ShapeDtypeStruct((m, D_GROUP), F32),
        scratch_shapes=[pltpu.VMEM((D_GROUP, D_GROUP), F32)],
        compiler_params=pltpu.CompilerParams(dimension_semantics=("parallel", "arbitrary"),
                                             vmem_limit_bytes=VMEM_LIMIT),
    )(hg, vec)


def _post_kernel(x_ref, ya_ref, yb_ref, yc_ref, yd_ref, wo_ref, wu_ref, wd_ref, ln_ref, o_ref):
    half = POST_TM // 2
    rows = [slice(0, half), slice(half, POST_TM)]

    def mixed(r):
        acc = ALPHA * x_ref[r, :]
        for i, ref in enumerate((ya_ref, yb_ref, yc_ref, yd_ref)):
            acc = acc + jnp.dot(ref[r, :].astype(BF16), wo_ref[i * D_GROUP:(i + 1) * D_GROUP, :],
                                preferred_element_type=F32)
        return acc

    def ffn_chunk(xb, acc, c0):
        h = jnp.maximum(jnp.dot(xb, wu_ref[:, c0:c0 + FF_CHUNK], preferred_element_type=F32), 0.0)
        return acc + jnp.dot((h * h).astype(BF16), wd_ref[c0:c0 + FF_CHUNK, :],
                             preferred_element_type=F32)

    chunks = list(range(0, D_FF, FF_CHUNK))
    pre_a = mixed(rows[0])
    x1a = _layer_norm(pre_a, ln_ref[0:1, :], ln_ref[1:2, :])
    pre_b = mixed(rows[1])
    xba = x1a.astype(BF16)
    acc_a = ALPHA * x1a
    for c0 in chunks[:2]:
        acc_a = ffn_chunk(xba, acc_a, c0)
    x1b = _layer_norm(pre_b, ln_ref[0:1, :], ln_ref[1:2, :])
    for c0 in chunks[2:]:
        acc_a = ffn_chunk(xba, acc_a, c0)
    xbb = x1b.astype(BF16)
    acc_b = ALPHA * x1b
    for c0 in chunks[:2]:
        acc_b = ffn_chunk(xbb, acc_b, c0)
    o_ref[rows[0], :] = _layer_norm(acc_a, ln_ref[2:3, :], ln_ref[3:4, :])
    for c0 in chunks[2:]:
        acc_b = ffn_chunk(xbb, acc_b, c0)
    o_ref[rows[1], :] = _layer_norm(acc_b, ln_ref[2:3, :], ln_ref[3:4, :])


def _post(x2d, ys, wo, wu, wd, ln):
    m = x2d.shape[0]
    row = lambda w: pl.BlockSpec((POST_TM, w), lambda i: (i, 0))
    res = lambda shape: pl.BlockSpec(shape, lambda i: (0, 0), pipeline_mode=pl.Buffered(1))
    return pl.pallas_call(
        _post_kernel,
        grid=(m // POST_TM,),
        in_specs=[row(D_MODEL)] + [row(D_GROUP)] * 4
        + [res((D_MODEL, D_MODEL)), res((D_MODEL, D_FF)), res((D_FF, D_MODEL)), res((8, D_MODEL))],
        out_specs=row(D_MODEL),
        out_shape=jax.ShapeDtypeStruct((m, D_MODEL), F32),
        compiler_params=pltpu.CompilerParams(dimension_semantics=("parallel",),
                                             vmem_limit_bytes=VMEM_LIMIT),
    )(x2d, *ys, wo, wu, wd, ln)


def _pad_cols(a, width):
    return jnp.pad(a, ((0, 0), (0, width - a.shape[1])))


def _pad_rows(a, top, total):
    return jnp.pad(a, ((top, total - top - a.shape[0]), (0, 0)))


def _rep_heads(v):
    return jnp.repeat(v.astype(F32), HEAD_DIM)[None, :]


def _rows8(rows, width):
    out = jnp.concatenate([r.reshape(1, width).astype(F32) for r in rows], axis=0)
    return jnp.pad(out, ((0, 8 - out.shape[0]), (0, 0)))


def kernel(x, lower_bounds, w_in, w_in_vres, mu_shift, mu_vres, rwkv_w0, rwkv_w2, rwkv_a0, rwkv_a2, rwkv_g2, rwkv_k_k, rwkv_k_a, rwkv_r_k, rwkv_lnx_w, rwkv_lnx_b, rwkv_v0, rwkv_v2, ssd_conv_w, ssd_conv_b, ssd_dt_bias, ssd_A_log, ssd_D, ssd_norm_w, hgrn_norm_w, w_out, ln1_w, ln1_b, w_up, w_down, ln2_w, ln2_b):
    batch, seq, _ = x.shape
    lb = jax.nn.softmax(lower_bounds.astype(F32), axis=0)
    lb = jnp.cumsum(lb, axis=0) - lb[0]
    expand = (jnp.arange(128)[:, None] == (jnp.arange(D_GROUP)[None, :] // HEAD_DIM)).astype(BF16)

    h = x.reshape(batch * seq, D_MODEL)
    v_first = None
    for l in range(DEPTH):
        mu = mu_shift[l][None, :]
        w_vres = None
        if l > 0:
            w_vres = _pad_cols(w_in_vres[l - 1], 128).astype(BF16)
            mu = jnp.concatenate([mu, mu_vres[l - 1][None, :]], axis=1)
        rw, at, sd, hg = _project(h, w_in, l, w_vres)

        vec_rows = [rwkv_w0[l], rwkv_a0[l], rwkv_k_k[l], rwkv_k_a[l], rwkv_lnx_w[l], rwkv_lnx_b[l],
                    rwkv_r_k[l].reshape(-1)]
        if l > 0:
            vec_rows.append(rwkv_v0[l - 1])
        y_a, v_first = _rwkv(
            rw, v_first, _pad_cols(mu, RW_W), _rows8(vec_rows, D_GROUP),
            _pad_rows(rwkv_w2[l], 0, 128), _pad_rows(rwkv_a2[l], 32, 128), _pad_rows(rwkv_g2[l], 64, 128),
            _pad_rows(rwkv_v2[l - 1], 0, 128) if l > 0 else None, batch, seq)
        y_b = _attention(at, batch, seq)
        ssd_vec = _rows8([_rep_heads(-jnp.exp(ssd_A_log[l].astype(F32)) * math.log2(math.e)), _rep_heads(ssd_D[l]),
                          ssd_norm_w[l]], D_GROUP)
        y_c = _ssd(sd, ssd_conv_w[l].astype(F32), ssd_conv_b[l][None, :],
                   _pad_cols(ssd_dt_bias[l][None, :], 128), ssd_vec, expand, batch, seq)
        y_d = _hgrn(hg, _rows8([lb[l], hgrn_norm_w[l]], D_GROUP), batch, seq)
        ln = _rows8([ln1_w[l], ln1_b[l], ln2_w[l], ln2_b[l]], D_MODEL)
        h = _post(h, (y_a, y_b, y_c, y_d), w_out[l].astype(BF16), w_up[l].astype(BF16),
                  w_down[l].astype(BF16), ln)
    return h.reshape(batch, seq, D_MODEL)
```
